```python
import math
import jax
import jax.numpy as jnp
from jax import lax
import numpy as np

D_MODEL = 1024
BATCH = 16
SEQ = 2048
DEPTH = 2
DEC_BATCH = 32
DEC_SEQ = 4
PAST_LEN = 16384
PAGE_SIZE = 128

F32 = jnp.float32
SSM_WIDTH = D_MODEL // 4
SSM_GROUP = 16
SSM_GROUPS = SSM_WIDTH // SSM_GROUP
SSM_STATE = 64
HG_WIDTH = D_MODEL // 4
HG_HEAD_DIM = 64
HG_HEADS = HG_WIDTH // HG_HEAD_DIM
HG_CHUNK = 64
LRU_WIDTH = D_MODEL // 4
LRU_BLOCKS = 4
LRU_BLOCK = LRU_WIDTH // LRU_BLOCKS
CONV_WIDTH = 4
LRU_C = 8.0
DA_HEADS = 4
DA_HEAD_DIM = 64
DA_V_DIM = 2 * DA_HEAD_DIM
DA_QK_WIDTH = DA_HEADS * 2 * DA_HEAD_DIM
DA_WIDTH = DA_HEADS * DA_V_DIM
Q_BLOCK = 128
ROPE_THETA = 10000.0
MASK_VALUE = -1e30
N_BRANCH = 4
MOE_GROUPS = 4
MOE_PER_GROUP = 8
MOE_EXPERTS = MOE_GROUPS * MOE_PER_GROUP
MOE_TOP_K = 2
MOE_HIDDEN = D_MODEL // 8
NORM_EPS = 1e-6
IN_SPLITS = (SSM_WIDTH, HG_WIDTH, HG_WIDTH, HG_WIDTH, HG_WIDTH, LRU_WIDTH, LRU_WIDTH,
             DA_QK_WIDTH, DA_QK_WIDTH, DA_WIDTH, N_BRANCH * D_MODEL)
IN_COLS = sum(IN_SPLITS)

kernel_name = 'hybrid_s5_hgrn2_rglru_diffattn_hmoe_step'


def rmsnorm(x, g):
    xf = x.astype(F32)
    y = xf * lax.rsqrt(jnp.mean(xf * xf, axis=-1, keepdims=True) + NORM_EPS)
    return (y * g.astype(F32)).astype(x.dtype)


def rope(x, pos):
    dh = x.shape[-1]
    half = dh // 2
    inv = 1.0 / (ROPE_THETA ** (jnp.arange(half, dtype=F32) * 2.0 / dh))
    ang = pos.astype(F32)[:, None] * inv[None, :]
    cos = jnp.cos(ang)[None, :, None, None, :]
    sin = jnp.sin(ang)[None, :, None, None, :]
    xf = x.astype(F32)
    x1, x2 = xf[..., :half], xf[..., half:]
    return jnp.concatenate([x1 * cos - x2 * sin, x2 * cos + x1 * sin], axis=-1).astype(x.dtype)


def _real_combine(e1, e2):
    a1, b1 = e1
    a2, b2 = e2
    return a1 * a2, a2 * b1 + b2


def _complex_combine(e1, e2):
    a1r, a1i, b1r, b1i = e1
    a2r, a2i, b2r, b2i = e2
    return (a2r * a1r - a2i * a1i, a2r * a1i + a2i * a1r,
            a2r * b1r - a2i * b1i + b2r, a2r * b1i + a2i * b1r + b2i)


def s5_mixer(u, h0_re, h0_im, lam_re, lam_im, log_dt, b_re, b_im, c_re, c_im, d, w_glu):
    Bsz, T, _ = u.shape
    ug = u.astype(F32).reshape(Bsz, T, SSM_GROUPS, SSM_GROUP)
    lr = lam_re.astype(F32)
    li = lam_im.astype(F32)
    dt = jnp.exp(log_dt.astype(F32))[:, None]
    mag = jnp.exp(lr * dt)
    ar = mag * jnp.cos(li * dt)
    ai = mag * jnp.sin(li * dt)
    den = lr * lr + li * li
    fr = ((ar - 1.0) * lr + ai * li) / den
    fi = (ai * lr - (ar - 1.0) * li) / den
    br = b_re.astype(F32)
    bi = b_im.astype(F32)
    bbr = fr[..., None] * br - fi[..., None] * bi
    bbi = fr[..., None] * bi + fi[..., None] * br
    xr = jnp.einsum('btgi,gpi->btgp', ug, bbr)
    xi = jnp.einsum('btgi,gpi->btgp', ug, bbi)
    h0r = h0_re.astype(F32)
    h0i = h0_im.astype(F32)
    xr = xr.at[:, 0].add(ar * h0r - ai * h0i)
    xi = xi.at[:, 0].add(ar * h0i + ai * h0r)
    shp = xr.shape
    _, _, hr, hi = lax.associative_scan(
        _complex_combine, (jnp.broadcast_to(ar, shp), jnp.broadcast_to(ai, shp), xr, xi), axis=1)
    y = (jnp.einsum('btgp,gip->btgi', hr, c_re.astype(F32))
         - jnp.einsum('btgp,gip->btgi', hi, c_im.astype(F32))
         + d.astype(F32).reshape(SSM_GROUPS, SSM_GROUP) * ug)
    z = jax.nn.gelu(y.reshape(Bsz, T, SSM_WIDTH))
    out = z * jax.nn.sigmoid(z @ w_glu.astype(F32))
    return out.astype(u.dtype), hr[:, -1], hi[:, -1]


def hgrn2_mixer(q, f, i, g, S0, lb, norm_g):
    Bsz, T, _ = q.shape
    H, dk = HG_HEADS, HG_HEAD_DIM

    def heads(t):
        return t.astype(F32).reshape(Bsz, T, H, dk).transpose(0, 2, 1, 3)

    lbh = lb.reshape(H, 1, dk)
    fv = lbh + (1.0 - lbh) * jax.nn.sigmoid(heads(f))
    logf = jnp.log(fv)
    kk = 1.0 - fv
    qh = heads(q)
    vh = heads(i)
    L = math.gcd(T, HG_CHUNK)
    nc = T // L

    def chunks(t):
        return t.reshape(Bsz, H, nc, L, dk).transpose(2, 0, 1, 3, 4)

    mask = jnp.tril(jnp.ones((L, L), dtype=bool))[:, :, None]

    def step(S, xs):
        qc, kc, vc, lf = xs
        b = jnp.cumsum(lf, axis=2)
        o = jnp.einsum('bhld,bhde->bhle', qc * jnp.exp(b), S)
        diff = b[:, :, :, None, :] - b[:, :, None, :, :]
        dec = jnp.where(mask, jnp.exp(jnp.where(mask, diff, 0.0)), 0.0)
        att = jnp.einsum('bhtd,bhsd,bhtsd->bhts', qc, kc, dec)
        o = o + jnp.einsum('bhts,bhse->bhte', att, vc)
        bl = b[:, :, -1:, :]
        S = (jnp.exp(bl[:, :, 0, :])[..., None] * S
             + jnp.einsum('bhsd,bhse->bhde', kc * jnp.exp(bl - b), vc))
        return S, o

    S, o = lax.scan(step, S0.astype(F32), (chunks(qh), chunks(kk), chunks(vh), chunks(logf)))
    o = o.transpose(1, 0, 3, 2, 4).reshape(Bsz, T, H, dk)
    o = rmsnorm(o, norm_g).reshape(Bsz, T, HG_WIDTH) * jax.nn.silu(g.astype(F32))
    return o.astype(q.dtype), S


def rglru_mixer(xr, gate, conv_buf, h0, conv_w, conv_b, wa, ba, wx, bx, lam):
    Bsz, T, W = xr.shape
    xp = jnp.concatenate([conv_buf.astype(xr.dtype), xr], axis=1)
    xc = conv_b.astype(F32)
    for j in range(CONV_WIDTH):
        xc = xc + xp[:, j:j + T].astype(F32) * conv_w[j].astype(F32)
    xb = xc.reshape(Bsz, T, LRU_BLOCKS, LRU_BLOCK)
    r = jax.nn.sigmoid(jnp.einsum('btni,nij->btnj', xb, wa.astype(F32)).reshape(Bsz, T, W) + ba.astype(F32))
    ig = jax.nn.sigmoid(jnp.einsum('btni,nij->btnj', xb, wx.astype(F32)).reshape(Bsz, T, W) + bx.astype(F32))
    log_a = -LRU_C * r * jax.nn.softplus(-lam.astype(F32))
    a = jnp.exp(log_a)
    bt = jnp.sqrt(jnp.maximum(-jnp.expm1(2.0 * log_a), 0.0)) * (ig * xc)
    bt = bt.at[:, 0].add(a[:, 0] * h0.astype(F32))
    _, h = lax.associative_scan(_real_combine, (a, bt), axis=1)
    y = jax.nn.gelu(gate.astype(F32)) * h
    return y.astype(xr.dtype), h[:, -1], xp[:, T:]


def diff_attention(q, k, v, kv_past, pos0, l, lq1, lk1, lq2, lk2, norm_g):
    Bsz, T, _ = q.shape
    pos = pos0 + jnp.arange(T, dtype=jnp.int32)
    qh = rope(q.reshape(Bsz, T, DA_HEADS, 2, DA_HEAD_DIM), pos)
    kh = rope(k.reshape(Bsz, T, DA_HEADS, 2, DA_HEAD_DIM), pos)
    vh = v.reshape(Bsz, T, DA_HEADS, DA_V_DIM)
    if kv_past is None:
        k_all, v_all, k_pos = kh, vh, pos
    else:
        kp, vp = kv_past
        k_all = jnp.concatenate([kp.astype(kh.dtype), kh], axis=1)
        v_all = jnp.concatenate([vp.astype(vh.dtype), vh], axis=1)
        k_pos = jnp.concatenate([jnp.arange(kp.shape[1], dtype=jnp.int32), pos])
    lam_init = 0.8 - 0.6 * math.exp(-0.3 * l)
    lam = (jnp.exp(jnp.sum(lq1.astype(F32) * lk1.astype(F32)))
           - jnp.exp(jnp.sum(lq2.astype(F32) * lk2.astype(F32))) + lam_init)
    scale = DA_HEAD_DIM ** -0.5

    def block(args):
        qb, qpos = args
        s = jnp.einsum('bqhcd,bkhcd->bhcqk', qb, k_all).astype(F32) * scale
        s = jnp.where(k_pos[None, :] <= qpos[:, None], s, MASK_VALUE)
        p = jax.nn.softmax(s, axis=-1)
        w = p[:, :, 0] - lam * p[:, :, 1]
        return jnp.einsum('bhqk,bkhe->bqhe', w.astype(v_all.dtype), v_all)

    if T % Q_BLOCK == 0:
        nb = T // Q_BLOCK
        qb = qh.reshape(Bsz, nb, Q_BLOCK, DA_HEADS, 2, DA_HEAD_DIM).transpose(1, 0, 2, 3, 4, 5)
        o = lax.map(block, (qb, pos.reshape(nb, Q_BLOCK)))
        o = o.transpose(1, 0, 2, 3, 4).reshape(Bsz, T, DA_HEADS, DA_V_DIM)
    else:
        o = block((qh, pos))
    o = rmsnorm(o, norm_g).astype(F32) * (1.0 - lam_init)
    return (o.reshape(Bsz, T, DA_WIDTH).astype(q.dtype),
            kh.reshape(Bsz, T, DA_HEADS, 2 * DA_HEAD_DIM), vh)


def hier_moe(x, w_grp, b_grp, w_exp, b_exp, w_gate, w_up, w_down):
    Bsz, T, D = x.shape
    xt = x.reshape(Bsz * T, D)
    gl = (xt @ w_grp).astype(F32) + b_grp.astype(F32)
    gp = jax.nn.softmax(gl, axis=-1)
    g_w, g_i = lax.top_k(gp, 1)
    el = ((xt @ w_exp).astype(F32) + b_exp.astype(F32)).reshape(-1, MOE_GROUPS, MOE_PER_GROUP)
    el_sel = jnp.einsum('ng,nge->ne', jax.nn.one_hot(g_i[:, 0], MOE_GROUPS, dtype=F32), el)
    e_l, e_i = lax.top_k(el_sel, MOE_TOP_K)
    wts = g_w * jax.nn.softmax(e_l, axis=-1)
    gates = jnp.sum(jax.nn.one_hot(g_i * MOE_PER_GROUP + e_i, MOE_EXPERTS, dtype=F32) * wts[..., None], axis=1)
    hid = jax.nn.silu(jnp.einsum('nd,edf->nef', xt, w_gate)) * jnp.einsum('nd,edf->nef', xt, w_up)
    y = jnp.einsum('nef,efd->nd', hid * gates[:, :, None].astype(hid.dtype), w_down)
    return y.reshape(Bsz, T, D)


def hybrid_layer(x, l, pos0, kv_past, st, prm):
    Bsz, T, _ = x.shape
    h = rmsnorm(x, prm['norm_mix'][l])
    z = h @ prm['w_in'][l]
    offs = [int(o) for o in np.cumsum(IN_SPLITS)[:-1]]
    (u_a, q_b, f_b, i_b, g_b, x_c, g_c, q_d, k_d, v_d, gts) = jnp.split(z, offs, axis=-1)
    y_a, ssm_re, ssm_im = s5_mixer(u_a, st[0], st[1], prm['ssm_lambda_re'][l], prm['ssm_lambda_im'][l],
                                   prm['ssm_log_dt'][l], prm['ssm_b_re'][l], prm['ssm_b_im'][l],
                                   prm['ssm_c_re'][l], prm['ssm_c_im'][l], prm['ssm_d'][l], prm['ssm_w_glu'][l])
    p_lb = jax.nn.softmax(prm['hg_lb_logits'].astype(F32), axis=0)
    lb = jnp.cumsum(p_lb, axis=0)[l] - p_lb[0]
    y_b, S = hgrn2_mixer(q_b, f_b, i_b, g_b, st[2], lb, prm['hg_norm'][l])
    y_c, lru_h, conv_buf = rglru_mixer(x_c, g_c, st[4], st[3], prm['lru_conv_w'][l], prm['lru_conv_b'][l],
                                       prm['lru_wa'][l], prm['lru_ba'][l], prm['lru_wx'][l], prm['lru_bx'][l],
                                       prm['lru_lambda'][l])
    y_d, k_new, v_new = diff_attention(q_d, k_d, v_d, kv_past, pos0, l, prm['diff_lq1'][l], prm['diff_lk1'][l],
                                       prm['diff_lq2'][l], prm['diff_lk2'][l], prm['diff_norm'][l])
    gs = jax.nn.sigmoid(gts.reshape(Bsz, T, N_BRANCH, D_MODEL))
    merged = (gs[:, :, 0] * (y_a @ prm['w_br_a'][l]) + gs[:, :, 1] * (y_b @ prm['w_br_b'][l])
              + gs[:, :, 2] * (y_c @ prm['w_br_c'][l]) + gs[:, :, 3] * (y_d @ prm['w_br_d'][l]))
    x = x + merged @ prm['w_out'][l]
    x = x + hier_moe(rmsnorm(x, prm['norm_ffn'][l]), prm['moe_w_grp'][l], prm['moe_b_grp'][l],
                     prm['moe_w_exp'][l], prm['moe_b_exp'][l], prm['moe_w_gate'][l],
                     prm['moe_w_up'][l], prm['moe_w_down'][l])
    return x, k_new, v_new, (ssm_re, ssm_im, S, lru_h, conv_buf)


def run_trunk(x, pos0, states, cache_k, cache_v, page_table, prm):
    ks, vs, sts = [], [], []
    for l in range(DEPTH):
        if page_table is None:
            kv = None
        else:
            nb, npg = page_table.shape
            kp = cache_k[l, page_table].reshape(nb, npg * PAGE_SIZE, DA_HEADS, 2, DA_HEAD_DIM)
            vp = cache_v[l, page_table].reshape(nb, npg * PAGE_SIZE, DA_HEADS, DA_V_DIM)
            kv = (kp, vp)
        x, k_l, v_l, st_l = hybrid_layer(x, l, pos0, kv, states[l], prm)
        ks.append(k_l)
        vs.append(v_l)
        sts.append(st_l)
    y = rmsnorm(x, prm['norm_final'])
    stacked = [jnp.stack([s[j] for s in sts]) for j in range(5)]
    return y, jnp.stack(ks), jnp.stack(vs), stacked


def setup_inputs(seed: int = 0) -> dict:
    key = jax.random.key(seed)
    ks = iter(jax.random.split(key, 64))

    def nrm(shape, scale):
        return jax.random.normal(next(ks), shape, F32) * scale

    def gain(shape):
        return 1.0 + nrm(shape, 0.02)

    n_pages = PAST_LEN // PAGE_SIZE
    n_pool = (DEC_BATCH * n_pages * 5) // 4
    page_table = jax.random.permutation(next(ks), n_pool)[:DEC_BATCH * n_pages].reshape(
        DEC_BATCH, n_pages).astype(jnp.int32)
    lru_u = jax.random.uniform(next(ks), (DEPTH, LRU_WIDTH), F32, 0.9, 0.999)
    lru_a = lru_u ** (1.0 / LRU_C)
    return {
        'x_prompt': nrm((BATCH, SEQ, D_MODEL), 1.0),
        'x_sample': nrm((DEC_BATCH, DEC_SEQ, D_MODEL), 1.0),
        'cache_k': nrm((DEPTH, n_pool, PAGE_SIZE, DA_HEADS, 2 * DA_HEAD_DIM), 1.0),
        'cache_v': nrm((DEPTH, n_pool, PAGE_SIZE, DA_HEADS, DA_V_DIM), 1.0),
        'page_table': page_table,
        'state_ssm_re': nrm((DEPTH, DEC_BATCH, SSM_GROUPS, SSM_STATE), 1.0),
        'state_ssm_im': nrm((DEPTH, DEC_BATCH, SSM_GROUPS, SSM_STATE), 1.0),
        'state_hgrn': nrm((DEPTH, DEC_BATCH, HG_HEADS, HG_HEAD_DIM, HG_HEAD_DIM), 0.5),
        'state_lru': nrm((DEPTH, DEC_BATCH, LRU_WIDTH), 0.5),
        'state_conv': nrm((DEPTH, DEC_BATCH, CONV_WIDTH - 1, LRU_WIDTH), 1.0),
        'norm_mix': gain((DEPTH, D_MODEL)),
        'w_in': nrm((DEPTH, D_MODEL, IN_COLS), D_MODEL ** -0.5),
        'ssm_lambda_re': -0.5 + nrm((DEPTH, SSM_GROUPS, SSM_STATE), 0.01),
        'ssm_lambda_im': math.pi * jnp.arange(SSM_STATE, dtype=F32) + nrm((DEPTH, SSM_GROUPS, SSM_STATE), 0.01),
        'ssm_log_dt': jax.random.uniform(next(ks), (DEPTH, SSM_GROUPS), F32, math.log(1e-3), math.log(1e-1)),
        'ssm_b_re': nrm((DEPTH, SSM_GROUPS, SSM_STATE, SSM_GROUP), (2 * SSM_GROUP) ** -0.5),
        'ssm_b_im': nrm((DEPTH, SSM_GROUPS, SSM_STATE, SSM_GROUP), (2 * SSM_GROUP) ** -0.5),
        'ssm_c_re': nrm((DEPTH, SSM_GROUPS, SSM_GROUP, SSM_STATE), SSM_STATE ** -0.5),
        'ssm_c_im': nrm((DEPTH, SSM_GROUPS, SSM_GROUP, SSM_STATE), SSM_STATE ** -0.5),
        'ssm_d': nrm((DEPTH, SSM_WIDTH), 1.0),
        'ssm_w_glu': nrm((DEPTH, SSM_WIDTH, SSM_WIDTH), SSM_WIDTH ** -0.5),
        'hg_lb_logits': nrm((DEPTH, HG_WIDTH), 0.5),
        'hg_norm': gain((DEPTH, HG_HEAD_DIM)),
        'lru_conv_w': nrm((DEPTH, CONV_WIDTH, LRU_WIDTH), CONV_WIDTH ** -0.5),
        'lru_conv_b': nrm((DEPTH, LRU_WIDTH), 0.01),
        'lru_wa': nrm((DEPTH, LRU_BLOCKS, LRU_BLOCK, LRU_BLOCK), LRU_BLOCK ** -0.5),
        'lru_ba': nrm((DEPTH, LRU_WIDTH), 0.01),
        'lru_wx': nrm((DEPTH, LRU_BLOCKS, LRU_BLOCK, LRU_BLOCK), LRU_BLOCK ** -0.5),
        'lru_bx': nrm((DEPTH, LRU_WIDTH), 0.01),
        'lru_lambda': jnp.log(lru_a) - jnp.log1p(-lru_a),
        'diff_lq1': nrm((DEPTH, DA_HEAD_DIM), 0.1),
        'diff_lk1': nrm((DEPTH, DA_HEAD_DIM), 0.1),
        'diff_lq2': nrm((DEPTH, DA_HEAD_DIM), 0.1),
        'diff_lk2': nrm((DEPTH, DA_HEAD_DIM), 0.1),
        'diff_norm': gain((DEPTH, DA_V_DIM)),
        'w_br_a': nrm((DEPTH, SSM_WIDTH, D_MODEL), SSM_WIDTH ** -0.5),
        'w_br_b': nrm((DEPTH, HG_WIDTH, D_MODEL), HG_WIDTH ** -0.5),
        'w_br_c': nrm((DEPTH, LRU_WIDTH, D_MODEL), LRU_WIDTH ** -0.5),
        'w_br_d': nrm((DEPTH, DA_WIDTH, D_MODEL), DA_WIDTH ** -0.5),
        'w_out': nrm((DEPTH, D_MODEL, D_MODEL), D_MODEL ** -0.5),
        'norm_ffn': gain((DEPTH, D_MODEL)),
        'moe_w_grp': nrm((DEPTH, D_MODEL, MOE_GROUPS), D_MODEL ** -0.5),
        'moe_b_grp': nrm((DEPTH, MOE_GROUPS), 0.01),
        'moe_w_exp': nrm((DEPTH, D_MODEL, MOE_EXPERTS), D_MODEL ** -0.5),
        'moe_b_exp': nrm((DEPTH, MOE_EXPERTS), 0.01),
        'moe_w_gate': nrm((DEPTH, MOE_EXPERTS, D_MODEL, MOE_HIDDEN), D_MODEL ** -0.5),
        'moe_w_up': nrm((DEPTH, MOE_EXPERTS, D_MODEL, MOE_HIDDEN), D_MODEL ** -0.5),
        'moe_w_down': nrm((DEPTH, MOE_EXPERTS, MOE_HIDDEN, D_MODEL), MOE_HIDDEN ** -0.5),
        'norm_final': gain((D_MODEL,)),
    }


def reference(x_prompt, x_sample, cache_k, cache_v, page_table, state_ssm_re, state_ssm_im, state_hgrn,
              state_lru, state_conv, norm_mix, w_in, ssm_lambda_re, ssm_lambda_im, ssm_log_dt, ssm_b_re,
              ssm_b_im, ssm_c_re, ssm_c_im, ssm_d, ssm_w_glu, hg_lb_logits, hg_norm, lru_conv_w, lru_conv_b,
              lru_wa, lru_ba, lru_wx, lru_bx, lru_lambda, diff_lq1, diff_lk1, diff_lq2, diff_lk2, diff_norm,
              w_br_a, w_br_b, w_br_c, w_br_d, w_out, norm_ffn, moe_w_grp, moe_b_grp, moe_w_exp, moe_b_exp,
              moe_w_gate, moe_w_up, moe_w_down, norm_final):
    prm = {
        'norm_mix': norm_mix, 'w_in': w_in,
        'ssm_lambda_re': ssm_lambda_re, 'ssm_lambda_im': ssm_lambda_im, 'ssm_log_dt': ssm_log_dt,
        'ssm_b_re': ssm_b_re, 'ssm_b_im': ssm_b_im, 'ssm_c_re': ssm_c_re, 'ssm_c_im': ssm_c_im,
        'ssm_d': ssm_d, 'ssm_w_glu': ssm_w_glu,
        'hg_lb_logits': hg_lb_logits, 'hg_norm': hg_norm,
        'lru_conv_w': lru_conv_w, 'lru_conv_b': lru_conv_b, 'lru_wa': lru_wa, 'lru_ba': lru_ba,
        'lru_wx': lru_wx, 'lru_bx': lru_bx, 'lru_lambda': lru_lambda,
        'diff_lq1': diff_lq1, 'diff_lk1': diff_lk1, 'diff_lq2': diff_lq2, 'diff_lk2': diff_lk2,
        'diff_norm': diff_norm,
        'w_br_a': w_br_a, 'w_br_b': w_br_b, 'w_br_c': w_br_c, 'w_br_d': w_br_d, 'w_out': w_out,
        'norm_ffn': norm_ffn, 'moe_w_grp': moe_w_grp, 'moe_b_grp': moe_b_grp, 'moe_w_exp': moe_w_exp,
        'moe_b_exp': moe_b_exp, 'moe_w_gate': moe_w_gate, 'moe_w_up': moe_w_up, 'moe_w_down': moe_w_down,
        'norm_final': norm_final,
    }
    Bp = x_prompt.shape[0]
    zero_states = [(jnp.zeros((Bp, SSM_GROUPS, SSM_STATE), F32),
                    jnp.zeros((Bp, SSM_GROUPS, SSM_STATE), F32),
                    jnp.zeros((Bp, HG_HEADS, HG_HEAD_DIM, HG_HEAD_DIM), F32),
                    jnp.zeros((Bp, LRU_WIDTH), F32),
                    jnp.zeros((Bp, CONV_WIDTH - 1, LRU_WIDTH), x_prompt.dtype)) for _ in range(DEPTH)]
    y_prompt, k_prompt, v_prompt, st_p = run_trunk(x_prompt, 0, zero_states, None, None, None, prm)
    past_len = page_table.shape[1] * PAGE_SIZE
    sample_states = [(state_ssm_re[l], state_ssm_im[l], state_hgrn[l], state_lru[l], state_conv[l])
                     for l in range(DEPTH)]
    y_sample, k_sample, v_sample, st_s = run_trunk(x_sample, past_len, sample_states, cache_k, cache_v,
                                                   page_table, prm)
    return (y_prompt, y_sample, k_prompt, v_prompt, k_sample, v_sample,
            st_p[0], st_p[1], st_s[0], st_s[1], st_p[2], st_s[2], st_p[3], st_s[3], st_p[4], st_s[4])
```

```python
import functools
import math

import jax
import jax.numpy as jnp
from jax import lax
from jax.experimental import pallas as pl
from jax.experimental.pallas import tpu as pltpu

F32 = jnp.float32
BF16 = jnp.bfloat16

D_MODEL = 1024
DEPTH = 2
PAGE_SIZE = 128
SSM_WIDTH = 256
SSM_GROUP = 16
SSM_GROUPS = 16
SSM_STATE = 64
HG_WIDTH = 256
HG_HEAD_DIM = 64
HG_HEADS = 4
LRU_WIDTH = 256
LRU_BLOCKS = 4
LRU_BLOCK = 64
CONV_WIDTH = 4
LRU_C = 8.0
DA_HEADS = 4
DA_HEAD_DIM = 64
DA_V_DIM = 128
DA_QK_WIDTH = 512
DA_WIDTH = 512
ROPE_THETA = 10000.0
MASK_VALUE = -1e30
N_BRANCH = 4
MOE_GROUPS = 4
MOE_PER_GROUP = 8
MOE_EXPERTS = 32
MOE_HIDDEN = 128
NORM_EPS = 1e-6
MIX_COLS = 3328
ROUTER_LANES = 128
VMEM_LIMIT = 56 * 1024 * 1024
HI = lax.Precision.HIGHEST


def _cparams(sem):
    return pltpu.CompilerParams(dimension_semantics=sem, vmem_limit_bytes=VMEM_LIMIT)


def _rms(x, g):
    return x * lax.rsqrt(jnp.mean(x * x, axis=-1, keepdims=True) + NORM_EPS) * g


def _dot(a, b):
    return jnp.dot(a, b, preferred_element_type=F32)


def _dot_nt(a, b):
    return lax.dot_general(a, b, (((1,), (1,)), ((), ())), preferred_element_type=F32)


def _dot_tn(a, b):
    return lax.dot_general(a, b, (((0,), (0,)), ((), ())), preferred_element_type=F32)


def _split_dot(a, b_bf16, terms):
    out = None
    rem = a
    for _ in range(terms):
        piece = rem.astype(BF16)
        part = _dot(piece, b_bf16)
        out = part if out is None else out + part
        rem = rem - piece.astype(F32)
    return out


def _head_ones(width, head):
    r = lax.broadcasted_iota(jnp.int32, (width, width), 0) // head
    c = lax.broadcasted_iota(jnp.int32, (width, width), 1) // head
    return r == c


def _inproj_body(x_ref, g_ref, w_ref, cos_ref, sin_ref,
                 ua_ref, hg_ref, lru_ref, q_ref, k_ref, kb_ref, v_ref, vb_ref):
    h = _rms(x_ref[...], g_ref[...]).astype(BF16)

    def mm(a, b):
        return _dot(h, w_ref[:, a:b])

    ua_ref[...] = mm(0, 256)
    hg_ref[...] = mm(256, 1280)
    lru_ref[...] = mm(1280, 1792)
    cos = cos_ref[...]
    sin = sin_ref[...]
    lane = lax.broadcasted_iota(jnp.int32, cos.shape, 1)
    first = (lane % DA_HEAD_DIM) < (DA_HEAD_DIM // 2)

    def rope(z):
        rot = jnp.where(first, -pltpu.roll(z, DA_QK_WIDTH - DA_HEAD_DIM // 2, 1),
                        pltpu.roll(z, DA_HEAD_DIM // 2, 1))
        return z * cos + rot * sin

    q = rope(mm(1792, 2304))
    q_ref[...] = (q * (DA_HEAD_DIM ** -0.5)).astype(BF16)
    k = rope(mm(2304, 2816))
    k_ref[...] = k
    kb_ref[...] = k.astype(BF16)
    v = mm(2816, 3328)
    v_ref[...] = v
    vb_ref[...] = v.astype(BF16)


def _inproj(x, g, w, cos, sin, tm):
    n = x.shape[0]
    ntab = cos.shape[0] // tm
    row = lambda i: (i, 0)
    fixed = lambda i: (0, 0)
    tab = lambda i: (i % ntab, 0)
    widths = (256, 1024, 512, 512, 512, 512, 512, 512)
    dtypes = (F32, F32, F32, BF16, F32, BF16, F32, BF16)
    return pl.pallas_call(
        _inproj_body,
        grid=(n // tm,),
        in_specs=[pl.BlockSpec((tm, D_MODEL), row), pl.BlockSpec((1, D_MODEL), fixed),
                  pl.BlockSpec((D_MODEL, MIX_COLS), fixed),
                  pl.BlockSpec((tm, 512), tab), pl.BlockSpec((tm, 512), tab)],
        out_specs=[pl.BlockSpec((tm, wd), row) for wd in widths],
        out_shape=[jax.ShapeDtypeStruct((n, wd), dt) for wd, dt in zip(widths, dtypes)],
        compiler_params=_cparams(("parallel",)),
        name="inproj",
    )(x, g, w, cos, sin)


def _s5_body(u_ref, m_ref, p_ref, q_ref, a_ref, h0_ref, y_ref, hf_ref, pu_scr, hs_scr, *, nb, nc):
    u = u_ref[0].astype(BF16)
    pu_scr[...] = _dot(u, p_ref[0])
    ar2 = a_ref[0, 0:1, :]
    ai2 = a_ref[0, 1:2, :]

    def step(c, h):
        r0 = pl.multiple_of(c * nb, nb)
        hs_scr[pl.ds(r0, nb), :] = h
        return ar2 * h + ai2 * pltpu.roll(h, SSM_STATE, 1) + pu_scr[pl.ds(r0, nb), :]

    hf_ref[0] = lax.fori_loop(0, nc, step, h0_ref[0])
    y_ref[0] = _dot(u, m_ref[0]) + _dot(hs_scr[...].astype(BF16), q_ref[0])


def _s5(u_g, m, p, q, a, h0, nb, nc):
    g, rows, lw = u_g.shape
    blk = lambda i: (i, 0, 0)
    return pl.pallas_call(
        functools.partial(_s5_body, nb=nb, nc=nc),
        grid=(g,),
        in_specs=[pl.BlockSpec((1, rows, lw), blk), pl.BlockSpec((1, lw, lw), blk),
                  pl.BlockSpec((1, lw, 128), blk), pl.BlockSpec((1, 128, lw), blk),
                  pl.BlockSpec((1, 2, 128), blk), pl.BlockSpec((1, nb, 128), blk)],
        out_specs=[pl.BlockSpec((1, rows, lw), blk), pl.BlockSpec((1, nb, 128), blk)],
        out_shape=[jax.ShapeDtypeStruct((g, rows, lw), F32), jax.ShapeDtypeStruct((g, nb, 128), F32)],
        scratch_shapes=[pltpu.VMEM((rows, 128), F32), pltpu.VMEM((rows, 128), F32)],
        compiler_params=_cparams(("parallel",)),
        name="s5",
    )(u_g, m, p, q, a, h0)


def _s5_weights(lam_re, lam_im, log_dt, b_re, b_im, c_re, c_im, L):
    G, P, J = SSM_GROUPS, SSM_STATE, SSM_GROUP
    lr, li = lam_re.astype(F32), lam_im.astype(F32)
    dt = jnp.exp(log_dt.astype(F32))[:, None]
    mag = jnp.exp(lr * dt)
    ar = mag * jnp.cos(li * dt)
    ai = mag * jnp.sin(li * dt)
    den = lr * lr + li * li
    fr = ((ar - 1.0) * lr + ai * li) / den
    fi = (ai * lr - (ar - 1.0) * li) / den
    br, bi = b_re.astype(F32), b_im.astype(F32)
    bbr = fr[..., None] * br - fi[..., None] * bi
    bbi = fr[..., None] * bi + fi[..., None] * br
    tau = jnp.arange(L + 1, dtype=F32)[:, None, None]
    pmag = jnp.exp(lr * dt * tau)
    pr = pmag * jnp.cos(li * dt * tau)
    pi = pmag * jnp.sin(li * dt * tau)
    t1r = pr[..., None] * bbr - pi[..., None] * bbi
    t1i = pr[..., None] * bbi + pi[..., None] * bbr
    cr, ci = c_re.astype(F32), c_im.astype(F32)
    kt = (jnp.einsum('gip,tgpj->tgij', cr, t1r, precision=HI)
          - jnp.einsum('gip,tgpj->tgij', ci, t1i, precision=HI))
    s = jnp.arange(L)[:, None]
    t = jnp.arange(L)[None, :]
    kst = kt[jnp.clip(t - s, 0, L)]
    kst = jnp.where((t >= s)[:, :, None, None, None], kst, 0.0)
    m = kst.transpose(2, 0, 4, 1, 3).reshape(G, L * J, L * J)
    rev = L - 1 - jnp.arange(L)
    pmat = jnp.concatenate([t1r[rev].transpose(1, 0, 3, 2), t1i[rev].transpose(1, 0, 3, 2)],
                           axis=-1).reshape(G, L * J, 2 * P)
    car = cr[None] * pr[1:, :, None, :] - ci[None] * pi[1:, :, None, :]
    cai = cr[None] * pi[1:, :, None, :] + ci[None] * pr[1:, :, None, :]
    qmat = jnp.concatenate([car.transpose(1, 3, 0, 2), -cai.transpose(1, 3, 0, 2)],
                           axis=1).reshape(G, 2 * P, L * J)
    a2 = jnp.stack([jnp.concatenate([pr[L], pr[L]], -1), jnp.concatenate([-pi[L], pi[L]], -1)], axis=1)
    return m.astype(BF16), pmat.astype(BF16), qmat.astype(BF16), a2


def _s5_mixer(u, h0_re, h0_im, wts, L):
    B, T, _ = u.shape
    nc = T // L
    G, J = SSM_GROUPS, SSM_GROUP
    u_g = u.reshape(B, nc, L, G, J).transpose(3, 1, 0, 2, 4).reshape(G, nc * B, L * J)
    h0 = jnp.concatenate([h0_re, h0_im], axis=-1).transpose(1, 0, 2)
    y_g, hf = _s5(u_g, *wts, h0, B, nc)
    y = y_g.reshape(G, nc, B, L, J).transpose(2, 1, 3, 0, 4).reshape(B, T, SSM_WIDTH)
    hf = hf.transpose(1, 0, 2)
    return y, hf[..., :SSM_STATE], hf[..., SSM_STATE:]


def _hgrn_body(hg_ref, s0_ref, lb_ref, ng_ref, y_ref, sf_ref,
               st_scr, k_scr, b_scr, v_scr, w_scr, *, tb, c, t_valid):
    j = pl.program_id(1)
    W = HG_WIDTH

    @pl.when(j == 0)
    def _():
        st_scr[...] = s0_ref[0]
        k_scr[0:c, :] = jnp.zeros((c, W), F32)
        b_scr[0:c, :] = jnp.zeros((c, W), F32)
        v_scr[0:c, :] = jnp.zeros((c, W), F32)

    q = hg_ref[0, :, 0:W]
    lb = lb_ref[...]
    fv = lb + (1.0 - lb) * jax.nn.sigmoid(hg_ref[0, :, W:2 * W])
    logf = jnp.log(fv)
    kk = 1.0 - fv
    v = hg_ref[0, :, 2 * W:3 * W]
    row = lax.broadcasted_iota(jnp.int32, (tb, W), 0)
    if t_valid < tb:
        valid = row < t_valid
        logf = jnp.where(valid, logf, 0.0)
        kk = jnp.where(valid, kk, 0.0)
    ri = lax.broadcasted_iota(jnp.int32, (tb, tb), 0)
    ci = lax.broadcasted_iota(jnp.int32, (tb, tb), 1)
    tril = ((ri // c == ci // c) & (ci <= ri)).astype(BF16)
    b = _split_dot_lhs_exact(tril, logf)
    k_scr[c:c + tb, :] = kk
    b_scr[c:c + tb, :] = b
    v_scr[c:c + tb, :] = v
    rin = row % c
    for d in range(c):
        ksh = k_scr[c - d:c - d + tb, :]
        bsh = b_scr[c - d:c - d + tb, :]
        w = jnp.where(rin >= d, q * ksh * jnp.exp(b - bsh), 0.0)
        w_scr[d * tb:(d + 1) * tb, :] = w.astype(BF16)
    ones_bd = _head_ones(W, HG_HEAD_DIM).astype(BF16)
    att = _dot(w_scr[...], ones_bd)
    o = att[0:tb] * v
    for d in range(1, c):
        o = o + att[d * tb:(d + 1) * tb] * v_scr[c - d:c - d + tb, :]
    bd = _head_ones(W, HG_HEAD_DIM)
    outs = []
    for ch in range(tb // c):
        sl = slice(ch * c, (ch + 1) * c)
        bc = b[sl]
        bl = bc[c - 1:c, :]
        st = st_scr[...]
        outs.append(_dot_nt((q[sl] * jnp.exp(bc)).astype(BF16), st.astype(BF16)))
        khat = (kk[sl] * jnp.exp(bl - bc)).astype(BF16)
        upd = _dot_tn(v[sl].astype(BF16), khat)
        st_scr[...] = st * jnp.exp(bl) + jnp.where(bd, upd, 0.0)
    o = o + jnp.concatenate(outs, axis=0) if len(outs) > 1 else o + outs[0]
    ms = _split_dot(o * o, ones_bd, 2) * (1.0 / HG_HEAD_DIM)
    y = o * lax.rsqrt(ms + NORM_EPS) * ng_ref[...]
    y_ref[0] = y * jax.nn.silu(hg_ref[0, :, 3 * W:4 * W])
    sf_ref[0] = st_scr[...]


def _split_dot_lhs_exact(a_bf16, b):
    out = None
    rem = b
    for _ in range(3):
        piece = rem.astype(BF16)
        part = _dot(a_bf16, piece)
        out = part if out is None else out + part
        rem = rem - piece.astype(F32)
    return out


def _hgrn(hg, s0t, lb, ng, tb, c, t_valid):
    B, T, _ = hg.shape
    W = HG_WIDTH
    return pl.pallas_call(
        functools.partial(_hgrn_body, tb=tb, c=c, t_valid=t_valid),
        grid=(B, T // tb),
        in_specs=[pl.BlockSpec((1, tb, 4 * W), lambda b, j: (b, j, 0)),
                  pl.BlockSpec((1, W, W), lambda b, j: (b, 0, 0)),
                  pl.BlockSpec((1, W), lambda b, j: (0, 0)),
                  pl.BlockSpec((1, W), lambda b, j: (0, 0))],
        out_specs=[pl.BlockSpec((1, tb, W), lambda b, j: (b, j, 0)),
                   pl.BlockSpec((1, W, W), lambda b, j: (b, 0, 0))],
        out_shape=[jax.ShapeDtypeStruct((B, T, W), F32), jax.ShapeDtypeStruct((B, W, W), F32)],
        scratch_shapes=[pltpu.VMEM((W, W), F32), pltpu.VMEM((c + tb, W), F32),
                        pltpu.VMEM((c + tb, W), F32), pltpu.VMEM((c + tb, W), F32),
                        pltpu.VMEM((c * tb, W), BF16)],
        compiler_params=_cparams(("parallel", "arbitrary")),
        name="hgrn2",
    )(hg, s0t, lb, ng)


def _hgrn_state_to_t(s):
    B = s.shape[0]
    eye = jnp.eye(HG_HEADS, dtype=s.dtype)
    return jnp.einsum('bhde,hg->bhegd', s, eye).reshape(B, HG_WIDTH, HG_WIDTH)


def _hgrn_state_from_t(st):
    B = st.shape[0]
    s5 = st.reshape(B, HG_HEADS, HG_HEAD_DIM, HG_HEADS, HG_HEAD_DIM)
    idx = jnp.arange(HG_HEADS)
    return s5[:, idx, :, idx, :].transpose(1, 0, 3, 2)


def _lru_body(x_ref, c0_ref, h0_ref, cw_ref, cb_ref, wax_ref, bax_ref, nsp_ref, y_ref, hl_ref,
              xs_scr, hc_scr, *, tb, r_last):
    j = pl.program_id(1)
    W = LRU_WIDTH

    @pl.when(j == 0)
    def _():
        xs_scr[0:8, :] = c0_ref[0]
        hc_scr[...] = h0_ref[0]

    x = x_ref[0, :, 0:W]
    xs_scr[8:8 + tb, :] = x
    xc = cb_ref[...] + x * cw_ref[3:4, :]
    for jj in range(CONV_WIDTH - 1):
        xc = xc + xs_scr[5 + jj:5 + jj + tb, :] * cw_ref[jj:jj + 1, :]
    tail = xs_scr[tb:tb + 8, :]
    xs_scr[0:8, :] = tail
    rg = _dot(xc.astype(BF16), wax_ref[...]) + bax_ref[...]
    r = jax.nn.sigmoid(rg[:, 0:W])
    ig = jax.nn.sigmoid(rg[:, W:2 * W])
    log_a = nsp_ref[...] * r
    a = jnp.exp(log_a)
    bt = jnp.sqrt(jnp.maximum(1.0 - a * a, 0.0)) * (ig * xc)
    row = lax.broadcasted_iota(jnp.int32, (tb, W), 0)
    k = 1
    while k < tb:
        keep = row >= k
        a_sh = jnp.where(keep, pltpu.roll(a, k, 0), 1.0)
        b_sh = jnp.where(keep, pltpu.roll(bt, k, 0), 0.0)
        bt = a * b_sh + bt
        a = a * a_sh
        k *= 2
    h = a * hc_scr[...] + bt
    y_ref[0] = jax.nn.gelu(x_ref[0, :, W:2 * W]) * h
    hc = h[r_last:r_last + 1, :]
    hc_scr[...] = hc
    hl_ref[0] = hc


def _lru(xg, c0, h0, cw, cb, wax, bax, nsp, tb, t_valid):
    B, T, _ = xg.shape
    W = LRU_WIDTH
    fixed = lambda b, j: (0, 0)
    return pl.pallas_call(
        functools.partial(_lru_body, tb=tb, r_last=(t_valid - 1) % tb),
        grid=(B, T // tb),
        in_specs=[pl.BlockSpec((1, tb, 2 * W), lambda b, j: (b, j, 0)),
                  pl.BlockSpec((1, 8, W), lambda b, j: (b, 0, 0)),
                  pl.BlockSpec((1, 1, W), lambda b, j: (b, 0, 0)),
                  pl.BlockSpec((CONV_WIDTH, W), fixed), pl.BlockSpec((1, W), fixed),
                  pl.BlockSpec((W, 2 * W), fixed), pl.BlockSpec((1, 2 * W), fixed),
                  pl.BlockSpec((1, W), fixed)],
        out_specs=[pl.BlockSpec((1, tb, W), lambda b, j: (b, j, 0)),
                   pl.BlockSpec((1, 1, W), lambda b, j: (b, 0, 0))],
        out_shape=[jax.ShapeDtypeStruct((B, T, W), F32), jax.ShapeDtypeStruct((B, 1, W), F32)],
        scratch_shapes=[pltpu.VMEM((8 + tb, W), F32), pltpu.VMEM((1, W), F32)],
        compiler_params=_cparams(("parallel", "arbitrary")),
        name="rglru",
    )(xg, c0, h0, cw, cb, wax, bax, nsp)


def _attn_body(lam_ref, q_ref, k_ref, v_ref, g_ref, o_ref, *, tq, out_scale):
    qi = pl.program_id(1)
    lam = lam_ref[0]
    hw = DA_V_DIM
    rowi = lax.broadcasted_iota(jnp.int32, (2 * tq, tq), 0) % tq
    coli = lax.broadcasted_iota(jnp.int32, (2 * tq, tq), 1)
    lane = lax.broadcasted_iota(jnp.int32, (tq, hw), 1)
    for h in range(DA_HEADS):
        cs = slice(h * hw, (h + 1) * hw)
        qh = q_ref[0, :, cs]
        zero = jnp.zeros_like(qh)
        q2 = jnp.concatenate([jnp.where(lane < DA_HEAD_DIM, qh, zero),
                              jnp.where(lane >= DA_HEAD_DIM, qh, zero)], axis=0)

        def body(jb, carry, cs=cs, q2=q2):
            m, l, acc = carry
            r0 = pl.multiple_of(jb * tq, tq)
            s = _dot_nt(q2, k_ref[0, pl.ds(r0, tq), cs])
            s = jnp.where(coli + jb * tq <= rowi + qi * tq, s, MASK_VALUE)
            m_new = jnp.maximum(m, jnp.max(s, axis=1, keepdims=True))
            alpha = jnp.exp(m - m_new)
            p = jnp.exp(s - m_new)
            l = alpha * l + jnp.sum(p, axis=1, keepdims=True)
            acc = alpha * acc + _dot(p.astype(BF16), v_ref[0, pl.ds(r0, tq), cs])
            return m_new, l, acc

        init = (jnp.full((2 * tq, 1), MASK_VALUE, F32), jnp.zeros((2 * tq, 1), F32),
                jnp.zeros((2 * tq, hw), F32))
        m, l, acc = lax.fori_loop(0, qi + 1, body, init)
        on = acc / l
        o = on[0:tq] - lam * on[tq:2 * tq]
        o_ref[0, :, cs] = _rms(o, g_ref[...]) * out_scale


def _attn(lam, q, k, v, g, tq, out_scale):
    B, T, Wd = q.shape
    return pl.pallas_call(
        functools.partial(_attn_body, tq=tq, out_scale=out_scale),
        grid=(B, T // tq),
        in_specs=[pl.BlockSpec(memory_space=pltpu.SMEM),
                  pl.BlockSpec((1, tq, Wd), lambda b, i: (b, i, 0)),
                  pl.BlockSpec((1, T, Wd), lambda b, i: (b, 0, 0)),
                  pl.BlockSpec((1, T, Wd), lambda b, i: (b, 0, 0)),
                  pl.BlockSpec((1, DA_V_DIM), lambda b, i: (0, 0))],
        out_specs=pl.BlockSpec((1, tq, Wd), lambda b, i: (b, i, 0)),
        out_shape=jax.ShapeDtypeStruct((B, T, Wd), F32),
        compiler_params=_cparams(("parallel", "arbitrary")),
        name="diff_attn",
    )(lam, q, k, v, g)


def _dec_body(pt_ref, lam_ref, q_ref, kn_ref, vn_ref, g_ref, *rest, pp, t_new, out_scale):
    k_refs = rest[0:pp]
    v_refs = rest[pp:2 * pp]
    o_ref = rest[2 * pp]
    m_scr, l_scr, acc_scr = rest[2 * pp + 1:]
    j = pl.program_id(1)
    nrow = 2 * DA_HEADS * t_new

    @pl.when(j == 0)
    def _():
        m_scr[...] = jnp.full(m_scr.shape, MASK_VALUE, F32)
        l_scr[...] = jnp.zeros(l_scr.shape, F32)
        acc_scr[...] = jnp.zeros(acc_scr.shape, F32)

    q = q_ref[0]

    def update(s, vals):
        m = m_scr[...]
        m_new = jnp.maximum(m, jnp.max(s, axis=1, keepdims=True))
        alpha = jnp.exp(m - m_new)
        p = jnp.exp(s - m_new)
        l_scr[...] = alpha * l_scr[...] + jnp.sum(p, axis=1, keepdims=True)
        acc_scr[...] = alpha * acc_scr[...] + _dot(p.astype(BF16), vals)
        m_scr[...] = m_new

    ncol = PAGE_SIZE * DA_HEADS
    rh = (lax.broadcasted_iota(jnp.int32, (nrow, ncol), 0) // t_new) % DA_HEADS
    chd = lax.broadcasted_iota(jnp.int32, (nrow, ncol), 1) % DA_HEADS
    same_head = rh == chd
    for i in range(pp):
        s = _dot_nt(q, k_refs[i][...].astype(BF16))
        update(jnp.where(same_head, s, MASK_VALUE), v_refs[i][...].astype(BF16))

    @pl.when(j == pl.num_programs(1) - 1)
    def _():
        nn = kn_ref.shape[1]
        r = lax.broadcasted_iota(jnp.int32, (nrow, nn), 0)
        cidx = lax.broadcasted_iota(jnp.int32, (nrow, nn), 1)
        ok = ((r // t_new) % DA_HEADS == cidx % DA_HEADS) & (cidx // DA_HEADS <= r % t_new)
        s = _dot_nt(q, kn_ref[0].astype(BF16))
        update(jnp.where(ok, s, MASK_VALUE), vn_ref[0].astype(BF16))
        on = acc_scr[...] / l_scr[...]
        half = nrow // 2
        o = on[0:half] - lam_ref[0] * on[half:nrow]
        o_ref[0] = _rms(o, g_ref[...]) * out_scale


def _dec_attn(pt, lam, q2, kn, vn, g, ck, cv, layer, pp, t_new, out_scale):
    B, nrow, hw = q2.shape
    n_pages = pt.shape[0] // B
    nn = kn.shape[1]
    rows = PAGE_SIZE * DA_HEADS

    def page_spec(i):
        return pl.BlockSpec((None, None, rows, hw),
                            lambda b, j, pt_ref: (layer, pt_ref[b * n_pages + j * pp + i], 0, 0))

    grid_spec = pltpu.PrefetchScalarGridSpec(
        num_scalar_prefetch=1,
        grid=(B, n_pages // pp),
        in_specs=[pl.BlockSpec(memory_space=pltpu.SMEM),
                  pl.BlockSpec((1, nrow, hw), lambda b, j, pt_ref: (b, 0, 0)),
                  pl.BlockSpec((1, nn, hw), lambda b, j, pt_ref: (b, 0, 0)),
                  pl.BlockSpec((1, nn, hw), lambda b, j, pt_ref: (b, 0, 0)),
                  pl.BlockSpec((1, hw), lambda b, j, pt_ref: (0, 0))]
                 + [page_spec(i) for i in range(pp)] + [page_spec(i) for i in range(pp)],
        out_specs=pl.BlockSpec((1, nrow // 2, hw), lambda b, j, pt_ref: (b, 0, 0)),
        scratch_shapes=[pltpu.VMEM((nrow, 1), F32), pltpu.VMEM((nrow, 1), F32),
                        pltpu.VMEM((nrow, hw), F32)],
    )
    return pl.pallas_call(
        functools.partial(_dec_body, pp=pp, t_new=t_new, out_scale=out_scale),
        grid_spec=grid_spec,
        out_shape=jax.ShapeDtypeStruct((B, nrow // 2, hw), F32),
        compiler_params=_cparams(("parallel", "arbitrary")),
        name="paged_diff_attn",
    )(pt, lam, q2, kn, vn, g, *([ck] * pp), *([cv] * pp))


def _merge_body(x_ref, g_ref, ya_ref, ua_ref, yb_ref, yc_ref, yd_ref, d_ref, wglu_ref,
                wgt_ref, wa_ref, wb_ref, wc_ref, wd_ref, wo_ref, o_ref):
    x = x_ref[...]
    h = _rms(x, g_ref[...]).astype(BF16)
    z = jax.nn.gelu(ya_ref[...] + d_ref[...] * ua_ref[...])
    ya = z * jax.nn.sigmoid(_dot(z.astype(BF16), wglu_ref[...]))
    merged = None
    branches = ((ya, wa_ref), (yb_ref[...], wb_ref), (yc_ref[...], wc_ref), (yd_ref[...], wd_ref))
    for i, (yv, w_ref) in enumerate(branches):
        gate = jax.nn.sigmoid(_dot(h, wgt_ref[:, i * D_MODEL:(i + 1) * D_MODEL]))
        term = gate * _dot(yv.astype(BF16), w_ref[...])
        merged = term if merged is None else merged + term
    o_ref[...] = x + _dot(merged.astype(BF16), wo_ref[...])


def _merge(x, g, ya, ua, yb, yc, yd, d, wglu, wgt, wa, wb, wc, wd, wo, tm):
    n = x.shape[0]
    row = lambda i: (i, 0)
    fixed = lambda i: (0, 0)
    full = lambda a: pl.BlockSpec(a.shape, fixed)
    return pl.pallas_call(
        _merge_body,
        grid=(n // tm,),
        in_specs=[pl.BlockSpec((tm, D_MODEL), row), full(g),
                  pl.BlockSpec((tm, 256), row), pl.BlockSpec((tm, 256), row),
                  pl.BlockSpec((tm, 256), row), pl.BlockSpec((tm, 256), row),
                  pl.BlockSpec((tm, 512), row),
                  full(d), full(wglu), full(wgt), full(wa), full(wb), full(wc), full(wd), full(wo)],
        out_specs=pl.BlockSpec((tm, D_MODEL), row),
        out_shape=jax.ShapeDtypeStruct((n, D_MODEL), F32),
        compiler_params=_cparams(("parallel",)),
        name="merge",
    )(x, g, ya, ua, yb, yc, yd, d, wglu, wgt, wa, wb, wc, wd, wo)


def _moe_body(x_ref, g_ref, wr_ref, br_ref, wg_ref, wu_ref, wd_ref, gf_ref, o_ref,
              h_scr, gate_scr, acc_scr, *, final_norm):
    gi = pl.program_id(1)
    tm = x_ref.shape[0]
    R = ROUTER_LANES

    @pl.when(gi == 0)
    def _():
        h = _rms(x_ref[...], g_ref[...]).astype(BF16)
        h_scr[...] = h
        logits = _dot(h, wr_ref[...]) + br_ref[...]
        lane = lax.broadcasted_iota(jnp.int32, (tm, R), 1)
        lanef = lane.astype(F32)
        neg = -jnp.inf
        is_g = lane < MOE_GROUPS
        glm = jnp.where(is_g, logits, neg)
        gmax = jnp.max(glm, axis=1, keepdims=True)
        gsum = jnp.sum(jnp.where(is_g, jnp.exp(glm - gmax), 0.0), axis=1, keepdims=True)
        g_w = 1.0 / gsum
        g_i = jnp.min(jnp.where(glm == gmax, lanef, float(R)), axis=1, keepdims=True)
        e_grp = ((lane - MOE_GROUPS) // MOE_PER_GROUP).astype(F32)
        sel = (lane >= MOE_GROUPS) & (lane < MOE_GROUPS + MOE_EXPERTS) & (e_grp == g_i)
        elm = jnp.where(sel, logits, neg)
        e1 = jnp.max(elm, axis=1, keepdims=True)
        i1 = jnp.min(jnp.where(elm == e1, lanef, float(R)), axis=1, keepdims=True)
        elm2 = jnp.where(lanef == i1, neg, elm)
        e2 = jnp.max(elm2, axis=1, keepdims=True)
        i2 = jnp.min(jnp.where(elm2 == e2, lanef, float(R)), axis=1, keepdims=True)
        t = jnp.exp(e2 - e1)
        w1 = g_w / (1.0 + t)
        w2 = g_w * t / (1.0 + t)
        gate_scr[...] = jnp.where(lanef == i1, w1, 0.0) + jnp.where(lanef == i2, w2, 0.0)
        acc_scr[...] = jnp.zeros(acc_scr.shape, F32)

    h = h_scr[...]
    hid = jax.nn.silu(_dot(h, wg_ref[...])) * _dot(h, wu_ref[...])
    gw = MOE_PER_GROUP * MOE_HIDDEN
    er = lax.broadcasted_iota(jnp.int32, (R, gw), 0)
    ec = lax.broadcasted_iota(jnp.int32, (R, gw), 1)
    expand = (er == MOE_GROUPS + MOE_PER_GROUP * gi + ec // MOE_HIDDEN).astype(BF16)
    gates = _split_dot(gate_scr[...], expand, 2)
    acc_scr[...] += _dot((hid * gates).astype(BF16), wd_ref[...])

    @pl.when(gi == MOE_GROUPS - 1)
    def _():
        o = x_ref[...] + acc_scr[...]
        if final_norm:
            o = _rms(o, gf_ref[...])
        o_ref[...] = o


def _moe(x, g, wr, br, wg, wu, wd, gf, tm, final_norm):
    n = x.shape[0]
    gw = MOE_PER_GROUP * MOE_HIDDEN
    row = lambda i, e: (i, 0)
    fixed = lambda i, e: (0, 0)
    return pl.pallas_call(
        functools.partial(_moe_body, final_norm=final_norm),
        grid=(n // tm, MOE_GROUPS),
        in_specs=[pl.BlockSpec((tm, D_MODEL), row), pl.BlockSpec((1, D_MODEL), fixed),
                  pl.BlockSpec((D_MODEL, ROUTER_LANES), fixed), pl.BlockSpec((1, ROUTER_LANES), fixed),
                  pl.BlockSpec((D_MODEL, gw), lambda i, e: (0, e)),
                  pl.BlockSpec((D_MODEL, gw), lambda i, e: (0, e)),
                  pl.BlockSpec((gw, D_MODEL), lambda i, e: (e, 0)),
                  pl.BlockSpec((1, D_MODEL), fixed)],
        out_specs=pl.BlockSpec((tm, D_MODEL), row),
        out_shape=jax.ShapeDtypeStruct((n, D_MODEL), F32),
        scratch_shapes=[pltpu.VMEM((tm, D_MODEL), BF16), pltpu.VMEM((tm, ROUTER_LANES), F32),
                        pltpu.VMEM((tm, D_MODEL), F32)],
        compiler_params=_cparams(("parallel", "arbitrary")),
        name="moe",
    )(x, g, wr, br, wg, wu, wd, gf)


def _rope_tables(pos):
    half = DA_HEAD_DIM // 2
    inv = 1.0 / (ROPE_THETA ** (jnp.arange(half, dtype=F32) * 2.0 / DA_HEAD_DIM))
    ang = pos.astype(F32)[:, None] * inv[None, :]
    reps = DA_QK_WIDTH // half
    return jnp.tile(jnp.cos(ang), (1, reps)), jnp.tile(jnp.sin(ang), (1, reps))


def _block_diag(w):
    n, a, b = w.shape
    eye = jnp.eye(n, dtype=w.dtype)
    return jnp.einsum('nij,nm->nimj', w, eye).reshape(n * a, n * b)


def _layer_params(l, p):
    row = lambda a: a.astype(F32).reshape(1, -1)
    w_in = p['w_in'][l]
    p_lb = jax.nn.softmax(p['hg_lb_logits'].astype(F32), axis=0)
    lb = jnp.cumsum(p_lb, axis=0)[l] - p_lb[0]
    lam_init = 0.8 - 0.6 * math.exp(-0.3 * l)
    lam = (jnp.exp(jnp.sum(p['diff_lq1'][l].astype(F32) * p['diff_lk1'][l].astype(F32)))
           - jnp.exp(jnp.sum(p['diff_lq2'][l].astype(F32) * p['diff_lk2'][l].astype(F32))) + lam_init)
    w_router = jnp.concatenate(
        [p['moe_w_grp'][l], p['moe_w_exp'][l],
         jnp.zeros((D_MODEL, ROUTER_LANES - MOE_GROUPS - MOE_EXPERTS), F32)], axis=1)
    b_router = jnp.concatenate(
        [p['moe_b_grp'][l].astype(F32), p['moe_b_exp'][l].astype(F32),
         jnp.zeros((ROUTER_LANES - MOE_GROUPS - MOE_EXPERTS,), F32)]).reshape(1, -1)
    eh = MOE_EXPERTS * MOE_HIDDEN
    return dict(
        norm_mix=row(p['norm_mix'][l]),
        w_mix=w_in[:, :MIX_COLS].astype(BF16),
        w_gates=w_in[:, MIX_COLS:].astype(BF16),
        s5=(p['ssm_lambda_re'][l], p['ssm_lambda_im'][l], p['ssm_log_dt'][l], p['ssm_b_re'][l],
            p['ssm_b_im'][l], p['ssm_c_re'][l], p['ssm_c_im'][l]),
        ssm_d=row(p['ssm_d'][l]),
        w_glu=p['ssm_w_glu'][l].astype(BF16),
        hg_lb=lb.reshape(1, -1),
        hg_norm=jnp.tile(p['hg_norm'][l].astype(F32), HG_HEADS).reshape(1, -1),
        conv_w=p['lru_conv_w'][l].astype(F32),
        conv_b=row(p['lru_conv_b'][l]),
        w_ax=jnp.concatenate([_block_diag(p['lru_wa'][l]), _block_diag(p['lru_wx'][l])], axis=1).astype(BF16),
        b_ax=jnp.concatenate([p['lru_ba'][l], p['lru_bx'][l]]).astype(F32).reshape(1, -1),
        neg_c_softplus=(-LRU_C * jax.nn.softplus(-p['lru_lambda'][l].astype(F32))).reshape(1, -1),
        lam=lam.reshape(1).astype(F32),
        out_scale=1.0 - lam_init,
        diff_norm=row(p['diff_norm'][l]),
        w_br_a=p['w_br_a'][l].astype(BF16), w_br_b=p['w_br_b'][l].astype(BF16),
        w_br_c=p['w_br_c'][l].astype(BF16), w_br_d=p['w_br_d'][l].astype(BF16),
        w_out=p['w_out'][l].astype(BF16),
        norm_ffn=row(p['norm_ffn'][l]),
        w_router=w_router.astype(BF16), b_router=b_router,
        moe_gate=p['moe_w_gate'][l].astype(BF16).transpose(1, 0, 2).reshape(D_MODEL, eh),
        moe_up=p['moe_w_up'][l].astype(BF16).transpose(1, 0, 2).reshape(D_MODEL, eh),
        moe_down=p['moe_w_down'][l].astype(BF16).reshape(eh, D_MODEL),
    )


def _pad_rows(a, rows):
    return jnp.pad(a, ((0, 0), (0, rows - a.shape[1])) + ((0, 0),) * (a.ndim - 2))


def _trunk(x, pos0, states, cache, page_table, layers, norm_final, cfg):
    B, T, _ = x.shape
    n = B * T
    tm, s5_chunk, tpad, hg_tb, hg_c, lru_tb, tq = (cfg[k] for k in
                                                   ('tm', 's5_chunk', 'tpad', 'hg_tb', 'hg_c', 'lru_tb', 'tq'))
    cos, sin = _rope_tables(pos0 + jnp.arange(T, dtype=jnp.int32))
    if T % tm:
        cos, sin = jnp.tile(cos, (tm // T, 1)), jnp.tile(sin, (tm // T, 1))
    xf = x.reshape(n, D_MODEL)
    ks, vs, sts = [], [], []
    for l, lp in enumerate(layers):
        st = states[l]
        ua, hg, xg, qb, k, kb, v, vb = _inproj(xf, lp['norm_mix'], lp['w_mix'], cos, sin, tm)
        ya, ssm_re, ssm_im = _s5_mixer(ua.reshape(B, T, SSM_WIDTH), st[0], st[1],
                                       _s5_weights(*lp['s5'], s5_chunk), s5_chunk)
        hg3 = _pad_rows(hg.reshape(B, T, 4 * HG_WIDTH), tpad)
        yb, s_t = _hgrn(hg3, _hgrn_state_to_t(st[2].astype(F32)), lp['hg_lb'], lp['hg_norm'],
                        hg_tb, hg_c, T)
        hg_state = _hgrn_state_from_t(s_t)
        xg3 = xg.reshape(B, T, 2 * LRU_WIDTH)
        conv0 = jnp.pad(st[4].astype(F32), ((0, 0), (8 - (CONV_WIDTH - 1), 0), (0, 0)))
        yc, lru_h = _lru(_pad_rows(xg3, tpad), conv0, st[3].astype(F32).reshape(B, 1, LRU_WIDTH),
                         lp['conv_w'], lp['conv_b'], lp['w_ax'], lp['b_ax'], lp['neg_c_softplus'],
                         lru_tb, T)
        xp = jnp.concatenate([st[4].astype(F32), xg3[:, :, :LRU_WIDTH]], axis=1)
        conv_buf = xp[:, T:]
        if cache is None:
            yd = _attn(lp['lam'], qb.reshape(B, T, -1), kb.reshape(B, T, -1), vb.reshape(B, T, -1),
                       lp['diff_norm'], tq, lp['out_scale'])
        else:
            hw = DA_V_DIM
            q4 = qb.reshape(B, T, DA_HEADS, 2, DA_HEAD_DIM)
            zero = jnp.zeros_like(q4[:, :, :, 0])
            q2 = jnp.stack([jnp.concatenate([q4[:, :, :, 0], zero], -1),
                            jnp.concatenate([zero, q4[:, :, :, 1]], -1)], axis=1)
            q2 = q2.transpose(0, 1, 3, 2, 4).reshape(B, 2 * DA_HEADS * T, hw)
            nn = 128
            kn = _pad_rows(k.reshape(B, T * DA_HEADS, hw), nn)
            vn = _pad_rows(v.reshape(B, T * DA_HEADS, hw), nn)
            o = _dec_attn(page_table.reshape(-1), lp['lam'], q2, kn, vn, lp['diff_norm'],
                          cache[0], cache[1], l, cfg['pp'], T, lp['out_scale'])
            yd = o.reshape(B, DA_HEADS, T, hw).transpose(0, 2, 1, 3).reshape(B, T, DA_WIDTH)
        x1 = _merge(xf, lp['norm_mix'], ya.reshape(n, -1), ua, yb[:, :T].reshape(n, -1),
                    yc[:, :T].reshape(n, -1), yd.reshape(n, -1), lp['ssm_d'], lp['w_glu'], lp['w_gates'],
                    lp['w_br_a'], lp['w_br_b'], lp['w_br_c'], lp['w_br_d'], lp['w_out'], tm)
        xf = _moe(x1, lp['norm_ffn'], lp['w_router'], lp['b_router'], lp['moe_gate'], lp['moe_up'],
                  lp['moe_down'], norm_final, tm, l == len(layers) - 1)
        ks.append(k.reshape(B, T, DA_HEADS, 2 * DA_HEAD_DIM))
        vs.append(v.reshape(B, T, DA_HEADS, DA_V_DIM))
        sts.append((ssm_re, ssm_im, hg_state, lru_h.reshape(B, LRU_WIDTH), conv_buf))
    stacked = [jnp.stack([s[j] for s in sts]) for j in range(5)]
    return xf.reshape(B, T, D_MODEL), jnp.stack(ks), jnp.stack(vs), stacked


PROMPT_CFG = dict(tm=512, s5_chunk=16, tpad=2048, hg_tb=128, hg_c=16, lru_tb=256, tq=128)
SAMPLE_CFG = dict(tm=128, s5_chunk=4, tpad=16, hg_tb=16, hg_c=16, lru_tb=16, tq=0, pp=4)


def kernel(x_prompt, x_sample, cache_k, cache_v, page_table, state_ssm_re, state_ssm_im, state_hgrn,
           state_lru, state_conv, norm_mix, w_in, ssm_lambda_re, ssm_lambda_im, ssm_log_dt, ssm_b_re,
           ssm_b_im, ssm_c_re, ssm_c_im, ssm_d, ssm_w_glu, hg_lb_logits, hg_norm, lru_conv_w, lru_conv_b,
           lru_wa, lru_ba, lru_wx, lru_bx, lru_lambda, diff_lq1, diff_lk1, diff_lq2, diff_lk2, diff_norm,
           w_br_a, w_br_b, w_br_c, w_br_d, w_out, norm_ffn, moe_w_grp, moe_b_grp, moe_w_exp, moe_b_exp,
           moe_w_gate, moe_w_up, moe_w_down, norm_final):
    p = dict(norm_mix=norm_mix, w_in=w_in, ssm_lambda_re=ssm_lambda_re, ssm_lambda_im=ssm_lambda_im,
             ssm_log_dt=ssm_log_dt, ssm_b_re=ssm_b_re, ssm_b_im=ssm_b_im, ssm_c_re=ssm_c_re,
             ssm_c_im=ssm_c_im, ssm_d=ssm_d, ssm_w_glu=ssm_w_glu, hg_lb_logits=hg_lb_logits,
             hg_norm=hg_norm, lru_conv_w=lru_conv_w, lru_conv_b=lru_conv_b, lru_wa=lru_wa, lru_ba=lru_ba,
             lru_wx=lru_wx, lru_bx=lru_bx, lru_lambda=lru_lambda, diff_lq1=diff_lq1, diff_lk1=diff_lk1,
             diff_lq2=diff_lq2, diff_lk2=diff_lk2, diff_norm=diff_norm, w_br_a=w_br_a, w_br_b=w_br_b,
             w_br_c=w_br_c, w_br_d=w_br_d, w_out=w_out, norm_ffn=norm_ffn, moe_w_grp=moe_w_grp,
             moe_b_grp=moe_b_grp, moe_w_exp=moe_w_exp, moe_b_exp=moe_b_exp, moe_w_gate=moe_w_gate,
             moe_w_up=moe_w_up, moe_w_down=moe_w_down)
    layers = [_layer_params(l, p) for l in range(DEPTH)]
    gf = norm_final.astype(F32).reshape(1, -1)
    Bp = x_prompt.shape[0]
    Bs = x_sample.shape[0]
    zero_states = [(jnp.zeros((Bp, SSM_GROUPS, SSM_STATE), F32), jnp.zeros((Bp, SSM_GROUPS, SSM_STATE), F32),
                    jnp.zeros((Bp, HG_HEADS, HG_HEAD_DIM, HG_HEAD_DIM), F32), jnp.zeros((Bp, LRU_WIDTH), F32),
                    jnp.zeros((Bp, CONV_WIDTH - 1, LRU_WIDTH), F32)) for _ in range(DEPTH)]
    y_p, k_p, v_p, st_p = _trunk(x_prompt, 0, zero_states, None, None, layers, gf, PROMPT_CFG)
    past_len = page_table.shape[1] * PAGE_SIZE
    sample_states = [(state_ssm_re[l], state_ssm_im[l], state_hgrn[l], state_lru[l], state_conv[l])
                     for l in range(DEPTH)]
    n_pool = cache_k.shape[1]
    rows = PAGE_SIZE * DA_HEADS
    cache = (cache_k.reshape(DEPTH, n_pool, rows, 2 * DA_HEAD_DIM), cache_v.reshape(DEPTH, n_pool, rows, DA_V_DIM))
    y_s, k_s, v_s, st_s = _trunk(x_sample, past_len, sample_states, cache, page_table, layers, gf, SAMPLE_CFG)
    return (y_p, y_s, k_p, v_p, k_s, v_s,
            st_p[0], st_p[1], st_s[0], st_s[1], st_p[2], st_s[2], st_p[3], st_s[3], st_p[4], st_s[4])
```

```python
import functools
import math

import jax
import jax.numpy as jnp
from jax import lax
from jax.experimental import pallas as pl
from jax.experimental.pallas import tpu as pltpu

F32 = jnp.float32
BF16 = jnp.bfloat16

D_MODEL = 1024
DEPTH = 2
PAGE_SIZE = 128
SSM_WIDTH = 256
SSM_GROUP = 16
SSM_GROUPS = 16
SSM_STATE = 64
HG_WIDTH = 256
HG_HEAD_DIM = 64
HG_HEADS = 4
LRU_WIDTH = 256
LRU_BLOCKS = 4
LRU_BLOCK = 64
CONV_WIDTH = 4
LRU_C = 8.0
DA_HEADS = 4
DA_HEAD_DIM = 64
DA_V_DIM = 128
DA_QK_WIDTH = 512
DA_WIDTH = 512
ROPE_THETA = 10000.0
MASK_VALUE = -1e30
N_BRANCH = 4
MOE_GROUPS = 4
MOE_PER_GROUP = 8
MOE_EXPERTS = 32
MOE_HIDDEN = 128
NORM_EPS = 1e-6
MIX_COLS = 3328
ROUTER_LANES = 128
VMEM_LIMIT = 56 * 1024 * 1024
HI = lax.Precision.HIGHEST
Q_SCALE = DA_HEAD_DIM ** -0.5 * math.log2(math.e)
KV_UNIT = 256


def _cparams(sem):
    return pltpu.CompilerParams(dimension_semantics=sem, vmem_limit_bytes=VMEM_LIMIT)


def _rms(x, g):
    return x * lax.rsqrt(jnp.mean(x * x, axis=-1, keepdims=True) + NORM_EPS) * g


def _dot(a, b):
    return jnp.dot(a, b, preferred_element_type=F32)


def _mm(a, w, dims=(((1,), (0,)), ((), ()))):
    if w.dtype == BF16:
        return lax.dot_general(a.astype(BF16), w, dims, preferred_element_type=F32)
    return lax.dot_general(a.astype(F32), w, dims, preferred_element_type=F32, precision=HI)


NT_DIMS = (((1,), (1,)), ((), ()))
TN_DIMS = (((0,), (0,)), ((), ()))


def _dot_nt(a, b):
    return lax.dot_general(a, b, (((1,), (1,)), ((), ())), preferred_element_type=F32)


def _dot_tn(a, b):
    return lax.dot_general(a, b, (((0,), (0,)), ((), ())), preferred_element_type=F32)


def _split_dot(a, b_bf16, terms):
    out = None
    rem = a
    for _ in range(terms):
        piece = rem.astype(BF16)
        part = _dot(piece, b_bf16)
        out = part if out is None else out + part
        rem = rem - piece.astype(F32)
    return out


def _head_ones(width, head):
    r = lax.broadcasted_iota(jnp.int32, (width, width), 0) // head
    c = lax.broadcasted_iota(jnp.int32, (width, width), 1) // head
    return r == c


def _inproj_body(x_ref, g_ref, w_ref, cos_ref, sin_ref,
                 ua_ref, hg_ref, lru_ref, q_ref, k_ref, kb_ref, v_ref, vt_ref):
    h = _rms(x_ref[...], g_ref[...]).astype(w_ref.dtype)

    def mm(a, b):
        return _mm(h, w_ref[:, a:b])

    ua_ref[...] = mm(0, 256)
    hg_ref[...] = mm(256, 1280)
    lru_ref[...] = mm(1280, 1792)
    cos = cos_ref[...]
    sin = sin_ref[...]
    lane = lax.broadcasted_iota(jnp.int32, cos.shape, 1)
    first = (lane % DA_HEAD_DIM) < (DA_HEAD_DIM // 2)

    def rope(z):
        rot = jnp.where(first, -pltpu.roll(z, DA_QK_WIDTH - DA_HEAD_DIM // 2, 1),
                        pltpu.roll(z, DA_HEAD_DIM // 2, 1))
        return z * cos + rot * sin

    q = rope(mm(1792, 2304))
    q_ref[...] = (q * Q_SCALE).astype(q_ref.dtype)
    k = rope(mm(2304, 2816))
    kb_ref[...] = k.astype(BF16)
    v = mm(2816, 3328)
    tm = k.shape[0]
    unit = vt_ref.shape[2]
    for u in range(tm // unit):
        vt_ref[u] = v[u * unit:(u + 1) * unit, :].T.astype(BF16)
    for hd in range(DA_HEADS):
        cs = slice(hd * DA_V_DIM, (hd + 1) * DA_V_DIM)
        k_ref[pl.ds(hd, tm, stride=DA_HEADS), :] = k[:, cs]
        v_ref[pl.ds(hd, tm, stride=DA_HEADS), :] = v[:, cs]


def _inproj(x, g, w, cos, sin, tm):
    n = x.shape[0]
    ntab = cos.shape[0] // tm
    row = lambda i: (i, 0)
    fixed = lambda i: (0, 0)
    tab = lambda i: (i % ntab, 0)
    outs = ((1, 256, F32), (1, 1024, F32), (1, 512, F32), (1, 512, w.dtype),
            (DA_HEADS, DA_V_DIM, F32), (1, 512, BF16), (DA_HEADS, DA_V_DIM, F32))
    unit = min(tm, KV_UNIT)
    return pl.pallas_call(
        _inproj_body,
        grid=(n // tm,),
        in_specs=[pl.BlockSpec((tm, D_MODEL), row), pl.BlockSpec((1, D_MODEL), fixed),
                  pl.BlockSpec((D_MODEL, MIX_COLS), fixed),
                  pl.BlockSpec((tm, 512), tab), pl.BlockSpec((tm, 512), tab)],
        out_specs=[pl.BlockSpec((tm * r, wd), row) for r, wd, _ in outs]
                  + [pl.BlockSpec((tm // unit, DA_WIDTH, unit), lambda i: (i, 0, 0))],
        out_shape=[jax.ShapeDtypeStruct((n * r, wd), dt) for r, wd, dt in outs]
                  + [jax.ShapeDtypeStruct((n // unit, DA_WIDTH, unit), BF16)],
        compiler_params=_cparams(("parallel",)),
        name="inproj",
    )(x, g, w, cos, sin)


def _s5_body(u_ref, m_ref, p_ref, q_ref, a_ref, h0_ref, y_ref, hf_ref, pu_scr, hs_scr, *, nb, nc):
    u = u_ref[0].astype(m_ref.dtype)
    pu_scr[...] = _mm(u, p_ref[0])
    ar2 = a_ref[0, 0:1, :]
    ai2 = a_ref[0, 1:2, :]

    def step(c, h):
        r0 = pl.multiple_of(c * nb, nb)
        hs_scr[pl.ds(r0, nb), :] = h
        return ar2 * h + ai2 * pltpu.roll(h, SSM_STATE, 1) + pu_scr[pl.ds(r0, nb), :]

    hf_ref[0] = lax.fori_loop(0, nc, step, h0_ref[0])
    y_ref[0] = _mm(u, m_ref[0]) + _mm(hs_scr[...], q_ref[0])


def _s5(u_g, m, p, q, a, h0, nb, nc):
    g, rows, lw = u_g.shape
    blk = lambda i: (i, 0, 0)
    return pl.pallas_call(
        functools.partial(_s5_body, nb=nb, nc=nc),
        grid=(g,),
        in_specs=[pl.BlockSpec((1, rows, lw), blk), pl.BlockSpec((1, lw, lw), blk),
                  pl.BlockSpec((1, lw, 128), blk), pl.BlockSpec((1, 128, lw), blk),
                  pl.BlockSpec((1, 2, 128), blk), pl.BlockSpec((1, nb, 128), blk)],
        out_specs=[pl.BlockSpec((1, rows, lw), blk), pl.BlockSpec((1, nb, 128), blk)],
        out_shape=[jax.ShapeDtypeStruct((g, rows, lw), F32), jax.ShapeDtypeStruct((g, nb, 128), F32)],
        scratch_shapes=[pltpu.VMEM((rows, 128), F32), pltpu.VMEM((rows, 128), F32)],
        compiler_params=_cparams(("parallel",)),
        name="s5",
    )(u_g, m, p, q, a, h0)


def _s5_weights(lam_re, lam_im, log_dt, b_re, b_im, c_re, c_im, L, wdtype):
    G, P, J = SSM_GROUPS, SSM_STATE, SSM_GROUP
    lr, li = lam_re.astype(F32), lam_im.astype(F32)
    dt = jnp.exp(log_dt.astype(F32))[:, None]
    mag = jnp.exp(lr * dt)
    ar = mag * jnp.cos(li * dt)
    ai = mag * jnp.sin(li * dt)
    den = lr * lr + li * li
    fr = ((ar - 1.0) * lr + ai * li) / den
    fi = (ai * lr - (ar - 1.0) * li) / den
    br, bi = b_re.astype(F32), b_im.astype(F32)
    bbr = fr[..., None] * br - fi[..., None] * bi
    bbi = fr[..., None] * bi + fi[..., None] * br
    tau = jnp.arange(L + 1, dtype=F32)[:, None, None]
    pmag = jnp.exp(lr * dt * tau)
    pr = pmag * jnp.cos(li * dt * tau)
    pi = pmag * jnp.sin(li * dt * tau)
    t1r = pr[..., None] * bbr - pi[..., None] * bbi
    t1i = pr[..., None] * bbi + pi[..., None] * bbr
    cr, ci = c_re.astype(F32), c_im.astype(F32)
    kt = (jnp.einsum('gip,tgpj->tgij', cr, t1r, precision=HI)
          - jnp.einsum('gip,tgpj->tgij', ci, t1i, precision=HI))
    s = jnp.arange(L)[:, None]
    t = jnp.arange(L)[None, :]
    kst = kt[jnp.clip(t - s, 0, L)]
    kst = jnp.where((t >= s)[:, :, None, None, None], kst, 0.0)
    m = kst.transpose(2, 0, 4, 1, 3).reshape(G, L * J, L * J)
    rev = L - 1 - jnp.arange(L)
    pmat = jnp.concatenate([t1r[rev].transpose(1, 0, 3, 2), t1i[rev].transpose(1, 0, 3, 2)],
                           axis=-1).reshape(G, L * J, 2 * P)
    car = cr[None] * pr[1:, :, None, :] - ci[None] * pi[1:, :, None, :]
    cai = cr[None] * pi[1:, :, None, :] + ci[None] * pr[1:, :, None, :]
    qmat = jnp.concatenate([car.transpose(1, 3, 0, 2), -cai.transpose(1, 3, 0, 2)],
                           axis=1).reshape(G, 2 * P, L * J)
    a2 = jnp.stack([jnp.concatenate([pr[L], pr[L]], -1), jnp.concatenate([-pi[L], pi[L]], -1)], axis=1)
    return m.astype(wdtype), pmat.astype(wdtype), qmat.astype(wdtype), a2


def _s5_mixer(u, h0_re, h0_im, wts, L):
    B, T, _ = u.shape
    nc = T // L
    G, J = SSM_GROUPS, SSM_GROUP
    u_g = u.reshape(B, nc, L, G, J).transpose(3, 1, 0, 2, 4).reshape(G, nc * B, L * J)
    h0 = jnp.concatenate([h0_re, h0_im], axis=-1).transpose(1, 0, 2)
    y_g, hf = _s5(u_g, *wts, h0, B, nc)
    y = y_g.reshape(G, nc, B, L, J).transpose(2, 1, 3, 0, 4).reshape(B, T, SSM_WIDTH)
    hf = hf.transpose(1, 0, 2)
    return y, hf[..., :SSM_STATE], hf[..., SSM_STATE:]


def _hgrn_body(hg_ref, s0_ref, lb_ref, ng_ref, y_ref, sf_ref,
               st_scr, k_scr, b_scr, v_scr, w_scr, *, tb, c, t_valid, precise):
    j = pl.program_id(1)
    W = HG_WIDTH
    mdt = F32 if precise else BF16

    @pl.when(j == 0)
    def _():
        st_scr[...] = s0_ref[0]
        k_scr[0:c, :] = jnp.zeros((c, W), F32)
        b_scr[0:c, :] = jnp.zeros((c, W), F32)
        v_scr[0:c, :] = jnp.zeros((c, W), F32)

    q = hg_ref[0, :, 0:W]
    lb = lb_ref[...]
    fv = lb + (1.0 - lb) * jax.nn.sigmoid(hg_ref[0, :, W:2 * W])
    logf = jnp.log(fv)
    kk = 1.0 - fv
    v = hg_ref[0, :, 2 * W:3 * W]
    row = lax.broadcasted_iota(jnp.int32, (tb, W), 0)
    if t_valid < tb:
        valid = row < t_valid
        logf = jnp.where(valid, logf, 0.0)
        kk = jnp.where(valid, kk, 0.0)
    ri = lax.broadcasted_iota(jnp.int32, (tb, tb), 0)
    ci = lax.broadcasted_iota(jnp.int32, (tb, tb), 1)
    tril = ((ri // c == ci // c) & (ci <= ri)).astype(BF16)
    b = _split_dot_lhs_exact(tril, logf)
    k_scr[c:c + tb, :] = kk
    b_scr[c:c + tb, :] = b
    v_scr[c:c + tb, :] = v
    rin = row % c
    for d in range(c):
        ksh = k_scr[c - d:c - d + tb, :]
        bsh = b_scr[c - d:c - d + tb, :]
        w = jnp.where(rin >= d, q * ksh * jnp.exp(b - bsh), 0.0)
        w_scr[d * tb:(d + 1) * tb, :] = w.astype(mdt)
    ones_bd = _head_ones(W, HG_HEAD_DIM).astype(BF16)
    att = _mm(w_scr[...], ones_bd.astype(mdt))
    o = att[0:tb] * v
    for d in range(1, c):
        o = o + att[d * tb:(d + 1) * tb] * v_scr[c - d:c - d + tb, :]
    bd = _head_ones(W, HG_HEAD_DIM)
    outs = []
    for ch in range(tb // c):
        sl = slice(ch * c, (ch + 1) * c)
        bc = b[sl]
        bl = bc[c - 1:c, :]
        st = st_scr[...]
        outs.append(_mm(q[sl] * jnp.exp(bc), st.astype(mdt), NT_DIMS))
        khat = (kk[sl] * jnp.exp(bl - bc)).astype(mdt)
        upd = _mm(v[sl], khat, TN_DIMS)
        st_scr[...] = st * jnp.exp(bl) + jnp.where(bd, upd, 0.0)
    o = o + jnp.concatenate(outs, axis=0) if len(outs) > 1 else o + outs[0]
    ms = _split_dot(o * o, ones_bd, 3 if precise else 2) * (1.0 / HG_HEAD_DIM)
    y = o * lax.rsqrt(ms + NORM_EPS) * ng_ref[...]
    y_ref[0] = y * jax.nn.silu(hg_ref[0, :, 3 * W:4 * W])
    sf_ref[0] = st_scr[...]


def _split_dot_lhs_exact(a_bf16, b):
    out = None
    rem = b
    for _ in range(3):
        piece = rem.astype(BF16)
        part = _dot(a_bf16, piece)
        out = part if out is None else out + part
        rem = rem - piece.astype(F32)
    return out


def _hgrn(hg, s0t, lb, ng, tb, c, t_valid, precise):
    B, T, _ = hg.shape
    W = HG_WIDTH
    return pl.pallas_call(
        functools.partial(_hgrn_body, tb=tb, c=c, t_valid=t_valid, precise=precise),
        grid=(B, T // tb),
        in_specs=[pl.BlockSpec((1, tb, 4 * W), lambda b, j: (b, j, 0)),
                  pl.BlockSpec((1, W, W), lambda b, j: (b, 0, 0)),
                  pl.BlockSpec((1, W), lambda b, j: (0, 0)),
                  pl.BlockSpec((1, W), lambda b, j: (0, 0))],
        out_specs=[pl.BlockSpec((1, tb, W), lambda b, j: (b, j, 0)),
                   pl.BlockSpec((1, W, W), lambda b, j: (b, 0, 0))],
        out_shape=[jax.ShapeDtypeStruct((B, T, W), F32), jax.ShapeDtypeStruct((B, W, W), F32)],
        scratch_shapes=[pltpu.VMEM((W, W), F32), pltpu.VMEM((c + tb, W), F32),
                        pltpu.VMEM((c + tb, W), F32), pltpu.VMEM((c + tb, W), F32),
                        pltpu.VMEM((c * tb, W), F32 if precise else BF16)],
        compiler_params=_cparams(("parallel", "arbitrary")),
        name="hgrn2",
    )(hg, s0t, lb, ng)


def _hgrn_state_to_t(s):
    B = s.shape[0]
    eye = jnp.eye(HG_HEADS, dtype=s.dtype)
    return jnp.einsum('bhde,hg->bhegd', s, eye).reshape(B, HG_WIDTH, HG_WIDTH)


def _hgrn_state_from_t(st):
    B = st.shape[0]
    s5 = st.reshape(B, HG_HEADS, HG_HEAD_DIM, HG_HEADS, HG_HEAD_DIM)
    idx = jnp.arange(HG_HEADS)
    return s5[:, idx, :, idx, :].transpose(1, 0, 3, 2)


def _lru_body(x_ref, c0_ref, h0_ref, cw_ref, cb_ref, wax_ref, bax_ref, nsp_ref, y_ref, hl_ref,
              xs_scr, hc_scr, *, tb, r_last):
    j = pl.program_id(1)
    W = LRU_WIDTH

    @pl.when(j == 0)
    def _():
        xs_scr[0:8, :] = c0_ref[0]
        hc_scr[...] = h0_ref[0]

    x = x_ref[0, :, 0:W]
    xs_scr[8:8 + tb, :] = x
    xc = cb_ref[...] + x * cw_ref[3:4, :]
    for jj in range(CONV_WIDTH - 1):
        xc = xc + xs_scr[5 + jj:5 + jj + tb, :] * cw_ref[jj:jj + 1, :]
    tail = xs_scr[tb:tb + 8, :]
    xs_scr[0:8, :] = tail
    rg = _mm(xc, wax_ref[...]) + bax_ref[...]
    r = jax.nn.sigmoid(rg[:, 0:W])
    ig = jax.nn.sigmoid(rg[:, W:2 * W])
    log_a = nsp_ref[...] * r
    a = jnp.exp(log_a)
    bt = jnp.sqrt(jnp.maximum(1.0 - a * a, 0.0)) * (ig * xc)
    row = lax.broadcasted_iota(jnp.int32, (tb, W), 0)
    k = 1
    while k < tb:
        keep = row >= k
        a_sh = jnp.where(keep, pltpu.roll(a, k, 0), 1.0)
        b_sh = jnp.where(keep, pltpu.roll(bt, k, 0), 0.0)
        bt = a * b_sh + bt
        a = a * a_sh
        k *= 2
    h = a * hc_scr[...] + bt
    y_ref[0] = jax.nn.gelu(x_ref[0, :, W:2 * W]) * h
    hc = h[r_last:r_last + 1, :]
    hc_scr[...] = hc
    hl_ref[0] = hc


def _lru(xg, c0, h0, cw, cb, wax, bax, nsp, tb, t_valid):
    B, T, _ = xg.shape
    W = LRU_WIDTH
    fixed = lambda b, j: (0, 0)
    return pl.pallas_call(
        functools.partial(_lru_body, tb=tb, r_last=(t_valid - 1) % tb),
        grid=(B, T // tb),
        in_specs=[pl.BlockSpec((1, tb, 2 * W), lambda b, j: (b, j, 0)),
                  pl.BlockSpec((1, 8, W), lambda b, j: (b, 0, 0)),
                  pl.BlockSpec((1, 1, W), lambda b, j: (b, 0, 0)),
                  pl.BlockSpec((CONV_WIDTH, W), fixed), pl.BlockSpec((1, W), fixed),
                  pl.BlockSpec((W, 2 * W), fixed), pl.BlockSpec((1, 2 * W), fixed),
                  pl.BlockSpec((1, W), fixed)],
        out_specs=[pl.BlockSpec((1, tb, W), lambda b, j: (b, j, 0)),
                   pl.BlockSpec((1, 1, W), lambda b, j: (b, 0, 0))],
        out_shape=[jax.ShapeDtypeStruct((B, T, W), F32), jax.ShapeDtypeStruct((B, 1, W), F32)],
        scratch_shapes=[pltpu.VMEM((8 + tb, W), F32), pltpu.VMEM((1, W), F32)],
        compiler_params=_cparams(("parallel", "arbitrary")),
        name="rglru",
    )(xg, c0, h0, cw, cb, wax, bax, nsp)


def _attn_body(lam_ref, q_ref, k_ref, vt_ref, g_ref, o_ref, q2_scr, m_scr, l_scr, acc_scr,
               *, tq, wide_units, out_scale):
    qi = pl.program_id(1)
    hw = DA_V_DIM
    lane = lax.broadcasted_iota(jnp.int32, (tq, hw), 1)
    for h in range(DA_HEADS):
        qh = q_ref[0, :, h * hw:(h + 1) * hw]
        zero = jnp.zeros_like(qh)
        q2_scr[h, 0:tq, :] = jnp.where(lane < DA_HEAD_DIM, qh, zero)
        q2_scr[h, tq:2 * tq, :] = jnp.where(lane >= DA_HEAD_DIM, qh, zero)
    m_scr[...] = jnp.full(m_scr.shape, MASK_VALUE, F32)
    l_scr[...] = jnp.zeros(l_scr.shape, F32)
    acc_scr[...] = jnp.zeros(acc_scr.shape, F32)

    def block(u0, nu, diagonal):
        tk = nu * KV_UNIT
        r0 = pl.multiple_of(u0 * KV_UNIT, KV_UNIT)
        for h in range(DA_HEADS):
            cs = slice(h * hw, (h + 1) * hw)
            st = _dot_nt(k_ref[0, pl.ds(r0, tk), cs], q2_scr[h])
            if diagonal:
                keyi = lax.broadcasted_iota(jnp.int32, (tk, 2 * tq), 0)
                qryi = lax.broadcasted_iota(jnp.int32, (tk, 2 * tq), 1) % tq
                st = jnp.where(keyi <= qryi, st, MASK_VALUE)
            m = m_scr[h]
            m_new = jnp.maximum(m, jnp.max(st, axis=0, keepdims=True))
            alpha = jnp.exp2(m - m_new)
            p = jnp.exp2(st - m_new)
            l_scr[h] = alpha * l_scr[h] + jnp.sum(p, axis=0, keepdims=True)
            pb = p.astype(BF16)
            pv = _dot(vt_ref[u0, cs, :], pb[0:KV_UNIT])
            for u in range(1, nu):
                pv = pv + _dot(vt_ref[u0 + u, cs, :], pb[u * KV_UNIT:(u + 1) * KV_UNIT])
            acc_scr[h] = alpha * acc_scr[h] + pv
            m_scr[h] = m_new

    nq = tq // KV_UNIT
    n_before = qi * nq
    n_wide = n_before // wide_units

    def wide_step(jb, carry):
        block(jb * wide_units, wide_units, False)
        return carry

    lax.fori_loop(0, n_wide, wide_step, 0)

    def unit_step(u, carry):
        block(u, 1, False)
        return carry

    lax.fori_loop(n_wide * wide_units, n_before, unit_step, 0)
    block(n_before, nq, True)
    lam = lam_ref[0]
    for h in range(DA_HEADS):
        on = acc_scr[h] / l_scr[h]
        o = (on[:, 0:tq] - lam * on[:, tq:2 * tq]).T
        o_ref[0, :, h * hw:(h + 1) * hw] = _rms(o, g_ref[...]) * out_scale


def _attn(lam, q, k, vt, g, tq, wide_units, out_scale):
    B, T, Wd = q.shape
    hw = DA_V_DIM
    units = T // KV_UNIT
    return pl.pallas_call(
        functools.partial(_attn_body, tq=tq, wide_units=wide_units, out_scale=out_scale),
        grid=(B, T // tq),
        in_specs=[pl.BlockSpec(memory_space=pltpu.SMEM),
                  pl.BlockSpec((1, tq, Wd), lambda b, i: (b, i, 0)),
                  pl.BlockSpec((1, T, Wd), lambda b, i: (b, 0, 0)),
                  pl.BlockSpec((units, Wd, KV_UNIT), lambda b, i: (b, 0, 0)),
                  pl.BlockSpec((1, hw), lambda b, i: (0, 0))],
        out_specs=pl.BlockSpec((1, tq, Wd), lambda b, i: (b, i, 0)),
        out_shape=jax.ShapeDtypeStruct((B, T, Wd), F32),
        scratch_shapes=[pltpu.VMEM((DA_HEADS, 2 * tq, hw), BF16), pltpu.VMEM((DA_HEADS, 1, 2 * tq), F32),
                        pltpu.VMEM((DA_HEADS, 1, 2 * tq), F32), pltpu.VMEM((DA_HEADS, hw, 2 * tq), F32)],
        compiler_params=_cparams(("parallel", "arbitrary")),
        name="diff_attn",
    )(lam, q, k, vt, g)


def _dec_body(pt_ref, lam_ref, q_ref, kn_ref, vn_ref, g_ref, *rest, pp, t_new, out_scale):
    k_refs = rest[0:pp]
    v_refs = rest[pp:2 * pp]
    o_ref = rest[2 * pp]
    m_scr, l_scr, acc_scr = rest[2 * pp + 1:]
    j = pl.program_id(1)
    nrow = 2 * DA_HEADS * t_new

    @pl.when(j == 0)
    def _():
        m_scr[...] = jnp.full(m_scr.shape, MASK_VALUE, F32)
        l_scr[...] = jnp.zeros(l_scr.shape, F32)
        acc_scr[...] = jnp.zeros(acc_scr.shape, F32)

    def hi_lo(a):
        hi = a.astype(BF16)
        return jnp.concatenate([hi, (a - hi.astype(F32)).astype(BF16)], axis=0)

    def fold(a):
        return a[0:nrow] + a[nrow:2 * nrow]

    q = hi_lo(q_ref[0])

    def scores(keys):
        return fold(_dot_nt(q, keys.astype(BF16)))

    def update(ss, vals):
        m = m_scr[...]
        smax = ss[0]
        for s in ss[1:]:
            smax = jnp.maximum(smax, s)
        m_new = jnp.maximum(m, jnp.max(smax, axis=1, keepdims=True))
        alpha = jnp.exp2(m - m_new)
        ps = [jnp.exp2(s - m_new) for s in ss]
        psum = ps[0]
        for p in ps[1:]:
            psum = psum + p
        pv = _dot(hi_lo(ps[0]), vals[0])
        for p, vv in zip(ps[1:], vals[1:]):
            pv = pv + _dot(hi_lo(p), vv)
        l_scr[...] = alpha * l_scr[...] + jnp.sum(psum, axis=1, keepdims=True)
        acc_scr[...] = alpha * acc_scr[...] + fold(pv)
        m_scr[...] = m_new

    ncol = PAGE_SIZE * DA_HEADS
    rh = (lax.broadcasted_iota(jnp.int32, (nrow, ncol), 0) // t_new) % DA_HEADS
    chd = lax.broadcasted_iota(jnp.int32, (nrow, ncol), 1) % DA_HEADS
    same_head = rh == chd
    update([jnp.where(same_head, scores(k_refs[i][...]), MASK_VALUE) for i in range(pp)],
           [v_refs[i][...].astype(BF16) for i in range(pp)])

    @pl.when(j == pl.num_programs(1) - 1)
    def _():
        nn = kn_ref.shape[1]
        r = lax.broadcasted_iota(jnp.int32, (nrow, nn), 0)
        cidx = lax.broadcasted_iota(jnp.int32, (nrow, nn), 1)
        ok = ((r // t_new) % DA_HEADS == cidx % DA_HEADS) & (cidx // DA_HEADS <= r % t_new)
        update([jnp.where(ok, scores(kn_ref[0]), MASK_VALUE)], [vn_ref[0].astype(BF16)])
        on = acc_scr[...] / l_scr[...]
        half = nrow // 2
        o = on[0:half] - lam_ref[0] * on[half:nrow]
        o_ref[0] = _rms(o, g_ref[...]) * out_scale


def _dec_attn(pt, lam, q2, kn, vn, g, ck, cv, layer, pp, t_new, out_scale):
    B, nrow, hw = q2.shape
    n_pages = pt.shape[0] // B
    nn = kn.shape[1]
    rows = PAGE_SIZE * DA_HEADS

    def page_spec(i):
        return pl.BlockSpec((None, None, rows, hw),
                            lambda b, j, pt_ref: (layer, pt_ref[b * n_pages + j * pp + i], 0, 0))

    grid_spec = pltpu.PrefetchScalarGridSpec(
        num_scalar_prefetch=1,
        grid=(B, n_pages // pp),
        in_specs=[pl.BlockSpec(memory_space=pltpu.SMEM),
                  pl.BlockSpec((1, nrow, hw), lambda b, j, pt_ref: (b, 0, 0)),
                  pl.BlockSpec((1, nn, hw), lambda b, j, pt_ref: (b, 0, 0)),
                  pl.BlockSpec((1, nn, hw), lambda b, j, pt_ref: (b, 0, 0)),
                  pl.BlockSpec((1, hw), lambda b, j, pt_ref: (0, 0))]
                 + [page_spec(i) for i in range(pp)] + [page_spec(i) for i in range(pp)],
        out_specs=pl.BlockSpec((1, nrow // 2, hw), lambda b, j, pt_ref: (b, 0, 0)),
        scratch_shapes=[pltpu.VMEM((nrow, 1), F32), pltpu.VMEM((nrow, 1), F32),
                        pltpu.VMEM((nrow, hw), F32)],
    )
    return pl.pallas_call(
        functools.partial(_dec_body, pp=pp, t_new=t_new, out_scale=out_scale),
        grid_spec=grid_spec,
        out_shape=jax.ShapeDtypeStruct((B, nrow // 2, hw), F32),
        compiler_params=_cparams(("parallel", "arbitrary")),
        name="paged_diff_attn",
    )(pt, lam, q2, kn, vn, g, *([ck] * pp), *([cv] * pp))


def _merge_body(x_ref, g_ref, ya_ref, ua_ref, yb_ref, yc_ref, yd_ref, d_ref, wglu_ref,
                wgt_ref, wa_ref, wb_ref, wc_ref, wd_ref, wo_ref, o_ref):
    x = x_ref[...]
    h = _rms(x, g_ref[...]).astype(wgt_ref.dtype)
    z = jax.nn.gelu(ya_ref[...] + d_ref[...] * ua_ref[...])
    ya = z * jax.nn.sigmoid(_mm(z, wglu_ref[...]))
    merged = None
    branches = ((ya, wa_ref), (yb_ref[...], wb_ref), (yc_ref[...], wc_ref), (yd_ref[...], wd_ref))
    for i, (yv, w_ref) in enumerate(branches):
        gate = jax.nn.sigmoid(_mm(h, wgt_ref[:, i * D_MODEL:(i + 1) * D_MODEL]))
        term = gate * _mm(yv, w_ref[...])
        merged = term if merged is None else merged + term
    o_ref[...] = x + _mm(merged, wo_ref[...])


def _merge(x, g, ya, ua, yb, yc, yd, d, wglu, wgt, wa, wb, wc, wd, wo, tm):
    n = x.shape[0]
    row = lambda i: (i, 0)
    fixed = lambda i: (0, 0)
    full = lambda a: pl.BlockSpec(a.shape, fixed, pipeline_mode=pl.Buffered(1))
    return pl.pallas_call(
        _merge_body,
        grid=(n // tm,),
        in_specs=[pl.BlockSpec((tm, D_MODEL), row), full(g),
                  pl.BlockSpec((tm, 256), row), pl.BlockSpec((tm, 256), row),
                  pl.BlockSpec((tm, 256), row), pl.BlockSpec((tm, 256), row),
                  pl.BlockSpec((tm, 512), row),
                  full(d), full(wglu), full(wgt), full(wa), full(wb), full(wc), full(wd), full(wo)],
        out_specs=pl.BlockSpec((tm, D_MODEL), row),
        out_shape=jax.ShapeDtypeStruct((n, D_MODEL), F32),
        compiler_params=_cparams(("parallel",)),
        name="merge",
    )(x, g, ya, ua, yb, yc, yd, d, wglu, wgt, wa, wb, wc, wd, wo)


def _moe_body(x_ref, g_ref, wr_ref, br_ref, wg_ref, wu_ref, wd_ref, gf_ref, o_ref,
              h_scr, gate_scr, acc_scr, *, final_norm):
    gi = pl.program_id(1)
    tm = x_ref.shape[0]
    R = ROUTER_LANES

    @pl.when(gi == 0)
    def _():
        h = _rms(x_ref[...], g_ref[...]).astype(h_scr.dtype)
        h_scr[...] = h
        logits = _mm(h, wr_ref[...]) + br_ref[...]
        lane = lax.broadcasted_iota(jnp.int32, (tm, R), 1)
        lanef = lane.astype(F32)
        neg = -jnp.inf
        is_g = lane < MOE_GROUPS
        glm = jnp.where(is_g, logits, neg)
        gmax = jnp.max(glm, axis=1, keepdims=True)
        gsum = jnp.sum(jnp.where(is_g, jnp.exp(glm - gmax), 0.0), axis=1, keepdims=True)
        g_w = 1.0 / gsum
        g_i = jnp.min(jnp.where(glm == gmax, lanef, float(R)), axis=1, keepdims=True)
        e_grp = ((lane - MOE_GROUPS) // MOE_PER_GROUP).astype(F32)
        sel = (lane >= MOE_GROUPS) & (lane < MOE_GROUPS + MOE_EXPERTS) & (e_grp == g_i)
        elm = jnp.where(sel, logits, neg)
        e1 = jnp.max(elm, axis=1, keepdims=True)
        i1 = jnp.min(jnp.where(elm == e1, lanef, float(R)), axis=1, keepdims=True)
        elm2 = jnp.where(lanef == i1, neg, elm)
        e2 = jnp.max(elm2, axis=1, keepdims=True)
        i2 = jnp.min(jnp.where(elm2 == e2, lanef, float(R)), axis=1, keepdims=True)
        t = jnp.exp(e2 - e1)
        w1 = g_w / (1.0 + t)
        w2 = g_w * t / (1.0 + t)
        gate_scr[...] = jnp.where(lanef == i1, w1, 0.0) + jnp.where(lanef == i2, w2, 0.0)
        acc_scr[...] = jnp.zeros(acc_scr.shape, F32)

    h = h_scr[...]
    hid = jax.nn.silu(_mm(h, wg_ref[...])) * _mm(h, wu_ref[...])
    gw = MOE_PER_GROUP * MOE_HIDDEN
    er = lax.broadcasted_iota(jnp.int32, (R, gw), 0)
    ec = lax.broadcasted_iota(jnp.int32, (R, gw), 1)
    expand = (er == MOE_GROUPS + MOE_PER_GROUP * gi + ec // MOE_HIDDEN).astype(BF16)
    gates = _split_dot(gate_scr[...], expand, 3 if wd_ref.dtype == F32 else 2)
    acc_scr[...] += _mm(hid * gates, wd_ref[...])

    @pl.when(gi == MOE_GROUPS - 1)
    def _():
        o = x_ref[...] + acc_scr[...]
        if final_norm:
            o = _rms(o, gf_ref[...])
        o_ref[...] = o


def _moe(x, g, wr, br, wg, wu, wd, gf, tm, final_norm):
    n = x.shape[0]
    gw = MOE_PER_GROUP * MOE_HIDDEN
    row = lambda i, e: (i, 0)
    fixed = lambda i, e: (0, 0)
    return pl.pallas_call(
        functools.partial(_moe_body, final_norm=final_norm),
        grid=(n // tm, MOE_GROUPS),
        in_specs=[pl.BlockSpec((tm, D_MODEL), row), pl.BlockSpec((1, D_MODEL), fixed),
                  pl.BlockSpec((D_MODEL, ROUTER_LANES), fixed), pl.BlockSpec((1, ROUTER_LANES), fixed),
                  pl.BlockSpec((D_MODEL, gw), lambda i, e: (0, e)),
                  pl.BlockSpec((D_MODEL, gw), lambda i, e: (0, e)),
                  pl.BlockSpec((gw, D_MODEL), lambda i, e: (e, 0)),
                  pl.BlockSpec((1, D_MODEL), fixed)],
        out_specs=pl.BlockSpec((tm, D_MODEL), row),
        out_shape=jax.ShapeDtypeStruct((n, D_MODEL), F32),
        scratch_shapes=[pltpu.VMEM((tm, D_MODEL), wg.dtype), pltpu.VMEM((tm, ROUTER_LANES), F32),
                        pltpu.VMEM((tm, D_MODEL), F32)],
        compiler_params=_cparams(("parallel", "arbitrary")),
        name="moe",
    )(x, g, wr, br, wg, wu, wd, gf)


def _rope_tables(pos):
    half = DA_HEAD_DIM // 2
    inv = 1.0 / (ROPE_THETA ** (jnp.arange(half, dtype=F32) * 2.0 / DA_HEAD_DIM))
    ang = pos.astype(F32)[:, None] * inv[None, :]
    reps = DA_QK_WIDTH // half
    return jnp.tile(jnp.cos(ang), (1, reps)), jnp.tile(jnp.sin(ang), (1, reps))


def _block_diag(w):
    n, a, b = w.shape
    eye = jnp.eye(n, dtype=w.dtype)
    return jnp.einsum('nij,nm->nimj', w, eye).reshape(n * a, n * b)


def _layer_params(l, p, wdt):
    row = lambda a: a.astype(F32).reshape(1, -1)
    w_in = p['w_in'][l]
    p_lb = jax.nn.softmax(p['hg_lb_logits'].astype(F32), axis=0)
    lb = jnp.cumsum(p_lb, axis=0)[l] - p_lb[0]
    lam_init = 0.8 - 0.6 * math.exp(-0.3 * l)
    lam = (jnp.exp(jnp.sum(p['diff_lq1'][l].astype(F32) * p['diff_lk1'][l].astype(F32)))
           - jnp.exp(jnp.sum(p['diff_lq2'][l].astype(F32) * p['diff_lk2'][l].astype(F32))) + lam_init)
    w_router = jnp.concatenate(
        [p['moe_w_grp'][l], p['moe_w_exp'][l],
         jnp.zeros((D_MODEL, ROUTER_LANES - MOE_GROUPS - MOE_EXPERTS), F32)], axis=1)
    b_router = jnp.concatenate(
        [p['moe_b_grp'][l].astype(F32), p['moe_b_exp'][l].astype(F32),
         jnp.zeros((ROUTER_LANES - MOE_GROUPS - MOE_EXPERTS,), F32)]).reshape(1, -1)
    eh = MOE_EXPERTS * MOE_HIDDEN
    return dict(
        norm_mix=row(p['norm_mix'][l]),
        wdt=wdt,
        w_mix=w_in[:, :MIX_COLS].astype(wdt),
        w_gates=w_in[:, MIX_COLS:].astype(wdt),
        s5=(p['ssm_lambda_re'][l], p['ssm_lambda_im'][l], p['ssm_log_dt'][l], p['ssm_b_re'][l],
            p['ssm_b_im'][l], p['ssm_c_re'][l], p['ssm_c_im'][l]),
        ssm_d=row(p['ssm_d'][l]),
        w_glu=p['ssm_w_glu'][l].astype(wdt),
        hg_lb=lb.reshape(1, -1),
        hg_norm=jnp.tile(p['hg_norm'][l].astype(F32), HG_HEADS).reshape(1, -1),
        conv_w=p['lru_conv_w'][l].astype(F32),
        conv_b=row(p['lru_conv_b'][l]),
        w_ax=jnp.concatenate([_block_diag(p['lru_wa'][l]), _block_diag(p['lru_wx'][l])], axis=1).astype(wdt),
        b_ax=jnp.concatenate([p['lru_ba'][l], p['lru_bx'][l]]).astype(F32).reshape(1, -1),
        neg_c_softplus=(-LRU_C * jax.nn.softplus(-p['lru_lambda'][l].astype(F32))).reshape(1, -1),
        lam=lam.reshape(1).astype(F32),
        out_scale=1.0 - lam_init,
        diff_norm=row(p['diff_norm'][l]),
        w_br_a=p['w_br_a'][l].astype(wdt), w_br_b=p['w_br_b'][l].astype(wdt),
        w_br_c=p['w_br_c'][l].astype(wdt), w_br_d=p['w_br_d'][l].astype(wdt),
        w_out=p['w_out'][l].astype(wdt),
        norm_ffn=row(p['norm_ffn'][l]),
        w_router=w_router.astype(wdt), b_router=b_router,
        moe_gate=p['moe_w_gate'][l].astype(wdt).transpose(1, 0, 2).reshape(D_MODEL, eh),
        moe_up=p['moe_w_up'][l].astype(wdt).transpose(1, 0, 2).reshape(D_MODEL, eh),
        moe_down=p['moe_w_down'][l].astype(wdt).reshape(eh, D_MODEL),
    )


def _pad_rows(a, rows):
    return jnp.pad(a, ((0, 0), (0, rows - a.shape[1])) + ((0, 0),) * (a.ndim - 2))


def _trunk(x, pos0, states, cache, page_table, layers, norm_final, cfg):
    B, T, _ = x.shape
    n = B * T
    tm, s5_chunk, tpad, hg_tb, hg_c, lru_tb, tq = (cfg[k] for k in
                                                   ('tm', 's5_chunk', 'tpad', 'hg_tb', 'hg_c', 'lru_tb', 'tq'))
    cos, sin = _rope_tables(pos0 + jnp.arange(T, dtype=jnp.int32))
    if T % tm:
        cos, sin = jnp.tile(cos, (tm // T, 1)), jnp.tile(sin, (tm // T, 1))
    xf = x.reshape(n, D_MODEL)
    ks, vs, sts = [], [], []
    for l, lp in enumerate(layers):
        st = states[l]
        ua, hg, xg, qb, k, kb, v, vt = _inproj(xf, lp['norm_mix'], lp['w_mix'], cos, sin, tm)
        ya, ssm_re, ssm_im = _s5_mixer(ua.reshape(B, T, SSM_WIDTH), st[0], st[1],
                                       _s5_weights(*lp['s5'], s5_chunk, lp['wdt']), s5_chunk)
        hg3 = _pad_rows(hg.reshape(B, T, 4 * HG_WIDTH), tpad)
        yb, s_t = _hgrn(hg3, _hgrn_state_to_t(st[2].astype(F32)), lp['hg_lb'], lp['hg_norm'],
                        hg_tb, hg_c, T, lp['wdt'] == F32)
        hg_state = _hgrn_state_from_t(s_t)
        xg3 = xg.reshape(B, T, 2 * LRU_WIDTH)
        conv0 = jnp.pad(st[4].astype(F32), ((0, 0), (8 - (CONV_WIDTH - 1), 0), (0, 0)))
        yc, lru_h = _lru(_pad_rows(xg3, tpad), conv0, st[3].astype(F32).reshape(B, 1, LRU_WIDTH),
                         lp['conv_w'], lp['conv_b'], lp['w_ax'], lp['b_ax'], lp['neg_c_softplus'],
                         lru_tb, T)
        xp = jnp.concatenate([st[4].astype(F32), xg3[:, :, :LRU_WIDTH]], axis=1)
        conv_buf = xp[:, T:]
        if cache is None:
            yd = _attn(lp['lam'], qb.reshape(B, T, -1), kb.reshape(B, T, -1), vt,
                       lp['diff_norm'], tq, cfg['wide_units'], lp['out_scale'])
        else:
            hw = DA_V_DIM
            q4 = qb.reshape(B, T, DA_HEADS, 2, DA_HEAD_DIM)
            zero = jnp.zeros_like(q4[:, :, :, 0])
            q2 = jnp.stack([jnp.concatenate([q4[:, :, :, 0], zero], -1),
                            jnp.concatenate([zero, q4[:, :, :, 1]], -1)], axis=1)
            q2 = q2.transpose(0, 1, 3, 2, 4).reshape(B, 2 * DA_HEADS * T, hw)
            nn = 128
            kn = _pad_rows(k.reshape(B, T * DA_HEADS, hw), nn)
            vn = _pad_rows(v.reshape(B, T * DA_HEADS, hw), nn)
            o = _dec_attn(page_table.reshape(-1), lp['lam'], q2, kn, vn, lp['diff_norm'],
                          cache[0], cache[1], l, cfg['pp'], T, lp['out_scale'])
            yd = o.reshape(B, DA_HEADS, T, hw).transpose(0, 2, 1, 3).reshape(B, T, DA_WIDTH)
        x1 = _merge(xf, lp['norm_mix'], ya.reshape(n, -1), ua, yb[:, :T].reshape(n, -1),
                    yc[:, :T].reshape(n, -1), yd.reshape(n, -1), lp['ssm_d'], lp['w_glu'], lp['w_gates'],
                    lp['w_br_a'], lp['w_br_b'], lp['w_br_c'], lp['w_br_d'], lp['w_out'], tm)
        xf = _moe(x1, lp['norm_ffn'], lp['w_router'], lp['b_router'], lp['moe_gate'], lp['moe_up'],
                  lp['moe_down'], norm_final, tm, l == len(layers) - 1)
        ks.append(k.reshape(B, T, DA_HEADS, 2 * DA_HEAD_DIM))
        vs.append(v.reshape(B, T, DA_HEADS, DA_V_DIM))
        sts.append((ssm_re, ssm_im, hg_state, lru_h.reshape(B, LRU_WIDTH), conv_buf))
    stacked = [jnp.stack([s[j] for s in sts]) for j in range(5)]
    return xf.reshape(B, T, D_MODEL), jnp.stack(ks), jnp.stack(vs), stacked


PROMPT_CFG = dict(tm=512, s5_chunk=16, tpad=2048, hg_tb=128, hg_c=16, lru_tb=256, tq=256, wide_units=2)
SAMPLE_CFG = dict(tm=128, s5_chunk=4, tpad=16, hg_tb=16, hg_c=16, lru_tb=16, tq=0, pp=8)


def kernel(x_prompt, x_sample, cache_k, cache_v, page_table, state_ssm_re, state_ssm_im, state_hgrn,
           state_lru, state_conv, norm_mix, w_in, ssm_lambda_re, ssm_lambda_im, ssm_log_dt, ssm_b_re,
           ssm_b_im, ssm_c_re, ssm_c_im, ssm_d, ssm_w_glu, hg_lb_logits, hg_norm, lru_conv_w, lru_conv_b,
           lru_wa, lru_ba, lru_wx, lru_bx, lru_lambda, diff_lq1, diff_lk1, diff_lq2, diff_lk2, diff_norm,
           w_br_a, w_br_b, w_br_c, w_br_d, w_out, norm_ffn, moe_w_grp, moe_b_grp, moe_w_exp, moe_b_exp,
           moe_w_gate, moe_w_up, moe_w_down, norm_final):
    p = dict(norm_mix=norm_mix, w_in=w_in, ssm_lambda_re=ssm_lambda_re, ssm_lambda_im=ssm_lambda_im,
             ssm_log_dt=ssm_log_dt, ssm_b_re=ssm_b_re, ssm_b_im=ssm_b_im, ssm_c_re=ssm_c_re,
             ssm_c_im=ssm_c_im, ssm_d=ssm_d, ssm_w_glu=ssm_w_glu, hg_lb_logits=hg_lb_logits,
             hg_norm=hg_norm, lru_conv_w=lru_conv_w, lru_conv_b=lru_conv_b, lru_wa=lru_wa, lru_ba=lru_ba,
             lru_wx=lru_wx, lru_bx=lru_bx, lru_lambda=lru_lambda, diff_lq1=diff_lq1, diff_lk1=diff_lk1,
             diff_lq2=diff_lq2, diff_lk2=diff_lk2, diff_norm=diff_norm, w_br_a=w_br_a, w_br_b=w_br_b,
             w_br_c=w_br_c, w_br_d=w_br_d, w_out=w_out, norm_ffn=norm_ffn, moe_w_grp=moe_w_grp,
             moe_b_grp=moe_b_grp, moe_w_exp=moe_w_exp, moe_b_exp=moe_b_exp, moe_w_gate=moe_w_gate,
             moe_w_up=moe_w_up, moe_w_down=moe_w_down)
    layers = [_layer_params(l, p, BF16) for l in range(DEPTH)]
    layers_f32 = [_layer_params(l, p, F32) for l in range(DEPTH)]
    gf = norm_final.astype(F32).reshape(1, -1)
    Bp = x_prompt.shape[0]
    Bs = x_sample.shape[0]
    zero_states = [(jnp.zeros((Bp, SSM_GROUPS, SSM_STATE), F32), jnp.zeros((Bp, SSM_GROUPS, SSM_STATE), F32),
                    jnp.zeros((Bp, HG_HEADS, HG_HEAD_DIM, HG_HEAD_DIM), F32), jnp.zeros((Bp, LRU_WIDTH), F32),
                    jnp.zeros((Bp, CONV_WIDTH - 1, LRU_WIDTH), F32)) for _ in range(DEPTH)]
    y_p, k_p, v_p, st_p = _trunk(x_prompt, 0, zero_states, None, None, layers, gf, PROMPT_CFG)
    past_len = page_table.shape[1] * PAGE_SIZE
    sample_states = [(state_ssm_re[l], state_ssm_im[l], state_hgrn[l], state_lru[l], state_conv[l])
                     for l in range(DEPTH)]
    n_pool = cache_k.shape[1]
    rows = PAGE_SIZE * DA_HEADS
    cache = (cache_k.reshape(DEPTH, n_pool, rows, 2 * DA_HEAD_DIM), cache_v.reshape(DEPTH, n_pool, rows, DA_V_DIM))
    y_s, k_s, v_s, st_s = _trunk(x_sample, past_len, sample_states, cache, page_table, layers_f32, gf,
                                 SAMPLE_CFG)
    return (y_p, y_s, k_p, v_p, k_s, v_s,
            st_p[0], st_p[1], st_s[0], st_s[1], st_p[2], st_s[2], st_p[3], st_s[3], st_p[4], st_s[4])
```

```python
import functools
import math

import jax
import jax.numpy as jnp
from jax import lax
from jax.experimental import pallas as pl
from jax.experimental.pallas import tpu as pltpu

F32 = jnp.float32
BF16 = jnp.bfloat16

D_MODEL = 1024
DEPTH = 2
PAGE_SIZE = 128
SSM_WIDTH = 256
SSM_GROUP = 16
SSM_GROUPS = 16
SSM_STATE = 64
HG_WIDTH = 256
HG_HEAD_DIM = 64
HG_HEADS = 4
LRU_WIDTH = 256
LRU_BLOCKS = 4
LRU_BLOCK = 64
CONV_WIDTH = 4
LRU_C = 8.0
DA_HEADS = 4
DA_HEAD_DIM = 64
DA_V_DIM = 128
DA_QK_WIDTH = 512
DA_WIDTH = 512
ROPE_THETA = 10000.0
MASK_VALUE = -1e30
N_BRANCH = 4
MOE_GROUPS = 4
MOE_PER_GROUP = 8
MOE_EXPERTS = 32
MOE_HIDDEN = 128
NORM_EPS = 1e-6
MIX_COLS = 3328
ROUTER_LANES = 128
VMEM_LIMIT = 56 * 1024 * 1024
HI = lax.Precision.HIGHEST
Q_SCALE = DA_HEAD_DIM ** -0.5 * math.log2(math.e)
KV_UNIT = 256


def _cparams(sem):
    return pltpu.CompilerParams(dimension_semantics=sem, vmem_limit_bytes=VMEM_LIMIT)


def _rms(x, g):
    return x * lax.rsqrt(jnp.mean(x * x, axis=-1, keepdims=True) + NORM_EPS) * g


def _dot(a, b):
    return jnp.dot(a, b, preferred_element_type=F32)


def _mm(a, w, dims=(((1,), (0,)), ((), ()))):
    if w.dtype == BF16:
        return lax.dot_general(a.astype(BF16), w, dims, preferred_element_type=F32)
    return lax.dot_general(a.astype(F32), w, dims, preferred_element_type=F32, precision=HI)


NT_DIMS = (((1,), (1,)), ((), ()))
TN_DIMS = (((0,), (0,)), ((), ()))


def _dot_nt(a, b):
    return lax.dot_general(a, b, (((1,), (1,)), ((), ())), preferred_element_type=F32)


def _dot_tn(a, b):
    return lax.dot_general(a, b, (((0,), (0,)), ((), ())), preferred_element_type=F32)


def _split_dot(a, b_bf16, terms):
    out = None
    rem = a
    for _ in range(terms):
        piece = rem.astype(BF16)
        part = _dot(piece, b_bf16)
        out = part if out is None else out + part
        rem = rem - piece.astype(F32)
    return out


def _head_ones(width, head):
    r = lax.broadcasted_iota(jnp.int32, (width, width), 0) // head
    c = lax.broadcasted_iota(jnp.int32, (width, width), 1) // head
    return r == c


def _inproj_body(x_ref, g_ref, w_ref, cos_ref, sin_ref,
                 ua_ref, hg_ref, lru_ref, q_ref, k_ref, kb_ref, v_ref, vt_ref, uc_ref, ulo_scr, uhi_scr):
    h = _rms(x_ref[...], g_ref[...]).astype(w_ref.dtype)

    def mm(a, b):
        return _mm(h, w_ref[:, a:b])

    ua = mm(0, 256)
    ua_ref[...] = ua
    chunk = uc_ref.shape[1] // SSM_WIDTH
    half = SSM_WIDTH // 2
    ulo_scr[...] = ua[:, 0:half]
    uhi_scr[...] = ua[:, half:SSM_WIDTH]
    for t in range(chunk):
        rows = pl.ds(t, uc_ref.shape[0], stride=chunk)
        uc_ref[:, t * SSM_WIDTH:t * SSM_WIDTH + half] = ulo_scr[rows, :].astype(uc_ref.dtype)
        uc_ref[:, t * SSM_WIDTH + half:(t + 1) * SSM_WIDTH] = uhi_scr[rows, :].astype(uc_ref.dtype)
    hg_ref[...] = mm(256, 1280)
    lru_ref[...] = mm(1280, 1792)
    cos = cos_ref[...]
    sin = sin_ref[...]
    lane = lax.broadcasted_iota(jnp.int32, cos.shape, 1)
    first = (lane % DA_HEAD_DIM) < (DA_HEAD_DIM // 2)

    def rope(z):
        rot = jnp.where(first, -pltpu.roll(z, DA_QK_WIDTH - DA_HEAD_DIM // 2, 1),
                        pltpu.roll(z, DA_HEAD_DIM // 2, 1))
        return z * cos + rot * sin

    q = rope(mm(1792, 2304))
    q_ref[...] = (q * Q_SCALE).astype(q_ref.dtype)
    k = rope(mm(2304, 2816))
    kb_ref[...] = k.astype(BF16)
    v = mm(2816, 3328)
    tm = k.shape[0]
    unit = vt_ref.shape[2]
    for u in range(tm // unit):
        vt_ref[u] = v[u * unit:(u + 1) * unit, :].T.astype(BF16)
    for hd in range(DA_HEADS):
        cs = slice(hd * DA_V_DIM, (hd + 1) * DA_V_DIM)
        k_ref[pl.ds(hd, tm, stride=DA_HEADS), :] = k[:, cs]
        v_ref[pl.ds(hd, tm, stride=DA_HEADS), :] = v[:, cs]


def _inproj(x, g, w, cos, sin, tm, chunk):
    n = x.shape[0]
    ntab = cos.shape[0] // tm
    row = lambda i: (i, 0)
    fixed = lambda i: (0, 0)
    tab = lambda i: (i % ntab, 0)
    outs = ((1, 256, F32), (1, 1024, F32), (1, 512, F32), (1, 512, w.dtype),
            (DA_HEADS, DA_V_DIM, F32), (1, 512, BF16), (DA_HEADS, DA_V_DIM, F32))
    unit = min(tm, KV_UNIT)
    return pl.pallas_call(
        _inproj_body,
        grid=(n // tm,),
        in_specs=[pl.BlockSpec((tm, D_MODEL), row), pl.BlockSpec((1, D_MODEL), fixed),
                  pl.BlockSpec((D_MODEL, MIX_COLS), fixed),
                  pl.BlockSpec((tm, 512), tab), pl.BlockSpec((tm, 512), tab)],
        out_specs=[pl.BlockSpec((tm * r, wd), row) for r, wd, _ in outs]
                  + [pl.BlockSpec((tm // unit, DA_WIDTH, unit), lambda i: (i, 0, 0)),
                     pl.BlockSpec((tm // chunk, chunk * SSM_WIDTH), row)],
        out_shape=[jax.ShapeDtypeStruct((n * r, wd), dt) for r, wd, dt in outs]
                  + [jax.ShapeDtypeStruct((n // unit, DA_WIDTH, unit), BF16),
                     jax.ShapeDtypeStruct((n // chunk, chunk * SSM_WIDTH), w.dtype)],
        scratch_shapes=[pltpu.VMEM((tm, SSM_WIDTH // 2), F32), pltpu.VMEM((tm, SSM_WIDTH // 2), F32)],
        compiler_params=_cparams(("parallel",)),
        name="inproj",
    )(x, g, w, cos, sin)


S5_STATE_LANES = SSM_GROUPS * SSM_STATE
S5_HALF = S5_STATE_LANES // 2


def _s5_body(u_ref, k_ref, p_ref, q_ref, a_ref, h0_ref, y_ref, hf_ref, pu_scr, hs_scr, *, L, bt, nc):
    W = SSM_WIDTH
    HW = W // 2
    SL, SH = S5_STATE_LANES, S5_HALF
    for half in range(2):
        acc = None
        for t in range(L):
            c0 = t * W + half * HW
            part = _mm(u_ref[:, c0:c0 + HW], p_ref[t, half])
            acc = part if acc is None else acc + part
        pu_scr[:, half * SH:(half + 1) * SH] = acc[:, 0:SH]
        pu_scr[:, SL + half * SH:SL + (half + 1) * SH] = acc[:, SH:2 * SH]
    ar2 = a_ref[0:1, :]
    ai2 = a_ref[1:2, :]

    assert bt == 1 or nc == 1

    def step(c, h):
        rows = pl.ds(c * bt, bt)
        hs_scr[rows, :] = h
        return ar2 * h + ai2 * pltpu.roll(h, SL, 1) + pu_scr[rows, :]

    hf_ref[0] = lax.fori_loop(0, nc, step, h0_ref[0])
    hs = [jnp.concatenate([hs_scr[:, half * SH:(half + 1) * SH],
                           hs_scr[:, SL + half * SH:SL + (half + 1) * SH]], axis=1).astype(k_ref.dtype)
          for half in range(2)]
    for t2 in range(L):
        acc = jnp.concatenate([_mm(hs[0], q_ref[t2, 0]), _mm(hs[1], q_ref[t2, 1])], axis=1)
        for t in range(t2 + 1):
            acc = acc + _mm(u_ref[:, t * W:(t + 1) * W], k_ref[t2 - t])
        y_ref[:, t2 * W:(t2 + 1) * W] = acc


def _s5(ucat, kbd, pmat, qmat, a2, h0, bt, nc):
    rows_all, lw = ucat.shape
    L = lw // SSM_WIDTH
    rows = bt * nc
    once = lambda a: pl.BlockSpec(a.shape, lambda i: (0,) * a.ndim, pipeline_mode=pl.Buffered(1))
    return pl.pallas_call(
        functools.partial(_s5_body, L=L, bt=bt, nc=nc),
        grid=(rows_all // rows,),
        in_specs=[pl.BlockSpec((rows, lw), lambda i: (i, 0)), once(kbd), once(pmat), once(qmat), once(a2),
                  pl.BlockSpec((1, bt, 2 * S5_STATE_LANES), lambda i: (i, 0, 0))],
        out_specs=[pl.BlockSpec((rows, lw), lambda i: (i, 0)),
                   pl.BlockSpec((1, bt, 2 * S5_STATE_LANES), lambda i: (i, 0, 0))],
        out_shape=[jax.ShapeDtypeStruct((rows_all, lw), F32), jax.ShapeDtypeStruct(h0.shape, F32)],
        scratch_shapes=[pltpu.VMEM((rows, 2 * S5_STATE_LANES), F32), pltpu.VMEM((rows, 2 * S5_STATE_LANES), F32)],
        compiler_params=_cparams(("parallel",)),
        name="s5",
    )(ucat, kbd, pmat, qmat, a2, h0)


def _s5_weights(lam_re, lam_im, log_dt, b_re, b_im, c_re, c_im, L, wdtype):
    G, P, J = SSM_GROUPS, SSM_STATE, SSM_GROUP
    lr, li = lam_re.astype(F32), lam_im.astype(F32)
    dt = jnp.exp(log_dt.astype(F32))[:, None]
    mag = jnp.exp(lr * dt)
    ar = mag * jnp.cos(li * dt)
    ai = mag * jnp.sin(li * dt)
    den = lr * lr + li * li
    fr = ((ar - 1.0) * lr + ai * li) / den
    fi = (ai * lr - (ar - 1.0) * li) / den
    br, bi = b_re.astype(F32), b_im.astype(F32)
    bbr = fr[..., None] * br - fi[..., None] * bi
    bbi = fr[..., None] * bi + fi[..., None] * br
    tau = jnp.arange(L + 1, dtype=F32)[:, None, None]
    pmag = jnp.exp(lr * dt * tau)
    pr = pmag * jnp.cos(li * dt * tau)
    pi = pmag * jnp.sin(li * dt * tau)
    t1r = pr[..., None] * bbr - pi[..., None] * bbi
    t1i = pr[..., None] * bbi + pi[..., None] * bbr
    cr, ci = c_re.astype(F32), c_im.astype(F32)
    kt = (jnp.einsum('gip,tgpj->tgij', cr, t1r, precision=HI)
          - jnp.einsum('gip,tgpj->tgij', ci, t1i, precision=HI))
    GH = G // 2
    eye, eye_h = jnp.eye(G, dtype=F32), jnp.eye(GH, dtype=F32)
    kbd = jnp.einsum('tgij,gh->tgjhi', kt[:L], eye).reshape(L, G * J, G * J)
    rev = L - 1 - jnp.arange(L)
    ph = jnp.stack([t1r[rev], t1i[rev]], axis=1).reshape(L, 2, 2, GH, P, J)
    pmat = jnp.einsum('trhgpj,gk->thgjrkp', ph, eye_h).reshape(L, 2, GH * J, 2 * GH * P)
    car = cr[None] * pr[1:, :, None, :] - ci[None] * pi[1:, :, None, :]
    cai = cr[None] * pi[1:, :, None, :] + ci[None] * pr[1:, :, None, :]
    qh = jnp.stack([car, -cai], axis=1).reshape(L, 2, 2, GH, J, P)
    qmat = jnp.einsum('trhgip,gk->thrgpki', qh, eye_h).reshape(L, 2, 2 * GH * P, GH * J)
    a_l = jnp.stack([jnp.concatenate([pr[L].reshape(-1), pr[L].reshape(-1)]),
                     jnp.concatenate([-pi[L].reshape(-1), pi[L].reshape(-1)])])
    return kbd.astype(wdtype), pmat.astype(wdtype), qmat.astype(wdtype), a_l


def _s5_mixer(ucat, h0_re, h0_im, wts, bt, nc):
    B = h0_re.shape[0]
    h0 = jnp.concatenate([h0_re.reshape(B, -1), h0_im.reshape(B, -1)], axis=-1).astype(F32)
    y, hf = _s5(ucat, *wts, h0.reshape(B // bt, bt, -1), bt, nc)
    hf = hf.reshape(B, 2, SSM_GROUPS, SSM_STATE)
    return y, hf[:, 0], hf[:, 1]


def _hgrn_body(hg_ref, s0_ref, lb_ref, ng_ref, y_ref, sf_ref,
               st_scr, k_scr, b_scr, v_scr, w_scr, *, tb, c, t_valid, precise):
    j = pl.program_id(1)
    W = HG_WIDTH
    mdt = F32 if precise else BF16

    @pl.when(j == 0)
    def _():
        st_scr[...] = s0_ref[0]
        k_scr[0:c, :] = jnp.zeros((c, W), F32)
        b_scr[0:c, :] = jnp.zeros((c, W), F32)
        v_scr[0:c, :] = jnp.zeros((c, W), F32)

    q = hg_ref[0, :, 0:W]
    lb = lb_ref[...]
    fv = lb + (1.0 - lb) * jax.nn.sigmoid(hg_ref[0, :, W:2 * W])
    logf = jnp.log(fv)
    kk = 1.0 - fv
    v = hg_ref[0, :, 2 * W:3 * W]
    row = lax.broadcasted_iota(jnp.int32, (tb, W), 0)
    if t_valid < tb:
        valid = row < t_valid
        logf = jnp.where(valid, logf, 0.0)
        kk = jnp.where(valid, kk, 0.0)
    ri = lax.broadcasted_iota(jnp.int32, (tb, tb), 0)
    ci = lax.broadcasted_iota(jnp.int32, (tb, tb), 1)
    tril = ((ri // c == ci // c) & (ci <= ri)).astype(BF16)
    b = _split_dot_lhs_exact(tril, logf)
    k_scr[c:c + tb, :] = kk
    b_scr[c:c + tb, :] = b
    v_scr[c:c + tb, :] = v
    rin = row % c
    for d in range(c):
        ksh = k_scr[c - d:c - d + tb, :]
        bsh = b_scr[c - d:c - d + tb, :]
        w = jnp.where(rin >= d, q * ksh * jnp.exp(b - bsh), 0.0)
        w_scr[d * tb:(d + 1) * tb, :] = w.astype(mdt)
    ones_bd = _head_ones(W, HG_HEAD_DIM).astype(BF16)
    att = _mm(w_scr[...], ones_bd.astype(mdt))
    o = att[0:tb] * v
    for d in range(1, c):
        o = o + att[d * tb:(d + 1) * tb] * v_scr[c - d:c - d + tb, :]
    bd = _head_ones(W, HG_HEAD_DIM)
    outs = []
    for ch in range(tb // c):
        sl = slice(ch * c, (ch + 1) * c)
        bc = b[sl]
        bl = bc[c - 1:c, :]
        st = st_scr[...]
        outs.append(_mm(q[sl] * jnp.exp(bc), st.astype(mdt), NT_DIMS))
        khat = (kk[sl] * jnp.exp(bl - bc)).astype(mdt)
        upd = _mm(v[sl], khat, TN_DIMS)
        st_scr[...] = st * jnp.exp(bl) + jnp.where(bd, upd, 0.0)
    o = o + jnp.concatenate(outs, axis=0) if len(outs) > 1 else o + outs[0]
    ms = _split_dot(o * o, ones_bd, 3 if precise else 2) * (1.0 / HG_HEAD_DIM)
    y = o * lax.rsqrt(ms + NORM_EPS) * ng_ref[...]
    y_ref[0] = y * jax.nn.silu(hg_ref[0, :, 3 * W:4 * W])
    sf_ref[0] = st_scr[...]


def _split_dot_lhs_exact(a_bf16, b):
    out = None
    rem = b
    for _ in range(3):
        piece = rem.astype(BF16)
        part = _dot(a_bf16, piece)
        out = part if out is None else out + part
        rem = rem - piece.astype(F32)
    return out


def _hgrn(hg, s0t, lb, ng, tb, c, t_valid, precise):
    B, T, _ = hg.shape
    W = HG_WIDTH
    return pl.pallas_call(
        functools.partial(_hgrn_body, tb=tb, c=c, t_valid=t_valid, precise=precise),
        grid=(B, T // tb),
        in_specs=[pl.BlockSpec((1, tb, 4 * W), lambda b, j: (b, j, 0)),
                  pl.BlockSpec((1, W, W), lambda b, j: (b, 0, 0)),
                  pl.BlockSpec((1, W), lambda b, j: (0, 0)),
                  pl.BlockSpec((1, W), lambda b, j: (0, 0))],
        out_specs=[pl.BlockSpec((1, tb, W), lambda b, j: (b, j, 0)),
                   pl.BlockSpec((1, W, W), lambda b, j: (b, 0, 0))],
        out_shape=[jax.ShapeDtypeStruct((B, T, W), F32), jax.ShapeDtypeStruct((B, W, W), F32)],
        scratch_shapes=[pltpu.VMEM((W, W), F32), pltpu.VMEM((c + tb, W), F32),
                        pltpu.VMEM((c + tb, W), F32), pltpu.VMEM((c + tb, W), F32),
                        pltpu.VMEM((c * tb, W), F32 if precise else BF16)],
        compiler_params=_cparams(("parallel", "arbitrary")),
        name="hgrn2",
    )(hg, s0t, lb, ng)


def _hgrn_state_to_t(s):
    B = s.shape[0]
    eye = jnp.eye(HG_HEADS, dtype=s.dtype)
    return jnp.einsum('bhde,hg->bhegd', s, eye).reshape(B, HG_WIDTH, HG_WIDTH)


def _hgrn_state_from_t(st):
    B = st.shape[0]
    s5 = st.reshape(B, HG_HEADS, HG_HEAD_DIM, HG_HEADS, HG_HEAD_DIM)
    idx = jnp.arange(HG_HEADS)
    return s5[:, idx, :, idx, :].transpose(1, 0, 3, 2)


def _lru_body(x_ref, c0_ref, h0_ref, cw_ref, cb_ref, wax_ref, bax_ref, nsp_ref, y_ref, hl_ref,
              xs_scr, hc_scr, *, tb, r_last):
    j = pl.program_id(1)
    W = LRU_WIDTH

    @pl.when(j == 0)
    def _():
        xs_scr[0:8, :] = c0_ref[0]
        hc_scr[...] = h0_ref[0]

    x = x_ref[0, :, 0:W]
    xs_scr[8:8 + tb, :] = x
    xc = cb_ref[...] + x * cw_ref[3:4, :]
    for jj in range(CONV_WIDTH - 1):
        xc = xc + xs_scr[5 + jj:5 + jj + tb, :] * cw_ref[jj:jj + 1, :]
    tail = xs_scr[tb:tb + 8, :]
    xs_scr[0:8, :] = tail
    rg = _mm(xc, wax_ref[...]) + bax_ref[...]
    r = jax.nn.sigmoid(rg[:, 0:W])
    ig = jax.nn.sigmoid(rg[:, W:2 * W])
    log_a = nsp_ref[...] * r
    a = jnp.exp(log_a)
    bt = jnp.sqrt(jnp.maximum(1.0 - a * a, 0.0)) * (ig * xc)
    row = lax.broadcasted_iota(jnp.int32, (tb, W), 0)
    k = 1
    while k < tb:
        keep = row >= k
        a_sh = jnp.where(keep, pltpu.roll(a, k, 0), 1.0)
        b_sh = jnp.where(keep, pltpu.roll(bt, k, 0), 0.0)
        bt = a * b_sh + bt
        a = a * a_sh
        k *= 2
    h = a * hc_scr[...] + bt
    y_ref[0] = jax.nn.gelu(x_ref[0, :, W:2 * W]) * h
    hc = h[r_last:r_last + 1, :]
    hc_scr[...] = hc
    hl_ref[0] = hc


def _lru(xg, c0, h0, cw, cb, wax, bax, nsp, tb, t_valid):
    B, T, _ = xg.shape
    W = LRU_WIDTH
    fixed = lambda b, j: (0, 0)
    return pl.pallas_call(
        functools.partial(_lru_body, tb=tb, r_last=(t_valid - 1) % tb),
        grid=(B, T // tb),
        in_specs=[pl.BlockSpec((1, tb, 2 * W), lambda b, j: (b, j, 0)),
                  pl.BlockSpec((1, 8, W), lambda b, j: (b, 0, 0)),
                  pl.BlockSpec((1, 1, W), lambda b, j: (b, 0, 0)),
                  pl.BlockSpec((CONV_WIDTH, W), fixed), pl.BlockSpec((1, W), fixed),
                  pl.BlockSpec((W, 2 * W), fixed), pl.BlockSpec((1, 2 * W), fixed),
                  pl.BlockSpec((1, W), fixed)],
        out_specs=[pl.BlockSpec((1, tb, W), lambda b, j: (b, j, 0)),
                   pl.BlockSpec((1, 1, W), lambda b, j: (b, 0, 0))],
        out_shape=[jax.ShapeDtypeStruct((B, T, W), F32), jax.ShapeDtypeStruct((B, 1, W), F32)],
        scratch_shapes=[pltpu.VMEM((8 + tb, W), F32), pltpu.VMEM((1, W), F32)],
        compiler_params=_cparams(("parallel", "arbitrary")),
        name="rglru",
    )(xg, c0, h0, cw, cb, wax, bax, nsp)


def _attn_body(lam_ref, q_ref, k_ref, vt_ref, g_ref, o_ref, q2_scr, m_scr, l_scr, acc_scr,
               *, tq, wide_units, out_scale):
    qi = pl.program_id(1)
    hw = DA_V_DIM
    lane = lax.broadcasted_iota(jnp.int32, (tq, hw), 1)
    for h in range(DA_HEADS):
        qh = q_ref[0, :, h * hw:(h + 1) * hw]
        zero = jnp.zeros_like(qh)
        q2_scr[h, 0:tq, :] = jnp.where(lane < DA_HEAD_DIM, qh, zero)
        q2_scr[h, tq:2 * tq, :] = jnp.where(lane >= DA_HEAD_DIM, qh, zero)
    m_scr[...] = jnp.full(m_scr.shape, MASK_VALUE, F32)
    l_scr[...] = jnp.zeros(l_scr.shape, F32)
    acc_scr[...] = jnp.zeros(acc_scr.shape, F32)

    def block(u0, nu, diagonal):
        tk = nu * KV_UNIT
        r0 = pl.multiple_of(u0 * KV_UNIT, KV_UNIT)
        for h in range(DA_HEADS):
            cs = slice(h * hw, (h + 1) * hw)
            st = _dot_nt(k_ref[0, pl.ds(r0, tk), cs], q2_scr[h])
            if diagonal:
                keyi = lax.broadcasted_iota(jnp.int32, (tk, 2 * tq), 0)
                qryi = lax.broadcasted_iota(jnp.int32, (tk, 2 * tq), 1) % tq
                st = jnp.where(keyi <= qryi, st, MASK_VALUE)
            m = m_scr[h]
            m_new = jnp.maximum(m, jnp.max(st, axis=0, keepdims=True))
            alpha = jnp.exp2(m - m_new)
            p = jnp.exp2(st - m_new)
            l_scr[h] = alpha * l_scr[h] + jnp.sum(p, axis=0, keepdims=True)
            pb = p.astype(BF16)
            pv = _dot(vt_ref[u0, cs, :], pb[0:KV_UNIT])
            for u in range(1, nu):
                pv = pv + _dot(vt_ref[u0 + u, cs, :], pb[u * KV_UNIT:(u + 1) * KV_UNIT])
            acc_scr[h] = alpha * acc_scr[h] + pv
            m_scr[h] = m_new

    nq = tq // KV_UNIT
    n_before = qi * nq
    n_wide = n_before // wide_units

    def wide_step(jb, carry):
        block(jb * wide_units, wide_units, False)
        return carry

    lax.fori_loop(0, n_wide, wide_step, 0)

    def unit_step(u, carry):
        block(u, 1, False)
        return carry

    lax.fori_loop(n_wide * wide_units, n_before, unit_step, 0)
    block(n_before, nq, True)
    lam = lam_ref[0]
    for h in range(DA_HEADS):
        on = acc_scr[h] / l_scr[h]
        o = (on[:, 0:tq] - lam * on[:, tq:2 * tq]).T
        o_ref[0, :, h * hw:(h + 1) * hw] = _rms(o, g_ref[...]) * out_scale


def _attn(lam, q, k, vt, g, tq, wide_units, out_scale):
    B, T, Wd = q.shape
    hw = DA_V_DIM
    units = T // KV_UNIT
    return pl.pallas_call(
        functools.partial(_attn_body, tq=tq, wide_units=wide_units, out_scale=out_scale),
        grid=(B, T // tq),
        in_specs=[pl.BlockSpec(memory_space=pltpu.SMEM),
                  pl.BlockSpec((1, tq, Wd), lambda b, i: (b, i, 0)),
                  pl.BlockSpec((1, T, Wd), lambda b, i: (b, 0, 0)),
                  pl.BlockSpec((units, Wd, KV_UNIT), lambda b, i: (b, 0, 0)),
                  pl.BlockSpec((1, hw), lambda b, i: (0, 0))],
        out_specs=pl.BlockSpec((1, tq, Wd), lambda b, i: (b, i, 0)),
        out_shape=jax.ShapeDtypeStruct((B, T, Wd), F32),
        scratch_shapes=[pltpu.VMEM((DA_HEADS, 2 * tq, hw), BF16), pltpu.VMEM((DA_HEADS, 1, 2 * tq), F32),
                        pltpu.VMEM((DA_HEADS, 1, 2 * tq), F32), pltpu.VMEM((DA_HEADS, hw, 2 * tq), F32)],
        compiler_params=_cparams(("parallel", "arbitrary")),
        name="diff_attn",
    )(lam, q, k, vt, g)


def _dec_body(pt_ref, lam_ref, q_ref, kn_ref, vn_ref, g_ref, *rest, pp, t_new, out_scale):
    k_refs = rest[0:pp]
    v_refs = rest[pp:2 * pp]
    o_ref = rest[2 * pp]
    m_scr, l_scr, acc_scr = rest[2 * pp + 1:]
    j = pl.program_id(1)
    nrow = 2 * DA_HEADS * t_new

    @pl.when(j == 0)
    def _():
        m_scr[...] = jnp.full(m_scr.shape, MASK_VALUE, F32)
        l_scr[...] = jnp.zeros(l_scr.shape, F32)
        acc_scr[...] = jnp.zeros(acc_scr.shape, F32)

    def hi_lo(a):
        hi = a.astype(BF16)
        return jnp.concatenate([hi, (a - hi.astype(F32)).astype(BF16)], axis=0)

    def fold(a):
        return a[0:nrow] + a[nrow:2 * nrow]

    q = hi_lo(q_ref[0])

    def scores(keys):
        return fold(_dot_nt(q, keys.astype(BF16)))

    def update(ss, vals):
        m = m_scr[...]
        smax = ss[0]
        for s in ss[1:]:
            smax = jnp.maximum(smax, s)
        m_new = jnp.maximum(m, jnp.max(smax, axis=1, keepdims=True))
        alpha = jnp.exp2(m - m_new)
        ps = [jnp.exp2(s - m_new) for s in ss]
        psum = ps[0]
        for p in ps[1:]:
            psum = psum + p
        pv = _dot(hi_lo(ps[0]), vals[0])
        for p, vv in zip(ps[1:], vals[1:]):
            pv = pv + _dot(hi_lo(p), vv)
        l_scr[...] = alpha * l_scr[...] + jnp.sum(psum, axis=1, keepdims=True)
        acc_scr[...] = alpha * acc_scr[...] + fold(pv)
        m_scr[...] = m_new

    ncol = PAGE_SIZE * DA_HEADS
    rh = (lax.broadcasted_iota(jnp.int32, (nrow, ncol), 0) // t_new) % DA_HEADS
    chd = lax.broadcasted_iota(jnp.int32, (nrow, ncol), 1) % DA_HEADS
    same_head = rh == chd
    update([jnp.where(same_head, scores(k_refs[i][...]), MASK_VALUE) for i in range(pp)],
           [v_refs[i][...].astype(BF16) for i in range(pp)])

    @pl.when(j == pl.num_programs(1) - 1)
    def _():
        nn = kn_ref.shape[1]
        r = lax.broadcasted_iota(jnp.int32, (nrow, nn), 0)
        cidx = lax.broadcasted_iota(jnp.int32, (nrow, nn), 1)
        ok = ((r // t_new) % DA_HEADS == cidx % DA_HEADS) & (cidx // DA_HEADS <= r % t_new)
        update([jnp.where(ok, scores(kn_ref[0]), MASK_VALUE)], [vn_ref[0].astype(BF16)])
        on = acc_scr[...] / l_scr[...]
        half = nrow // 2
        o = on[0:half] - lam_ref[0] * on[half:nrow]
        o_ref[0] = _rms(o, g_ref[...]) * out_scale


def _dec_attn(pt, lam, q2, kn, vn, g, ck, cv, layer, pp, t_new, out_scale):
    B, nrow, hw = q2.shape
    n_pages = pt.shape[0] // B
    nn = kn.shape[1]
    rows = PAGE_SIZE * DA_HEADS

    def page_spec(i):
        return pl.BlockSpec((None, None, rows, hw),
                            lambda b, j, pt_ref: (layer, pt_ref[b * n_pages + j * pp + i], 0, 0))

    grid_spec = pltpu.PrefetchScalarGridSpec(
        num_scalar_prefetch=1,
        grid=(B, n_pages // pp),
        in_specs=[pl.BlockSpec(memory_space=pltpu.SMEM),
                  pl.BlockSpec((1, nrow, hw), lambda b, j, pt_ref: (b, 0, 0)),
                  pl.BlockSpec((1, nn, hw), lambda b, j, pt_ref: (b, 0, 0)),
                  pl.BlockSpec((1, nn, hw), lambda b, j, pt_ref: (b, 0, 0)),
                  pl.BlockSpec((1, hw), lambda b, j, pt_ref: (0, 0))]
                 + [page_spec(i) for i in range(pp)] + [page_spec(i) for i in range(pp)],
        out_specs=pl.BlockSpec((1, nrow // 2, hw), lambda b, j, pt_ref: (b, 0, 0)),
        scratch_shapes=[pltpu.VMEM((nrow, 1), F32), pltpu.VMEM((nrow, 1), F32),
                        pltpu.VMEM((nrow, hw), F32)],
    )
    return pl.pallas_call(
        functools.partial(_dec_body, pp=pp, t_new=t_new, out_scale=out_scale),
        grid_spec=grid_spec,
        out_shape=jax.ShapeDtypeStruct((B, nrow // 2, hw), F32),
        compiler_params=_cparams(("parallel", "arbitrary")),
        name="paged_diff_attn",
    )(pt, lam, q2, kn, vn, g, *([ck] * pp), *([cv] * pp))


def _merge_body(x_ref, g_ref, ya_ref, ua_ref, yb_ref, yc_ref, yd_ref, d_ref, wglu_ref,
                wgt_ref, wa_ref, wb_ref, wc_ref, wd_ref, wo_ref, o_ref, ylo_scr, yhi_scr):
    x = x_ref[...]
    h = _rms(x, g_ref[...]).astype(wgt_ref.dtype)
    chunk = ya_ref.shape[1] // SSM_WIDTH
    half = SSM_WIDTH // 2
    for t in range(chunk):
        rows = pl.ds(t, ya_ref.shape[0], stride=chunk)
        ylo_scr[rows, :] = ya_ref[:, t * SSM_WIDTH:t * SSM_WIDTH + half]
        yhi_scr[rows, :] = ya_ref[:, t * SSM_WIDTH + half:(t + 1) * SSM_WIDTH]
    ya_tok = jnp.concatenate([ylo_scr[...], yhi_scr[...]], axis=1)
    z = jax.nn.gelu(ya_tok + d_ref[...] * ua_ref[...])
    ya = z * jax.nn.sigmoid(_mm(z, wglu_ref[...]))
    merged = None
    branches = ((ya, wa_ref), (yb_ref[...], wb_ref), (yc_ref[...], wc_ref), (yd_ref[...], wd_ref))
    for i, (yv, w_ref) in enumerate(branches):
        gate = jax.nn.sigmoid(_mm(h, wgt_ref[:, i * D_MODEL:(i + 1) * D_MODEL]))
        term = gate * _mm(yv, w_ref[...])
        merged = term if merged is None else merged + term
    o_ref[...] = x + _mm(merged, wo_ref[...])


def _merge(x, g, ya, ua, yb, yc, yd, d, wglu, wgt, wa, wb, wc, wd, wo, tm):
    n = x.shape[0]
    row = lambda i: (i, 0)
    fixed = lambda i: (0, 0)
    full = lambda a: pl.BlockSpec(a.shape, fixed, pipeline_mode=pl.Buffered(1))
    return pl.pallas_call(
        _merge_body,
        grid=(n // tm,),
        in_specs=[pl.BlockSpec((tm, D_MODEL), row), full(g),
                  pl.BlockSpec((tm * ya.shape[0] // n, ya.shape[1]), row), pl.BlockSpec((tm, 256), row),
                  pl.BlockSpec((tm, 256), row), pl.BlockSpec((tm, 256), row),
                  pl.BlockSpec((tm, 512), row),
                  full(d), full(wglu), full(wgt), full(wa), full(wb), full(wc), full(wd), full(wo)],
        out_specs=pl.BlockSpec((tm, D_MODEL), row),
        out_shape=jax.ShapeDtypeStruct((n, D_MODEL), F32),
        scratch_shapes=[pltpu.VMEM((tm, SSM_WIDTH // 2), F32), pltpu.VMEM((tm, SSM_WIDTH // 2), F32)],
        compiler_params=_cparams(("parallel",)),
        name="merge",
    )(x, g, ya, ua, yb, yc, yd, d, wglu, wgt, wa, wb, wc, wd, wo)


def _moe_body(x_ref, g_ref, wr_ref, br_ref, wg_ref, wu_ref, wd_ref, gf_ref, o_ref,
              h_scr, gate_scr, acc_scr, *, final_norm):
    gi = pl.program_id(1)
    tm = x_ref.shape[0]
    R = ROUTER_LANES

    @pl.when(gi == 0)
    def _():
        h = _rms(x_ref[...], g_ref[...]).astype(h_scr.dtype)
        h_scr[...] = h
        logits = _mm(h, wr_ref[...]) + br_ref[...]
        lane = lax.broadcasted_iota(jnp.int32, (tm, R), 1)
        lanef = lane.astype(F32)
        neg = -jnp.inf
        is_g = lane < MOE_GROUPS
        glm = jnp.where(is_g, logits, neg)
        gmax = jnp.max(glm, axis=1, keepdims=True)
        gsum = jnp.sum(jnp.where(is_g, jnp.exp(glm - gmax), 0.0), axis=1, keepdims=True)
        g_w = 1.0 / gsum
        g_i = jnp.min(jnp.where(glm == gmax, lanef, float(R)), axis=1, keepdims=True)
        e_grp = ((lane - MOE_GROUPS) // MOE_PER_GROUP).astype(F32)
        sel = (lane >= MOE_GROUPS) & (lane < MOE_GROUPS + MOE_EXPERTS) & (e_grp == g_i)
        elm = jnp.where(sel, logits, neg)
        e1 = jnp.max(elm, axis=1, keepdims=True)
        i1 = jnp.min(jnp.where(elm == e1, lanef, float(R)), axis=1, keepdims=True)
        elm2 = jnp.where(lanef == i1, neg, elm)
        e2 = jnp.max(elm2, axis=1, keepdims=True)
        i2 = jnp.min(jnp.where(elm2 == e2, lanef, float(R)), axis=1, keepdims=True)
        t = jnp.exp(e2 - e1)
        w1 = g_w / (1.0 + t)
        w2 = g_w * t / (1.0 + t)
        gates = jnp.where(lanef == i1, w1, 0.0) + jnp.where(lanef == i2, w2, 0.0)
        for grp in range(MOE_GROUPS):
            gate_scr[grp] = pltpu.roll(gates, R - MOE_GROUPS - MOE_PER_GROUP * grp, 1)
        acc_scr[...] = jnp.zeros(acc_scr.shape, F32)

    h = h_scr[...]
    hid = jax.nn.silu(_mm(h, wg_ref[...])) * _mm(h, wu_ref[...])
    gates = gate_scr[gi]
    hid = jnp.concatenate([hid[:, e * MOE_HIDDEN:(e + 1) * MOE_HIDDEN] * gates[:, e:e + 1]
                           for e in range(MOE_PER_GROUP)], axis=1)
    acc_scr[...] += _mm(hid, wd_ref[...])

    @pl.when(gi == MOE_GROUPS - 1)
    def _():
        o = x_ref[...] + acc_scr[...]
        if final_norm:
            o = _rms(o, gf_ref[...])
        o_ref[...] = o


def _moe(x, g, wr, br, wg, wu, wd, gf, tm, final_norm):
    n = x.shape[0]
    gw = MOE_PER_GROUP * MOE_HIDDEN
    row = lambda i, e: (i, 0)
    fixed = lambda i, e: (0, 0)
    return pl.pallas_call(
        functools.partial(_moe_body, final_norm=final_norm),
        grid=(n // tm, MOE_GROUPS),
        in_specs=[pl.BlockSpec((tm, D_MODEL), row), pl.BlockSpec((1, D_MODEL), fixed),
                  pl.BlockSpec((D_MODEL, ROUTER_LANES), fixed), pl.BlockSpec((1, ROUTER_LANES), fixed),
                  pl.BlockSpec((D_MODEL, gw), lambda i, e: (0, e)),
                  pl.BlockSpec((D_MODEL, gw), lambda i, e: (0, e)),
                  pl.BlockSpec((gw, D_MODEL), lambda i, e: (e, 0)),
                  pl.BlockSpec((1, D_MODEL), fixed)],
        out_specs=pl.BlockSpec((tm, D_MODEL), row),
        out_shape=jax.ShapeDtypeStruct((n, D_MODEL), F32),
        scratch_shapes=[pltpu.VMEM((tm, D_MODEL), wg.dtype), pltpu.VMEM((MOE_GROUPS, tm, ROUTER_LANES), F32),
                        pltpu.VMEM((tm, D_MODEL), F32)],
        compiler_params=_cparams(("parallel", "arbitrary")),
        name="moe",
    )(x, g, wr, br, wg, wu, wd, gf)


def _rope_tables(pos):
    half = DA_HEAD_DIM // 2
    inv = 1.0 / (ROPE_THETA ** (jnp.arange(half, dtype=F32) * 2.0 / DA_HEAD_DIM))
    ang = pos.astype(F32)[:, None] * inv[None, :]
    reps = DA_QK_WIDTH // half
    return jnp.tile(jnp.cos(ang), (1, reps)), jnp.tile(jnp.sin(ang), (1, reps))


def _block_diag(w):
    n, a, b = w.shape
    eye = jnp.eye(n, dtype=w.dtype)
    return jnp.einsum('nij,nm->nimj', w, eye).reshape(n * a, n * b)


def _layer_params(l, p, wdt):
    row = lambda a: a.astype(F32).reshape(1, -1)
    w_in = p['w_in'][l]
    p_lb = jax.nn.softmax(p['hg_lb_logits'].astype(F32), axis=0)
    lb = jnp.cumsum(p_lb, axis=0)[l] - p_lb[0]
    lam_init = 0.8 - 0.6 * math.exp(-0.3 * l)
    lam = (jnp.exp(jnp.sum(p['diff_lq1'][l].astype(F32) * p['diff_lk1'][l].astype(F32)))
           - jnp.exp(jnp.sum(p['diff_lq2'][l].astype(F32) * p['diff_lk2'][l].astype(F32))) + lam_init)
    w_router = jnp.concatenate(
        [p['moe_w_grp'][l], p['moe_w_exp'][l],
         jnp.zeros((D_MODEL, ROUTER_LANES - MOE_GROUPS - MOE_EXPERTS), F32)], axis=1)
    b_router = jnp.concatenate(
        [p['moe_b_grp'][l].astype(F32), p['moe_b_exp'][l].astype(F32),
         jnp.zeros((ROUTER_LANES - MOE_GROUPS - MOE_EXPERTS,), F32)]).reshape(1, -1)
    eh = MOE_EXPERTS * MOE_HIDDEN
    return dict(
        norm_mix=row(p['norm_mix'][l]),
        wdt=wdt,
        w_mix=w_in[:, :MIX_COLS].astype(wdt),
        w_gates=w_in[:, MIX_COLS:].astype(wdt),
        s5=(p['ssm_lambda_re'][l], p['ssm_lambda_im'][l], p['ssm_log_dt'][l], p['ssm_b_re'][l],
            p['ssm_b_im'][l], p['ssm_c_re'][l], p['ssm_c_im'][l]),
        ssm_d=row(p['ssm_d'][l]),
        w_glu=p['ssm_w_glu'][l].astype(wdt),
        hg_lb=lb.reshape(1, -1),
        hg_norm=jnp.tile(p['hg_norm'][l].astype(F32), HG_HEADS).reshape(1, -1),
        conv_w=p['lru_conv_w'][l].astype(F32),
        conv_b=row(p['lru_conv_b'][l]),
        w_ax=jnp.concatenate([_block_diag(p['lru_wa'][l]), _block_diag(p['lru_wx'][l])], axis=1).astype(wdt),
        b_ax=jnp.concatenate([p['lru_ba'][l], p['lru_bx'][l]]).astype(F32).reshape(1, -1),
        neg_c_softplus=(-LRU_C * jax.nn.softplus(-p['lru_lambda'][l].astype(F32))).reshape(1, -1),
        lam=lam.reshape(1).astype(F32),
        out_scale=1.0 - lam_init,
        diff_norm=row(p['diff_norm'][l]),
        w_br_a=p['w_br_a'][l].astype(wdt), w_br_b=p['w_br_b'][l].astype(wdt),
        w_br_c=p['w_br_c'][l].astype(wdt), w_br_d=p['w_br_d'][l].astype(wdt),
        w_out=p['w_out'][l].astype(wdt),
        norm_ffn=row(p['norm_ffn'][l]),
        w_router=w_router.astype(wdt), b_router=b_router,
        moe_gate=p['moe_w_gate'][l].astype(wdt).transpose(1, 0, 2).reshape(D_MODEL, eh),
        moe_up=p['moe_w_up'][l].astype(wdt).transpose(1, 0, 2).reshape(D_MODEL, eh),
        moe_down=p['moe_w_down'][l].astype(wdt).reshape(eh, D_MODEL),
    )


def _pad_rows(a, rows):
    return jnp.pad(a, ((0, 0), (0, rows - a.shape[1])) + ((0, 0),) * (a.ndim - 2))


def _trunk(x, pos0, states, cache, page_table, layers, norm_final, cfg):
    B, T, _ = x.shape
    n = B * T
    tm, s5_chunk, tpad, hg_tb, hg_c, lru_tb, tq = (cfg[k] for k in
                                                   ('tm', 's5_chunk', 'tpad', 'hg_tb', 'hg_c', 'lru_tb', 'tq'))
    cos, sin = _rope_tables(pos0 + jnp.arange(T, dtype=jnp.int32))
    if T % tm:
        cos, sin = jnp.tile(cos, (tm // T, 1)), jnp.tile(sin, (tm // T, 1))
    xf = x.reshape(n, D_MODEL)
    ks, vs, sts = [], [], []
    for l, lp in enumerate(layers):
        st = states[l]
        ua, hg, xg, qb, k, kb, v, vt, ucat = _inproj(xf, lp['norm_mix'], lp['w_mix'], cos, sin, tm, s5_chunk)
        ya, ssm_re, ssm_im = _s5_mixer(ucat, st[0], st[1], _s5_weights(*lp['s5'], s5_chunk, lp['wdt']),
                                       cfg['s5_bt'], T // s5_chunk)
        hg3 = _pad_rows(hg.reshape(B, T, 4 * HG_WIDTH), tpad)
        yb, s_t = _hgrn(hg3, _hgrn_state_to_t(st[2].astype(F32)), lp['hg_lb'], lp['hg_norm'],
                        hg_tb, hg_c, T, lp['wdt'] == F32)
        hg_state = _hgrn_state_from_t(s_t)
        xg3 = xg.reshape(B, T, 2 * LRU_WIDTH)
        conv0 = jnp.pad(st[4].astype(F32), ((0, 0), (8 - (CONV_WIDTH - 1), 0), (0, 0)))
        yc, lru_h = _lru(_pad_rows(xg3, tpad), conv0, st[3].astype(F32).reshape(B, 1, LRU_WIDTH),
                         lp['conv_w'], lp['conv_b'], lp['w_ax'], lp['b_ax'], lp['neg_c_softplus'],
                         lru_tb, T)
        xp = jnp.concatenate([st[4].astype(F32), xg3[:, :, :LRU_WIDTH]], axis=1)
        conv_buf = xp[:, T:]
        if cache is None:
            yd = _attn(lp['lam'], qb.reshape(B, T, -1), kb.reshape(B, T, -1), vt,
                       lp['diff_norm'], tq, cfg['wide_units'], lp['out_scale'])
        else:
            hw = DA_V_DIM
            q4 = qb.reshape(B, T, DA_HEADS, 2, DA_HEAD_DIM)
            zero = jnp.zeros_like(q4[:, :, :, 0])
            q2 = jnp.stack([jnp.concatenate([q4[:, :, :, 0], zero], -1),
                            jnp.concatenate([zero, q4[:, :, :, 1]], -1)], axis=1)
            q2 = q2.transpose(0, 1, 3, 2, 4).reshape(B, 2 * DA_HEADS * T, hw)
            nn = 128
            kn = _pad_rows(k.reshape(B, T * DA_HEADS, hw), nn)
            vn = _pad_rows(v.reshape(B, T * DA_HEADS, hw), nn)
            o = _dec_attn(page_table.reshape(-1), lp['lam'], q2, kn, vn, lp['diff_norm'],
                          cache[0], cache[1], l, cfg['pp'], T, lp['out_scale'])
            yd = o.reshape(B, DA_HEADS, T, hw).transpose(0, 2, 1, 3).reshape(B, T, DA_WIDTH)
        x1 = _merge(xf, lp['norm_mix'], ya, ua, yb[:, :T].reshape(n, -1),
                    yc[:, :T].reshape(n, -1), yd.reshape(n, -1), lp['ssm_d'], lp['w_glu'], lp['w_gates'],
                    lp['w_br_a'], lp['w_br_b'], lp['w_br_c'], lp['w_br_d'], lp['w_out'], tm)
        xf = _moe(x1, lp['norm_ffn'], lp['w_router'], lp['b_router'], lp['moe_gate'], lp['moe_up'],
                  lp['moe_down'], norm_final, tm, l == len(layers) - 1)
        ks.append(k.reshape(B, T, DA_HEADS, 2 * DA_HEAD_DIM))
        vs.append(v.reshape(B, T, DA_HEADS, DA_V_DIM))
        sts.append((ssm_re, ssm_im, hg_state, lru_h.reshape(B, LRU_WIDTH), conv_buf))
    stacked = [jnp.stack([s[j] for s in sts]) for j in range(5)]
    return xf.reshape(B, T, D_MODEL), jnp.stack(ks), jnp.stack(vs), stacked


PROMPT_CFG = dict(tm=512, s5_chunk=16, s5_bt=1, tpad=2048, hg_tb=128, hg_c=16, lru_tb=256, tq=256, wide_units=2)
SAMPLE_CFG = dict(tm=128, s5_chunk=4, s5_bt=32, tpad=16, hg_tb=16, hg_c=16, lru_tb=16, tq=0, pp=8)


def kernel(x_prompt, x_sample, cache_k, cache_v, page_table, state_ssm_re, state_ssm_im, state_hgrn,
           state_lru, state_conv, norm_mix, w_in, ssm_lambda_re, ssm_lambda_im, ssm_log_dt, ssm_b_re,
           ssm_b_im, ssm_c_re, ssm_c_im, ssm_d, ssm_w_glu, hg_lb_logits, hg_norm, lru_conv_w, lru_conv_b,
           lru_wa, lru_ba, lru_wx, lru_bx, lru_lambda, diff_lq1, diff_lk1, diff_lq2, diff_lk2, diff_norm,
           w_br_a, w_br_b, w_br_c, w_br_d, w_out, norm_ffn, moe_w_grp, moe_b_grp, moe_w_exp, moe_b_exp,
           moe_w_gate, moe_w_up, moe_w_down, norm_final):
    p = dict(norm_mix=norm_mix, w_in=w_in, ssm_lambda_re=ssm_lambda_re, ssm_lambda_im=ssm_lambda_im,
             ssm_log_dt=ssm_log_dt, ssm_b_re=ssm_b_re, ssm_b_im=ssm_b_im, ssm_c_re=ssm_c_re,
             ssm_c_im=ssm_c_im, ssm_d=ssm_d, ssm_w_glu=ssm_w_glu, hg_lb_logits=hg_lb_logits,
             hg_norm=hg_norm, lru_conv_w=lru_conv_w, lru_conv_b=lru_conv_b, lru_wa=lru_wa, lru_ba=lru_ba,
             lru_wx=lru_wx, lru_bx=lru_bx, lru_lambda=lru_lambda, diff_lq1=diff_lq1, diff_lk1=diff_lk1,
             diff_lq2=diff_lq2, diff_lk2=diff_lk2, diff_norm=diff_norm, w_br_a=w_br_a, w_br_b=w_br_b,
             w_br_c=w_br_c, w_br_d=w_br_d, w_out=w_out, norm_ffn=norm_ffn, moe_w_grp=moe_w_grp,
             moe_b_grp=moe_b_grp, moe_w_exp=moe_w_exp, moe_b_exp=moe_b_exp, moe_w_gate=moe_w_gate,
             moe_w_up=moe_w_up, moe_w_down=moe_w_down)
    layers = [_layer_params(l, p, BF16) for l in range(DEPTH)]
    layers_f32 = [_layer_params(l, p, F32) for l in range(DEPTH)]
    gf = norm_final.astype(F32).reshape(1, -1)
    Bp = x_prompt.shape[0]
    Bs = x_sample.shape[0]
    zero_states = [(jnp.zeros((Bp, SSM_GROUPS, SSM_STATE), F32), jnp.zeros((Bp, SSM_GROUPS, SSM_STATE), F32),
                    jnp.zeros((Bp, HG_HEADS, HG_HEAD_DIM, HG_HEAD_DIM), F32), jnp.zeros((Bp, LRU_WIDTH), F32),
                    jnp.zeros((Bp, CONV_WIDTH - 1, LRU_WIDTH), F32)) for _ in range(DEPTH)]
    y_p, k_p, v_p, st_p = _trunk(x_prompt, 0, zero_states, None, None, layers, gf, PROMPT_CFG)
    past_len = page_table.shape[1] * PAGE_SIZE
    sample_states = [(state_ssm_re[l], state_ssm_im[l], state_hgrn[l], state_lru[l], state_conv[l])
                     for l in range(DEPTH)]
    n_pool = cache_k.shape[1]
    rows = PAGE_SIZE * DA_HEADS
    cache = (cache_k.reshape(DEPTH, n_pool, rows, 2 * DA_HEAD_DIM), cache_v.reshape(DEPTH, n_pool, rows, DA_V_DIM))
    y_s, k_s, v_s, st_s = _trunk(x_sample, past_len, sample_states, cache, page_table, layers_f32, gf,
                                 SAMPLE_CFG)
    return (y_p, y_s, k_p, v_p, k_s, v_s,
            st_p[0], st_p[1], st_s[0], st_s[1], st_p[2], st_s[2], st_p[3], st_s[3], st_p[4], st_s[4])
```

```python
import functools
import math

import jax
import jax.numpy as jnp
from jax import lax
from jax.experimental import pallas as pl
from jax.experimental.pallas import tpu as pltpu

F32 = jnp.float32
BF16 = jnp.bfloat16

D_MODEL = 1024
DEPTH = 2
PAGE_SIZE = 128
SSM_WIDTH = 256
SSM_GROUP = 16
SSM_GROUPS = 16
SSM_STATE = 64
HG_WIDTH = 256
HG_HEAD_DIM = 64
HG_HEADS = 4
LRU_WIDTH = 256
LRU_BLOCKS = 4
LRU_BLOCK = 64
CONV_WIDTH = 4
LRU_C = 8.0
DA_HEADS = 4
DA_HEAD_DIM = 64
DA_V_DIM = 128
DA_QK_WIDTH = 512
DA_WIDTH = 512
ROPE_THETA = 10000.0
MASK_VALUE = -1e30
N_BRANCH = 4
MOE_GROUPS = 4
MOE_PER_GROUP = 8
MOE_EXPERTS = 32
MOE_HIDDEN = 128
NORM_EPS = 1e-6
MIX_COLS = 3328
ROUTER_LANES = 128
VMEM_LIMIT = 56 * 1024 * 1024
HI = lax.Precision.HIGHEST
Q_SCALE = DA_HEAD_DIM ** -0.5 * math.log2(math.e)
KV_UNIT = 256


def _cparams(sem):
    return pltpu.CompilerParams(dimension_semantics=sem, vmem_limit_bytes=VMEM_LIMIT)


def _rms(x, g):
    return x * lax.rsqrt(jnp.mean(x * x, axis=-1, keepdims=True) + NORM_EPS) * g


def _dot(a, b):
    return jnp.dot(a, b, preferred_element_type=F32)


def _mm(a, w, dims=(((1,), (0,)), ((), ()))):
    if w.dtype == BF16:
        return lax.dot_general(a.astype(BF16), w, dims, preferred_element_type=F32)
    return lax.dot_general(a.astype(F32), w, dims, preferred_element_type=F32, precision=HI)


NT_DIMS = (((1,), (1,)), ((), ()))
TN_DIMS = (((0,), (0,)), ((), ()))


def _dot_nt(a, b):
    return lax.dot_general(a, b, (((1,), (1,)), ((), ())), preferred_element_type=F32)


def _dot_tn(a, b):
    return lax.dot_general(a, b, (((0,), (0,)), ((), ())), preferred_element_type=F32)


def _split_dot(a, b_bf16, terms):
    out = None
    rem = a
    for _ in range(terms):
        piece = rem.astype(BF16)
        part = _dot(piece, b_bf16)
        out = part if out is None else out + part
        rem = rem - piece.astype(F32)
    return out


def _head_ones(width, head):
    r = lax.broadcasted_iota(jnp.int32, (width, width), 0) // head
    c = lax.broadcasted_iota(jnp.int32, (width, width), 1) // head
    return r == c


def _inproj_body(x_ref, g_ref, w_ref, cos_ref, sin_ref,
                 ua_ref, hg_ref, lru_ref, q_ref, k_ref, kb_ref, v_ref, vt_ref, uc_ref, ulo_scr, uhi_scr):
    h = _rms(x_ref[...], g_ref[...]).astype(w_ref.dtype)

    def mm(a, b):
        return _mm(h, w_ref[:, a:b])

    ua = mm(0, 256)
    ua_ref[...] = ua
    chunk = uc_ref.shape[1] // SSM_WIDTH
    half = SSM_WIDTH // 2
    ulo_scr[...] = ua[:, 0:half]
    uhi_scr[...] = ua[:, half:SSM_WIDTH]
    for t in range(chunk):
        rows = pl.ds(t, uc_ref.shape[0], stride=chunk)
        uc_ref[:, t * SSM_WIDTH:t * SSM_WIDTH + half] = ulo_scr[rows, :].astype(uc_ref.dtype)
        uc_ref[:, t * SSM_WIDTH + half:(t + 1) * SSM_WIDTH] = uhi_scr[rows, :].astype(uc_ref.dtype)
    hg_ref[...] = mm(256, 1280)
    lru_ref[...] = mm(1280, 1792)
    cos = cos_ref[...]
    sin = sin_ref[...]
    lane = lax.broadcasted_iota(jnp.int32, cos.shape, 1)
    first = (lane % DA_HEAD_DIM) < (DA_HEAD_DIM // 2)

    def rope(z):
        rot = jnp.where(first, -pltpu.roll(z, DA_QK_WIDTH - DA_HEAD_DIM // 2, 1),
                        pltpu.roll(z, DA_HEAD_DIM // 2, 1))
        return z * cos + rot * sin

    q = rope(mm(1792, 2304))
    q_ref[...] = (q * Q_SCALE).astype(q_ref.dtype)
    k = rope(mm(2304, 2816))
    kb_ref[...] = k.astype(BF16)
    v = mm(2816, 3328)
    tm = k.shape[0]
    unit = vt_ref.shape[2]
    for u in range(tm // unit):
        vt_ref[u] = v[u * unit:(u + 1) * unit, :].T.astype(BF16)
    for hd in range(DA_HEADS):
        cs = slice(hd * DA_V_DIM, (hd + 1) * DA_V_DIM)
        k_ref[pl.ds(hd, tm, stride=DA_HEADS), :] = k[:, cs]
        v_ref[pl.ds(hd, tm, stride=DA_HEADS), :] = v[:, cs]


def _inproj(x, g, w, cos, sin, tm, chunk):
    n = x.shape[0]
    ntab = cos.shape[0] // tm
    row = lambda i: (i, 0)
    fixed = lambda i: (0, 0)
    tab = lambda i: (i % ntab, 0)
    outs = ((1, 256, F32), (1, 1024, F32), (1, 512, F32), (1, 512, w.dtype),
            (DA_HEADS, DA_V_DIM, F32), (1, 512, BF16), (DA_HEADS, DA_V_DIM, F32))
    unit = min(tm, KV_UNIT)
    return pl.pallas_call(
        _inproj_body,
        grid=(n // tm,),
        in_specs=[pl.BlockSpec((tm, D_MODEL), row), pl.BlockSpec((1, D_MODEL), fixed),
                  pl.BlockSpec((D_MODEL, MIX_COLS), fixed),
                  pl.BlockSpec((tm, 512), tab), pl.BlockSpec((tm, 512), tab)],
        out_specs=[pl.BlockSpec((tm * r, wd), row) for r, wd, _ in outs]
                  + [pl.BlockSpec((tm // unit, DA_WIDTH, unit), lambda i: (i, 0, 0)),
                     pl.BlockSpec((tm // chunk, chunk * SSM_WIDTH), row)],
        out_shape=[jax.ShapeDtypeStruct((n * r, wd), dt) for r, wd, dt in outs]
                  + [jax.ShapeDtypeStruct((n // unit, DA_WIDTH, unit), BF16),
                     jax.ShapeDtypeStruct((n // chunk, chunk * SSM_WIDTH), w.dtype)],
        scratch_shapes=[pltpu.VMEM((tm, SSM_WIDTH // 2), F32), pltpu.VMEM((tm, SSM_WIDTH // 2), F32)],
        compiler_params=_cparams(("parallel",)),
        name="inproj",
    )(x, g, w, cos, sin)


S5_STATE_LANES = SSM_GROUPS * SSM_STATE
S5_HALF = S5_STATE_LANES // 2


def _s5_body(u_ref, k_ref, p_ref, q_ref, a_ref, h0_ref, y_ref, hf_ref, pu_scr, hs_scr, *, L, bt, nc):
    W = SSM_WIDTH
    HW = W // 2
    SL, SH = S5_STATE_LANES, S5_HALF
    for half in range(2):
        acc = None
        for t in range(L):
            c0 = t * W + half * HW
            part = _mm(u_ref[:, c0:c0 + HW], p_ref[t, half])
            acc = part if acc is None else acc + part
        pu_scr[:, half * SH:(half + 1) * SH] = acc[:, 0:SH]
        pu_scr[:, SL + half * SH:SL + (half + 1) * SH] = acc[:, SH:2 * SH]
    ar2 = a_ref[0:1, :]
    ai2 = a_ref[1:2, :]

    assert bt == 1 or nc == 1

    def step(c, h):
        rows = pl.ds(c * bt, bt)
        hs_scr[rows, :] = h
        return ar2 * h + ai2 * pltpu.roll(h, SL, 1) + pu_scr[rows, :]

    hf_ref[0] = lax.fori_loop(0, nc, step, h0_ref[0])
    hs = [jnp.concatenate([hs_scr[:, half * SH:(half + 1) * SH],
                           hs_scr[:, SL + half * SH:SL + (half + 1) * SH]], axis=1).astype(k_ref.dtype)
          for half in range(2)]
    for t2 in range(L):
        acc = jnp.concatenate([_mm(hs[0], q_ref[t2, 0]), _mm(hs[1], q_ref[t2, 1])], axis=1)
        for t in range(t2 + 1):
            acc = acc + _mm(u_ref[:, t * W:(t + 1) * W], k_ref[t2 - t])
        y_ref[:, t2 * W:(t2 + 1) * W] = acc


def _s5(ucat, kbd, pmat, qmat, a2, h0, bt, nc):
    rows_all, lw = ucat.shape
    L = lw // SSM_WIDTH
    rows = bt * nc
    once = lambda a: pl.BlockSpec(a.shape, lambda i: (0,) * a.ndim, pipeline_mode=pl.Buffered(1))
    return pl.pallas_call(
        functools.partial(_s5_body, L=L, bt=bt, nc=nc),
        grid=(rows_all // rows,),
        in_specs=[pl.BlockSpec((rows, lw), lambda i: (i, 0)), once(kbd), once(pmat), once(qmat), once(a2),
                  pl.BlockSpec((1, bt, 2 * S5_STATE_LANES), lambda i: (i, 0, 0))],
        out_specs=[pl.BlockSpec((rows, lw), lambda i: (i, 0)),
                   pl.BlockSpec((1, bt, 2 * S5_STATE_LANES), lambda i: (i, 0, 0))],
        out_shape=[jax.ShapeDtypeStruct((rows_all, lw), F32), jax.ShapeDtypeStruct(h0.shape, F32)],
        scratch_shapes=[pltpu.VMEM((rows, 2 * S5_STATE_LANES), F32), pltpu.VMEM((rows, 2 * S5_STATE_LANES), F32)],
        compiler_params=_cparams(("parallel",)),
        name="s5",
    )(ucat, kbd, pmat, qmat, a2, h0)


def _s5_weights(lam_re, lam_im, log_dt, b_re, b_im, c_re, c_im, L, wdtype):
    G, P, J = SSM_GROUPS, SSM_STATE, SSM_GROUP
    lr, li = lam_re.astype(F32), lam_im.astype(F32)
    dt = jnp.exp(log_dt.astype(F32))[:, None]
    mag = jnp.exp(lr * dt)
    ar = mag * jnp.cos(li * dt)
    ai = mag * jnp.sin(li * dt)
    den = lr * lr + li * li
    fr = ((ar - 1.0) * lr + ai * li) / den
    fi = (ai * lr - (ar - 1.0) * li) / den
    br, bi = b_re.astype(F32), b_im.astype(F32)
    bbr = fr[..., None] * br - fi[..., None] * bi
    bbi = fr[..., None] * bi + fi[..., None] * br
    tau = jnp.arange(L + 1, dtype=F32)[:, None, None]
    pmag = jnp.exp(lr * dt * tau)
    pr = pmag * jnp.cos(li * dt * tau)
    pi = pmag * jnp.sin(li * dt * tau)
    t1r = pr[..., None] * bbr - pi[..., None] * bbi
    t1i = pr[..., None] * bbi + pi[..., None] * bbr
    cr, ci = c_re.astype(F32), c_im.astype(F32)
    kt = (jnp.einsum('gip,tgpj->tgij', cr, t1r, precision=HI)
          - jnp.einsum('gip,tgpj->tgij', ci, t1i, precision=HI))
    GH = G // 2
    eye, eye_h = jnp.eye(G, dtype=F32), jnp.eye(GH, dtype=F32)
    kbd = jnp.einsum('tgij,gh->tgjhi', kt[:L], eye).reshape(L, G * J, G * J)
    rev = L - 1 - jnp.arange(L)
    ph = jnp.stack([t1r[rev], t1i[rev]], axis=1).reshape(L, 2, 2, GH, P, J)
    pmat = jnp.einsum('trhgpj,gk->thgjrkp', ph, eye_h).reshape(L, 2, GH * J, 2 * GH * P)
    car = cr[None] * pr[1:, :, None, :] - ci[None] * pi[1:, :, None, :]
    cai = cr[None] * pi[1:, :, None, :] + ci[None] * pr[1:, :, None, :]
    qh = jnp.stack([car, -cai], axis=1).reshape(L, 2, 2, GH, J, P)
    qmat = jnp.einsum('trhgip,gk->thrgpki', qh, eye_h).reshape(L, 2, 2 * GH * P, GH * J)
    a_l = jnp.stack([jnp.concatenate([pr[L].reshape(-1), pr[L].reshape(-1)]),
                     jnp.concatenate([-pi[L].reshape(-1), pi[L].reshape(-1)])])
    return kbd.astype(wdtype), pmat.astype(wdtype), qmat.astype(wdtype), a_l


def _s5_mixer(ucat, h0_re, h0_im, wts, bt, nc):
    B = h0_re.shape[0]
    h0 = jnp.concatenate([h0_re.reshape(B, -1), h0_im.reshape(B, -1)], axis=-1).astype(F32)
    y, hf = _s5(ucat, *wts, h0.reshape(B // bt, bt, -1), bt, nc)
    hf = hf.reshape(B, 2, SSM_GROUPS, SSM_STATE)
    return y, hf[:, 0], hf[:, 1]


def _hgrn_body(hg_ref, s0_ref, lb_ref, ng_ref, y_ref, sf_ref,
               st_scr, k_scr, b_scr, v_scr, w_scr, *, tb, c, t_valid, precise):
    j = pl.program_id(1)
    W = HG_WIDTH
    mdt = F32 if precise else BF16

    @pl.when(j == 0)
    def _():
        st_scr[...] = s0_ref[0]
        k_scr[0:c, :] = jnp.zeros((c, W), F32)
        b_scr[0:c, :] = jnp.zeros((c, W), F32)
        v_scr[0:c, :] = jnp.zeros((c, W), F32)

    q = hg_ref[0, :, 0:W]
    lb = lb_ref[...]
    fv = lb + (1.0 - lb) * jax.nn.sigmoid(hg_ref[0, :, W:2 * W])
    logf = jnp.log(fv) * math.log2(math.e)
    kk = 1.0 - fv
    v = hg_ref[0, :, 2 * W:3 * W]
    row = lax.broadcasted_iota(jnp.int32, (tb, W), 0)
    if t_valid < tb:
        valid = row < t_valid
        logf = jnp.where(valid, logf, 0.0)
        kk = jnp.where(valid, kk, 0.0)
    ri = lax.broadcasted_iota(jnp.int32, (tb, tb), 0)
    ci = lax.broadcasted_iota(jnp.int32, (tb, tb), 1)
    tril = ((ri // c == ci // c) & (ci <= ri)).astype(BF16)
    b = _split_dot_lhs_exact(tril, logf)
    k_scr[c:c + tb, :] = kk
    b_scr[c:c + tb, :] = b
    v_scr[c:c + tb, :] = v
    rin = row % c
    for d in range(c):
        ksh = k_scr[c - d:c - d + tb, :]
        bsh = b_scr[c - d:c - d + tb, :]
        w = jnp.where(rin >= d, q * ksh * jnp.exp2(b - bsh), 0.0)
        w_scr[d * tb:(d + 1) * tb, :] = w.astype(mdt)
    ones_bd = _head_ones(W, HG_HEAD_DIM).astype(BF16)
    att = _mm(w_scr[...], ones_bd.astype(mdt))
    o = att[0:tb] * v
    for d in range(1, c):
        o = o + att[d * tb:(d + 1) * tb] * v_scr[c - d:c - d + tb, :]
    bd = _head_ones(W, HG_HEAD_DIM)
    outs = []
    for ch in range(tb // c):
        sl = slice(ch * c, (ch + 1) * c)
        bc = b[sl]
        bl = bc[c - 1:c, :]
        st = st_scr[...]
        outs.append(_mm(q[sl] * jnp.exp2(bc), st.astype(mdt), NT_DIMS))
        khat = (kk[sl] * jnp.exp2(bl - bc)).astype(mdt)
        upd = _mm(v[sl], khat, TN_DIMS)
        st_scr[...] = st * jnp.exp2(bl) + jnp.where(bd, upd, 0.0)
    o = o + jnp.concatenate(outs, axis=0) if len(outs) > 1 else o + outs[0]
    ms = _split_dot(o * o, ones_bd, 3 if precise else 2) * (1.0 / HG_HEAD_DIM)
    y = o * lax.rsqrt(ms + NORM_EPS) * ng_ref[...]
    y_ref[0] = y * jax.nn.silu(hg_ref[0, :, 3 * W:4 * W])
    sf_ref[0] = st_scr[...]


def _split_dot_lhs_exact(a_bf16, b):
    out = None
    rem = b
    for _ in range(3):
        piece = rem.astype(BF16)
        part = _dot(a_bf16, piece)
        out = part if out is None else out + part
        rem = rem - piece.astype(F32)
    return out


def _hgrn(hg, s0t, lb, ng, tb, c, t_valid, precise):
    B, T, _ = hg.shape
    W = HG_WIDTH
    return pl.pallas_call(
        functools.partial(_hgrn_body, tb=tb, c=c, t_valid=t_valid, precise=precise),
        grid=(B, T // tb),
        in_specs=[pl.BlockSpec((1, tb, 4 * W), lambda b, j: (b, j, 0)),
                  pl.BlockSpec((1, W, W), lambda b, j: (b, 0, 0)),
                  pl.BlockSpec((1, W), lambda b, j: (0, 0)),
                  pl.BlockSpec((1, W), lambda b, j: (0, 0))],
        out_specs=[pl.BlockSpec((1, tb, W), lambda b, j: (b, j, 0)),
                   pl.BlockSpec((1, W, W), lambda b, j: (b, 0, 0))],
        out_shape=[jax.ShapeDtypeStruct((B, T, W), F32), jax.ShapeDtypeStruct((B, W, W), F32)],
        scratch_shapes=[pltpu.VMEM((W, W), F32), pltpu.VMEM((c + tb, W), F32),
                        pltpu.VMEM((c + tb, W), F32), pltpu.VMEM((c + tb, W), F32),
                        pltpu.VMEM((c * tb, W), F32 if precise else BF16)],
        compiler_params=_cparams(("parallel", "arbitrary")),
        name="hgrn2",
    )(hg, s0t, lb, ng)


def _hgrn_state_to_t(s):
    B = s.shape[0]
    eye = jnp.eye(HG_HEADS, dtype=s.dtype)
    return jnp.einsum('bhde,hg->bhegd', s, eye).reshape(B, HG_WIDTH, HG_WIDTH)


def _hgrn_state_from_t(st):
    B = st.shape[0]
    s5 = st.reshape(B, HG_HEADS, HG_HEAD_DIM, HG_HEADS, HG_HEAD_DIM)
    idx = jnp.arange(HG_HEADS)
    return s5[:, idx, :, idx, :].transpose(1, 0, 3, 2)


def _lru_body(x_ref, c0_ref, h0_ref, cw_ref, cb_ref, wax_ref, bax_ref, nsp_ref, y_ref, hl_ref,
              xs_scr, hc_scr, *, tb, r_last):
    j = pl.program_id(1)
    W = LRU_WIDTH

    @pl.when(j == 0)
    def _():
        xs_scr[0:8, :] = c0_ref[0]
        hc_scr[...] = h0_ref[0]

    x = x_ref[0, :, 0:W]
    xs_scr[8:8 + tb, :] = x
    xc = cb_ref[...] + x * cw_ref[3:4, :]
    for jj in range(CONV_WIDTH - 1):
        xc = xc + xs_scr[5 + jj:5 + jj + tb, :] * cw_ref[jj:jj + 1, :]
    tail = xs_scr[tb:tb + 8, :]
    xs_scr[0:8, :] = tail
    rg = _mm(xc, wax_ref[...]) + bax_ref[...]
    r = jax.nn.sigmoid(rg[:, 0:W])
    ig = jax.nn.sigmoid(rg[:, W:2 * W])
    log_a = nsp_ref[...] * r
    a = jnp.exp(log_a)
    bt = jnp.sqrt(jnp.maximum(1.0 - a * a, 0.0)) * (ig * xc)
    row = lax.broadcasted_iota(jnp.int32, (tb, W), 0)
    k = 1
    while k < tb:
        keep = row >= k
        a_sh = jnp.where(keep, pltpu.roll(a, k, 0), 1.0)
        b_sh = jnp.where(keep, pltpu.roll(bt, k, 0), 0.0)
        bt = a * b_sh + bt
        a = a * a_sh
        k *= 2
    h = a * hc_scr[...] + bt
    y_ref[0] = jax.nn.gelu(x_ref[0, :, W:2 * W]) * h
    hc = h[r_last:r_last + 1, :]
    hc_scr[...] = hc
    hl_ref[0] = hc


def _lru(xg, c0, h0, cw, cb, wax, bax, nsp, tb, t_valid):
    B, T, _ = xg.shape
    W = LRU_WIDTH
    fixed = lambda b, j: (0, 0)
    return pl.pallas_call(
        functools.partial(_lru_body, tb=tb, r_last=(t_valid - 1) % tb),
        grid=(B, T // tb),
        in_specs=[pl.BlockSpec((1, tb, 2 * W), lambda b, j: (b, j, 0)),
                  pl.BlockSpec((1, 8, W), lambda b, j: (b, 0, 0)),
                  pl.BlockSpec((1, 1, W), lambda b, j: (b, 0, 0)),
                  pl.BlockSpec((CONV_WIDTH, W), fixed), pl.BlockSpec((1, W), fixed),
                  pl.BlockSpec((W, 2 * W), fixed), pl.BlockSpec((1, 2 * W), fixed),
                  pl.BlockSpec((1, W), fixed)],
        out_specs=[pl.BlockSpec((1, tb, W), lambda b, j: (b, j, 0)),
                   pl.BlockSpec((1, 1, W), lambda b, j: (b, 0, 0))],
        out_shape=[jax.ShapeDtypeStruct((B, T, W), F32), jax.ShapeDtypeStruct((B, 1, W), F32)],
        scratch_shapes=[pltpu.VMEM((8 + tb, W), F32), pltpu.VMEM((1, W), F32)],
        compiler_params=_cparams(("parallel", "arbitrary")),
        name="rglru",
    )(xg, c0, h0, cw, cb, wax, bax, nsp)


def _attn_body(lam_ref, q_ref, k_ref, vt_ref, g_ref, o_ref, q2_scr, m_scr, l_scr, acc_scr,
               *, tq, wide_units, out_scale):
    qi = pl.program_id(1)
    hw = DA_V_DIM
    lane = lax.broadcasted_iota(jnp.int32, (tq, hw), 1)
    for h in range(DA_HEADS):
        qh = q_ref[0, :, h * hw:(h + 1) * hw]
        zero = jnp.zeros_like(qh)
        q2_scr[h, 0:tq, :] = jnp.where(lane < DA_HEAD_DIM, qh, zero)
        q2_scr[h, tq:2 * tq, :] = jnp.where(lane >= DA_HEAD_DIM, qh, zero)
    m_scr[...] = jnp.full(m_scr.shape, MASK_VALUE, F32)
    l_scr[...] = jnp.zeros(l_scr.shape, F32)
    acc_scr[...] = jnp.zeros(acc_scr.shape, F32)

    def block(u0, nu, diagonal):
        tk = nu * KV_UNIT
        r0 = pl.multiple_of(u0 * KV_UNIT, KV_UNIT)
        for h in range(DA_HEADS):
            cs = slice(h * hw, (h + 1) * hw)
            st = _dot_nt(k_ref[0, pl.ds(r0, tk), cs], q2_scr[h])
            if diagonal:
                keyi = lax.broadcasted_iota(jnp.int32, (tk, 2 * tq), 0)
                qryi = lax.broadcasted_iota(jnp.int32, (tk, 2 * tq), 1) % tq
                st = jnp.where(keyi <= qryi, st, MASK_VALUE)
            m = m_scr[h]
            m_new = jnp.maximum(m, jnp.max(st, axis=0, keepdims=True))
            alpha = jnp.exp2(m - m_new)
            p = jnp.exp2(st - m_new)
            l_scr[h] = alpha * l_scr[h] + jnp.sum(p, axis=0, keepdims=True)
            pb = p.astype(BF16)
            pv = _dot(vt_ref[u0, cs, :], pb[0:KV_UNIT])
            for u in range(1, nu):
                pv = pv + _dot(vt_ref[u0 + u, cs, :], pb[u * KV_UNIT:(u + 1) * KV_UNIT])
            acc_scr[h] = alpha * acc_scr[h] + pv
            m_scr[h] = m_new

    nq = tq // KV_UNIT
    n_before = qi * nq
    done = 0
    width = wide_units
    while width >= 1:
        n_blocks = (n_before - done) // width

        def step(jb, carry, width=width, done=done):
            block(done + jb * width, width, False)
            return carry

        lax.fori_loop(0, n_blocks, step, 0)
        done = done + n_blocks * width
        width //= 2
    block(n_before, nq, True)
    lam = lam_ref[0]
    for h in range(DA_HEADS):
        on = acc_scr[h] / l_scr[h]
        o = (on[:, 0:tq] - lam * on[:, tq:2 * tq]).T
        o_ref[0, :, h * hw:(h + 1) * hw] = _rms(o, g_ref[...]) * out_scale


def _attn(lam, q, k, vt, g, tq, wide_units, out_scale):
    B, T, Wd = q.shape
    hw = DA_V_DIM
    units = T // KV_UNIT
    return pl.pallas_call(
        functools.partial(_attn_body, tq=tq, wide_units=wide_units, out_scale=out_scale),
        grid=(B, T // tq),
        in_specs=[pl.BlockSpec(memory_space=pltpu.SMEM),
                  pl.BlockSpec((1, tq, Wd), lambda b, i: (b, i, 0)),
                  pl.BlockSpec((1, T, Wd), lambda b, i: (b, 0, 0)),
                  pl.BlockSpec((units, Wd, KV_UNIT), lambda b, i: (b, 0, 0)),
                  pl.BlockSpec((1, hw), lambda b, i: (0, 0))],
        out_specs=pl.BlockSpec((1, tq, Wd), lambda b, i: (b, i, 0)),
        out_shape=jax.ShapeDtypeStruct((B, T, Wd), F32),
        scratch_shapes=[pltpu.VMEM((DA_HEADS, 2 * tq, hw), BF16), pltpu.VMEM((DA_HEADS, 1, 2 * tq), F32),
                        pltpu.VMEM((DA_HEADS, 1, 2 * tq), F32), pltpu.VMEM((DA_HEADS, hw, 2 * tq), F32)],
        compiler_params=_cparams(("parallel", "arbitrary")),
        name="diff_attn",
    )(lam, q, k, vt, g)


def _dec_body(pt_ref, lam_ref, q_ref, kn_ref, vn_ref, g_ref, *rest, pp, t_new, out_scale):
    k_refs = rest[0:pp]
    v_refs = rest[pp:2 * pp]
    o_ref = rest[2 * pp]
    m_scr, l_scr, acc_scr, bias_scr = rest[2 * pp + 1:]
    j = pl.program_id(1)
    nrow = 2 * DA_HEADS * t_new

    @pl.when(j == 0)
    def _():
        m_scr[...] = jnp.full(m_scr.shape, MASK_VALUE, F32)
        l_scr[...] = jnp.zeros(l_scr.shape, F32)
        acc_scr[...] = jnp.zeros(acc_scr.shape, F32)
        ncol = PAGE_SIZE * DA_HEADS
        rh = (lax.broadcasted_iota(jnp.int32, (nrow, ncol), 0) // t_new) % DA_HEADS
        chd = lax.broadcasted_iota(jnp.int32, (nrow, ncol), 1) % DA_HEADS
        bias_scr[...] = jnp.where(rh == chd, 0.0, MASK_VALUE)

    def hi_lo(a):
        hi = a.astype(BF16)
        return jnp.concatenate([hi, (a - hi.astype(F32)).astype(BF16)], axis=0)

    def fold(a):
        return a[0:nrow] + a[nrow:2 * nrow]

    q = hi_lo(q_ref[0])

    def scores(keys):
        return fold(_dot_nt(q, keys.astype(BF16)))

    def update(ss, vals):
        m = m_scr[...]
        smax = ss[0]
        for s in ss[1:]:
            smax = jnp.maximum(smax, s)
        m_new = jnp.maximum(m, jnp.max(smax, axis=1, keepdims=True))
        alpha = jnp.exp2(m - m_new)
        ps = [jnp.exp2(s - m_new) for s in ss]
        psum = ps[0]
        for p in ps[1:]:
            psum = psum + p
        pv = _dot(hi_lo(ps[0]), vals[0])
        for p, vv in zip(ps[1:], vals[1:]):
            pv = pv + _dot(hi_lo(p), vv)
        l_scr[...] = alpha * l_scr[...] + jnp.sum(psum, axis=1, keepdims=True)
        acc_scr[...] = alpha * acc_scr[...] + fold(pv)
        m_scr[...] = m_new

    bias = bias_scr[...]
    update([scores(k_refs[i][...]) + bias for i in range(pp)],
           [v_refs[i][...].astype(BF16) for i in range(pp)])

    @pl.when(j == pl.num_programs(1) - 1)
    def _():
        nn = kn_ref.shape[1]
        r = lax.broadcasted_iota(jnp.int32, (nrow, nn), 0)
        cidx = lax.broadcasted_iota(jnp.int32, (nrow, nn), 1)
        ok = ((r // t_new) % DA_HEADS == cidx % DA_HEADS) & (cidx // DA_HEADS <= r % t_new)
        update([jnp.where(ok, scores(kn_ref[0]), MASK_VALUE)], [vn_ref[0].astype(BF16)])
        on = acc_scr[...] / l_scr[...]
        half = nrow // 2
        o = on[0:half] - lam_ref[0] * on[half:nrow]
        o_ref[0] = _rms(o, g_ref[...]) * out_scale


def _dec_attn(pt, lam, q2, kn, vn, g, ck, cv, layer, pp, t_new, out_scale):
    B, nrow, hw = q2.shape
    n_pages = pt.shape[0] // B
    nn = kn.shape[1]
    rows = PAGE_SIZE * DA_HEADS

    def page_spec(i):
        return pl.BlockSpec((None, None, rows, hw),
                            lambda b, j, pt_ref: (layer, pt_ref[b * n_pages + j * pp + i], 0, 0))

    grid_spec = pltpu.PrefetchScalarGridSpec(
        num_scalar_prefetch=1,
        grid=(B, n_pages // pp),
        in_specs=[pl.BlockSpec(memory_space=pltpu.SMEM),
                  pl.BlockSpec((1, nrow, hw), lambda b, j, pt_ref: (b, 0, 0)),
                  pl.BlockSpec((1, nn, hw), lambda b, j, pt_ref: (b, 0, 0)),
                  pl.BlockSpec((1, nn, hw), lambda b, j, pt_ref: (b, 0, 0)),
                  pl.BlockSpec((1, hw), lambda b, j, pt_ref: (0, 0))]
                 + [page_spec(i) for i in range(pp)] + [page_spec(i) for i in range(pp)],
        out_specs=pl.BlockSpec((1, nrow // 2, hw), lambda b, j, pt_ref: (b, 0, 0)),
        scratch_shapes=[pltpu.VMEM((nrow, 1), F32), pltpu.VMEM((nrow, 1), F32),
                        pltpu.VMEM((nrow, hw), F32), pltpu.VMEM((nrow, rows), F32)],
    )
    return pl.pallas_call(
        functools.partial(_dec_body, pp=pp, t_new=t_new, out_scale=out_scale),
        grid_spec=grid_spec,
        out_shape=jax.ShapeDtypeStruct((B, nrow // 2, hw), F32),
        compiler_params=_cparams(("parallel", "arbitrary")),
        name="paged_diff_attn",
    )(pt, lam, q2, kn, vn, g, *([ck] * pp), *([cv] * pp))


def _merge_body(x_ref, g_ref, ya_ref, ua_ref, yb_ref, yc_ref, yd_ref, d_ref, wglu_ref,
                wgt_ref, wa_ref, wb_ref, wc_ref, wd_ref, wo_ref, o_ref, ylo_scr, yhi_scr):
    x = x_ref[...]
    h = _rms(x, g_ref[...]).astype(wgt_ref.dtype)
    chunk = ya_ref.shape[1] // SSM_WIDTH
    half = SSM_WIDTH // 2
    for t in range(chunk):
        rows = pl.ds(t, ya_ref.shape[0], stride=chunk)
        ylo_scr[rows, :] = ya_ref[:, t * SSM_WIDTH:t * SSM_WIDTH + half]
        yhi_scr[rows, :] = ya_ref[:, t * SSM_WIDTH + half:(t + 1) * SSM_WIDTH]
    ya_tok = jnp.concatenate([ylo_scr[...], yhi_scr[...]], axis=1)
    z = jax.nn.gelu(ya_tok + d_ref[...] * ua_ref[...])
    ya = z * jax.nn.sigmoid(_mm(z, wglu_ref[...]))
    merged = None
    branches = ((ya, wa_ref), (yb_ref[...], wb_ref), (yc_ref[...], wc_ref), (yd_ref[...], wd_ref))
    for i, (yv, w_ref) in enumerate(branches):
        gate = jax.nn.sigmoid(_mm(h, wgt_ref[:, i * D_MODEL:(i + 1) * D_MODEL]))
        term = gate * _mm(yv, w_ref[...])
        merged = term if merged is None else merged + term
    o_ref[...] = x + _mm(merged, wo_ref[...])


def _merge(x, g, ya, ua, yb, yc, yd, d, wglu, wgt, wa, wb, wc, wd, wo, tm):
    n = x.shape[0]
    row = lambda i: (i, 0)
    fixed = lambda i: (0, 0)
    full = lambda a: pl.BlockSpec(a.shape, fixed, pipeline_mode=pl.Buffered(1))
    return pl.pallas_call(
        _merge_body,
        grid=(n // tm,),
        in_specs=[pl.BlockSpec((tm, D_MODEL), row), full(g),
                  pl.BlockSpec((tm * ya.shape[0] // n, ya.shape[1]), row), pl.BlockSpec((tm, 256), row),
                  pl.BlockSpec((tm, 256), row), pl.BlockSpec((tm, 256), row),
                  pl.BlockSpec((tm, 512), row),
                  full(d), full(wglu), full(wgt), full(wa), full(wb), full(wc), full(wd), full(wo)],
        out_specs=pl.BlockSpec((tm, D_MODEL), row),
        out_shape=jax.ShapeDtypeStruct((n, D_MODEL), F32),
        scratch_shapes=[pltpu.VMEM((tm, SSM_WIDTH // 2), F32), pltpu.VMEM((tm, SSM_WIDTH // 2), F32)],
        compiler_params=_cparams(("parallel",)),
        name="merge",
    )(x, g, ya, ua, yb, yc, yd, d, wglu, wgt, wa, wb, wc, wd, wo)


def _moe_body(x_ref, g_ref, wr_ref, br_ref, wg_ref, wu_ref, wd_ref, gf_ref, o_ref,
              h_scr, gate_scr, acc_scr, *, final_norm):
    gi = pl.program_id(1)
    tm = x_ref.shape[0]
    R = ROUTER_LANES

    @pl.when(gi == 0)
    def _():
        h = _rms(x_ref[...], g_ref[...]).astype(h_scr.dtype)
        h_scr[...] = h
        logits = _mm(h, wr_ref[...]) + br_ref[...]
        lane = lax.broadcasted_iota(jnp.int32, (tm, R), 1)
        lanef = lane.astype(F32)
        neg = -jnp.inf
        is_g = lane < MOE_GROUPS
        glm = jnp.where(is_g, logits, neg)
        gmax = jnp.max(glm, axis=1, keepdims=True)
        gsum = jnp.sum(jnp.where(is_g, jnp.exp(glm - gmax), 0.0), axis=1, keepdims=True)
        g_w = 1.0 / gsum
        g_i = jnp.min(jnp.where(glm == gmax, lanef, float(R)), axis=1, keepdims=True)
        e_grp = ((lane - MOE_GROUPS) // MOE_PER_GROUP).astype(F32)
        sel = (lane >= MOE_GROUPS) & (lane < MOE_GROUPS + MOE_EXPERTS) & (e_grp == g_i)
        elm = jnp.where(sel, logits, neg)
        e1 = jnp.max(elm, axis=1, keepdims=True)
        i1 = jnp.min(jnp.where(elm == e1, lanef, float(R)), axis=1, keepdims=True)
        elm2 = jnp.where(lanef == i1, neg, elm)
        e2 = jnp.max(elm2, axis=1, keepdims=True)
        i2 = jnp.min(jnp.where(elm2 == e2, lanef, float(R)), axis=1, keepdims=True)
        t = jnp.exp(e2 - e1)
        w1 = g_w / (1.0 + t)
        w2 = g_w * t / (1.0 + t)
        gates = jnp.where(lanef == i1, w1, 0.0) + jnp.where(lanef == i2, w2, 0.0)
        for grp in range(MOE_GROUPS):
            gate_scr[grp] = pltpu.roll(gates, R - MOE_GROUPS - MOE_PER_GROUP * grp, 1)
        acc_scr[...] = jnp.zeros(acc_scr.shape, F32)

    h = h_scr[...]
    hid = jax.nn.silu(_mm(h, wg_ref[...])) * _mm(h, wu_ref[...])
    gates = gate_scr[gi]
    hid = jnp.concatenate([hid[:, e * MOE_HIDDEN:(e + 1) * MOE_HIDDEN] * gates[:, e:e + 1]
                           for e in range(MOE_PER_GROUP)], axis=1)
    acc_scr[...] += _mm(hid, wd_ref[...])

    @pl.when(gi == MOE_GROUPS - 1)
    def _():
        o = x_ref[...] + acc_scr[...]
        if final_norm:
            o = _rms(o, gf_ref[...])
        o_ref[...] = o


def _moe(x, g, wr, br, wg, wu, wd, gf, tm, final_norm):
    n = x.shape[0]
    gw = MOE_PER_GROUP * MOE_HIDDEN
    row = lambda i, e: (i, 0)
    fixed = lambda i, e: (0, 0)
    return pl.pallas_call(
        functools.partial(_moe_body, final_norm=final_norm),
        grid=(n // tm, MOE_GROUPS),
        in_specs=[pl.BlockSpec((tm, D_MODEL), row), pl.BlockSpec((1, D_MODEL), fixed),
                  pl.BlockSpec((D_MODEL, ROUTER_LANES), fixed), pl.BlockSpec((1, ROUTER_LANES), fixed),
                  pl.BlockSpec((D_MODEL, gw), lambda i, e: (0, e)),
                  pl.BlockSpec((D_MODEL, gw), lambda i, e: (0, e)),
                  pl.BlockSpec((gw, D_MODEL), lambda i, e: (e, 0)),
                  pl.BlockSpec((1, D_MODEL), fixed)],
        out_specs=pl.BlockSpec((tm, D_MODEL), row),
        out_shape=jax.ShapeDtypeStruct((n, D_MODEL), F32),
        scratch_shapes=[pltpu.VMEM((tm, D_MODEL), wg.dtype), pltpu.VMEM((MOE_GROUPS, tm, ROUTER_LANES), F32),
                        pltpu.VMEM((tm, D_MODEL), F32)],
        compiler_params=_cparams(("parallel", "arbitrary")),
        name="moe",
    )(x, g, wr, br, wg, wu, wd, gf)


def _rope_tables(pos):
    half = DA_HEAD_DIM // 2
    inv = 1.0 / (ROPE_THETA ** (jnp.arange(half, dtype=F32) * 2.0 / DA_HEAD_DIM))
    ang = pos.astype(F32)[:, None] * inv[None, :]
    reps = DA_QK_WIDTH // half
    return jnp.tile(jnp.cos(ang), (1, reps)), jnp.tile(jnp.sin(ang), (1, reps))


def _block_diag(w):
    n, a, b = w.shape
    eye = jnp.eye(n, dtype=w.dtype)
    return jnp.einsum('nij,nm->nimj', w, eye).reshape(n * a, n * b)


def _layer_params(l, p, wdt):
    row = lambda a: a.astype(F32).reshape(1, -1)
    w_in = p['w_in'][l]
    p_lb = jax.nn.softmax(p['hg_lb_logits'].astype(F32), axis=0)
    lb = jnp.cumsum(p_lb, axis=0)[l] - p_lb[0]
    lam_init = 0.8 - 0.6 * math.exp(-0.3 * l)
    lam = (jnp.exp(jnp.sum(p['diff_lq1'][l].astype(F32) * p['diff_lk1'][l].astype(F32)))
           - jnp.exp(jnp.sum(p['diff_lq2'][l].astype(F32) * p['diff_lk2'][l].astype(F32))) + lam_init)
    w_router = jnp.concatenate(
        [p['moe_w_grp'][l], p['moe_w_exp'][l],
         jnp.zeros((D_MODEL, ROUTER_LANES - MOE_GROUPS - MOE_EXPERTS), F32)], axis=1)
    b_router = jnp.concatenate(
        [p['moe_b_grp'][l].astype(F32), p['moe_b_exp'][l].astype(F32),
         jnp.zeros((ROUTER_LANES - MOE_GROUPS - MOE_EXPERTS,), F32)]).reshape(1, -1)
    eh = MOE_EXPERTS * MOE_HIDDEN
    return dict(
        norm_mix=row(p['norm_mix'][l]),
        wdt=wdt,
        w_mix=w_in[:, :MIX_COLS].astype(wdt),
        w_gates=w_in[:, MIX_COLS:].astype(wdt),
        s5=(p['ssm_lambda_re'][l], p['ssm_lambda_im'][l], p['ssm_log_dt'][l], p['ssm_b_re'][l],
            p['ssm_b_im'][l], p['ssm_c_re'][l], p['ssm_c_im'][l]),
        ssm_d=row(p['ssm_d'][l]),
        w_glu=p['ssm_w_glu'][l].astype(wdt),
        hg_lb=lb.reshape(1, -1),
        hg_norm=jnp.tile(p['hg_norm'][l].astype(F32), HG_HEADS).reshape(1, -1),
        conv_w=p['lru_conv_w'][l].astype(F32),
        conv_b=row(p['lru_conv_b'][l]),
        w_ax=jnp.concatenate([_block_diag(p['lru_wa'][l]), _block_diag(p['lru_wx'][l])], axis=1).astype(wdt),
        b_ax=jnp.concatenate([p['lru_ba'][l], p['lru_bx'][l]]).astype(F32).reshape(1, -1),
        neg_c_softplus=(-LRU_C * jax.nn.softplus(-p['lru_lambda'][l].astype(F32))).reshape(1, -1),
        lam=lam.reshape(1).astype(F32),
        out_scale=1.0 - lam_init,
        diff_norm=row(p['diff_norm'][l]),
        w_br_a=p['w_br_a'][l].astype(wdt), w_br_b=p['w_br_b'][l].astype(wdt),
        w_br_c=p['w_br_c'][l].astype(wdt), w_br_d=p['w_br_d'][l].astype(wdt),
        w_out=p['w_out'][l].astype(wdt),
        norm_ffn=row(p['norm_ffn'][l]),
        w_router=w_router.astype(wdt), b_router=b_router,
        moe_gate=p['moe_w_gate'][l].astype(wdt).transpose(1, 0, 2).reshape(D_MODEL, eh),
        moe_up=p['moe_w_up'][l].astype(wdt).transpose(1, 0, 2).reshape(D_MODEL, eh),
        moe_down=p['moe_w_down'][l].astype(wdt).reshape(eh, D_MODEL),
    )


def _pad_rows(a, rows):
    return jnp.pad(a, ((0, 0), (0, rows - a.shape[1])) + ((0, 0),) * (a.ndim - 2))


def _trunk(x, pos0, states, cache, page_table, layers, norm_final, cfg):
    B, T, _ = x.shape
    n = B * T
    tm, s5_chunk, tpad, hg_tb, hg_c, lru_tb, tq = (cfg[k] for k in
                                                   ('tm', 's5_chunk', 'tpad', 'hg_tb', 'hg_c', 'lru_tb', 'tq'))
    cos, sin = _rope_tables(pos0 + jnp.arange(T, dtype=jnp.int32))
    if T % tm:
        cos, sin = jnp.tile(cos, (tm // T, 1)), jnp.tile(sin, (tm // T, 1))
    xf = x.reshape(n, D_MODEL)
    ks, vs, sts = [], [], []
    for l, lp in enumerate(layers):
        st = states[l]
        ua, hg, xg, qb, k, kb, v, vt, ucat = _inproj(xf, lp['norm_mix'], lp['w_mix'], cos, sin, tm, s5_chunk)
        ya, ssm_re, ssm_im = _s5_mixer(ucat, st[0], st[1], _s5_weights(*lp['s5'], s5_chunk, lp['wdt']),
                                       cfg['s5_bt'], T // s5_chunk)
        hg3 = _pad_rows(hg.reshape(B, T, 4 * HG_WIDTH), tpad)
        yb, s_t = _hgrn(hg3, _hgrn_state_to_t(st[2].astype(F32)), lp['hg_lb'], lp['hg_norm'],
                        hg_tb, hg_c, T, lp['wdt'] == F32)
        hg_state = _hgrn_state_from_t(s_t)
        xg3 = xg.reshape(B, T, 2 * LRU_WIDTH)
        conv0 = jnp.pad(st[4].astype(F32), ((0, 0), (8 - (CONV_WIDTH - 1), 0), (0, 0)))
        yc, lru_h = _lru(_pad_rows(xg3, tpad), conv0, st[3].astype(F32).reshape(B, 1, LRU_WIDTH),
                         lp['conv_w'], lp['conv_b'], lp['w_ax'], lp['b_ax'], lp['neg_c_softplus'],
                         lru_tb, T)
        xp = jnp.concatenate([st[4].astype(F32), xg3[:, :, :LRU_WIDTH]], axis=1)
        conv_buf = xp[:, T:]
        if cache is None:
            yd = _attn(lp['lam'], qb.reshape(B, T, -1), kb.reshape(B, T, -1), vt,
                       lp['diff_norm'], tq, cfg['wide_units'], lp['out_scale'])
        else:
            hw = DA_V_DIM
            q4 = qb.reshape(B, T, DA_HEADS, 2, DA_HEAD_DIM)
            zero = jnp.zeros_like(q4[:, :, :, 0])
            q2 = jnp.stack([jnp.concatenate([q4[:, :, :, 0], zero], -1),
                            jnp.concatenate([zero, q4[:, :, :, 1]], -1)], axis=1)
            q2 = q2.transpose(0, 1, 3, 2, 4).reshape(B, 2 * DA_HEADS * T, hw)
            nn = 128
            kn = _pad_rows(k.reshape(B, T * DA_HEADS, hw), nn)
            vn = _pad_rows(v.reshape(B, T * DA_HEADS, hw), nn)
            o = _dec_attn(page_table.reshape(-1), lp['lam'], q2, kn, vn, lp['diff_norm'],
                          cache[0], cache[1], l, cfg['pp'], T, lp['out_scale'])
            yd = o.reshape(B, DA_HEADS, T, hw).transpose(0, 2, 1, 3).reshape(B, T, DA_WIDTH)
        x1 = _merge(xf, lp['norm_mix'], ya, ua, yb[:, :T].reshape(n, -1),
                    yc[:, :T].reshape(n, -1), yd.reshape(n, -1), lp['ssm_d'], lp['w_glu'], lp['w_gates'],
                    lp['w_br_a'], lp['w_br_b'], lp['w_br_c'], lp['w_br_d'], lp['w_out'], tm)
        xf = _moe(x1, lp['norm_ffn'], lp['w_router'], lp['b_router'], lp['moe_gate'], lp['moe_up'],
                  lp['moe_down'], norm_final, cfg['tm_moe'], l == len(layers) - 1)
        ks.append(k.reshape(B, T, DA_HEADS, 2 * DA_HEAD_DIM))
        vs.append(v.reshape(B, T, DA_HEADS, DA_V_DIM))
        sts.append((ssm_re, ssm_im, hg_state, lru_h.reshape(B, LRU_WIDTH), conv_buf))
    stacked = [jnp.stack([s[j] for s in sts]) for j in range(5)]
    return xf.reshape(B, T, D_MODEL), jnp.stack(ks), jnp.stack(vs), stacked


PROMPT_CFG = dict(tm=512, tm_moe=1024, s5_chunk=16, s5_bt=1, tpad=2048, hg_tb=256, hg_c=16, lru_tb=256, tq=256,
                  wide_units=4)
SAMPLE_CFG = dict(tm=128, tm_moe=128, s5_chunk=4, s5_bt=32, tpad=16, hg_tb=16, hg_c=16, lru_tb=16, tq=0, pp=16)


def kernel(x_prompt, x_sample, cache_k, cache_v, page_table, state_ssm_re, state_ssm_im, state_hgrn,
           state_lru, state_conv, norm_mix, w_in, ssm_lambda_re, ssm_lambda_im, ssm_log_dt, ssm_b_re,
           ssm_b_im, ssm_c_re, ssm_c_im, ssm_d, ssm_w_glu, hg_lb_logits, hg_norm, lru_conv_w, lru_conv_b,
           lru_wa, lru_ba, lru_wx, lru_bx, lru_lambda, diff_lq1, diff_lk1, diff_lq2, diff_lk2, diff_norm,
           w_br_a, w_br_b, w_br_c, w_br_d, w_out, norm_ffn, moe_w_grp, moe_b_grp, moe_w_exp, moe_b_exp,
           moe_w_gate, moe_w_up, moe_w_down, norm_final):
    p = dict(norm_mix=norm_mix, w_in=w_in, ssm_lambda_re=ssm_lambda_re, ssm_lambda_im=ssm_lambda_im,
             ssm_log_dt=ssm_log_dt, ssm_b_re=ssm_b_re, ssm_b_im=ssm_b_im, ssm_c_re=ssm_c_re,
             ssm_c_im=ssm_c_im, ssm_d=ssm_d, ssm_w_glu=ssm_w_glu, hg_lb_logits=hg_lb_logits,
             hg_norm=hg_norm, lru_conv_w=lru_conv_w, lru_conv_b=lru_conv_b, lru_wa=lru_wa, lru_ba=lru_ba,
             lru_wx=lru_wx, lru_bx=lru_bx, lru_lambda=lru_lambda, diff_lq1=diff_lq1, diff_lk1=diff_lk1,
             diff_lq2=diff_lq2, diff_lk2=diff_lk2, diff_norm=diff_norm, w_br_a=w_br_a, w_br_b=w_br_b,
             w_br_c=w_br_c, w_br_d=w_br_d, w_out=w_out, norm_ffn=norm_ffn, moe_w_grp=moe_w_grp,
             moe_b_grp=moe_b_grp, moe_w_exp=moe_w_exp, moe_b_exp=moe_b_exp, moe_w_gate=moe_w_gate,
             moe_w_up=moe_w_up, moe_w_down=moe_w_down)
    layers = [_layer_params(l, p, BF16) for l in range(DEPTH)]
    layers_f32 = [_layer_params(l, p, F32) for l in range(DEPTH)]
    gf = norm_final.astype(F32).reshape(1, -1)
    Bp = x_prompt.shape[0]
    Bs = x_sample.shape[0]
    zero_states = [(jnp.zeros((Bp, SSM_GROUPS, SSM_STATE), F32), jnp.zeros((Bp, SSM_GROUPS, SSM_STATE), F32),
                    jnp.zeros((Bp, HG_HEADS, HG_HEAD_DIM, HG_HEAD_DIM), F32), jnp.zeros((Bp, LRU_WIDTH), F32),
                    jnp.zeros((Bp, CONV_WIDTH - 1, LRU_WIDTH), F32)) for _ in range(DEPTH)]
    y_p, k_p, v_p, st_p = _trunk(x_prompt, 0, zero_states, None, None, layers, gf, PROMPT_CFG)
    past_len = page_table.shape[1] * PAGE_SIZE
    sample_states = [(state_ssm_re[l], state_ssm_im[l], state_hgrn[l], state_lru[l], state_conv[l])
                     for l in range(DEPTH)]
    n_pool = cache_k.shape[1]
    rows = PAGE_SIZE * DA_HEADS
    cache = (cache_k.reshape(DEPTH, n_pool, rows, 2 * DA_HEAD_DIM), cache_v.reshape(DEPTH, n_pool, rows, DA_V_DIM))
    y_s, k_s, v_s, st_s = _trunk(x_sample, past_len, sample_states, cache, page_table, layers_f32, gf,
                                 SAMPLE_CFG)
    return (y_p, y_s, k_p, v_p, k_s, v_s,
            st_p[0], st_p[1], st_s[0], st_s[1], st_p[2], st_s[2], st_p[3], st_s[3], st_p[4], st_s[4])
```

```python
import functools
import math

import jax
import jax.numpy as jnp
from jax import lax
from jax.experimental import pallas as pl
from jax.experimental.pallas import tpu as pltpu

F32 = jnp.float32
BF16 = jnp.bfloat16

D_MODEL = 1024
DEPTH = 2
PAGE_SIZE = 128
SSM_WIDTH = 256
SSM_GROUP = 16
SSM_GROUPS = 16
SSM_STATE = 64
HG_WIDTH = 256
HG_HEAD_DIM = 64
HG_HEADS = 4
LRU_WIDTH = 256
LRU_BLOCKS = 4
LRU_BLOCK = 64
CONV_WIDTH = 4
LRU_C = 8.0
DA_HEADS = 4
DA_HEAD_DIM = 64
DA_V_DIM = 128
DA_QK_WIDTH = 512
DA_WIDTH = 512
ROPE_THETA = 10000.0
MASK_VALUE = -1e30
N_BRANCH = 4
MOE_GROUPS = 4
MOE_PER_GROUP = 8
MOE_EXPERTS = 32
MOE_HIDDEN = 128
NORM_EPS = 1e-6
MIX_COLS = 3328
ROUTER_LANES = 128
VMEM_LIMIT = 56 * 1024 * 1024
HI = lax.Precision.HIGHEST
Q_SCALE = DA_HEAD_DIM ** -0.5 * math.log2(math.e)
KV_UNIT = 256
SUBLANES = 8


def _cparams(sem):
    return pltpu.CompilerParams(dimension_semantics=sem, vmem_limit_bytes=VMEM_LIMIT)


def _rms(x, g):
    return x * lax.rsqrt(jnp.mean(x * x, axis=-1, keepdims=True) + NORM_EPS) * g


def _dot(a, b):
    return jnp.dot(a, b, preferred_element_type=F32)


def _mm(a, w, dims=(((1,), (0,)), ((), ()))):
    if w.dtype == BF16:
        return lax.dot_general(a.astype(BF16), w, dims, preferred_element_type=F32)
    return lax.dot_general(a.astype(F32), w, dims, preferred_element_type=F32, precision=HI)


NT_DIMS = (((1,), (1,)), ((), ()))
TN_DIMS = (((0,), (0,)), ((), ()))


def _dot_nt(a, b):
    return lax.dot_general(a, b, (((1,), (1,)), ((), ())), preferred_element_type=F32)


def _dot_tn(a, b):
    return lax.dot_general(a, b, (((0,), (0,)), ((), ())), preferred_element_type=F32)


def _split_dot(a, b_bf16, terms):
    out = None
    rem = a
    for _ in range(terms):
        piece = rem.astype(BF16)
        part = _dot(piece, b_bf16)
        out = part if out is None else out + part
        rem = rem - piece.astype(F32)
    return out


def _head_ones(width, head):
    r = lax.broadcasted_iota(jnp.int32, (width, width), 0) // head
    c = lax.broadcasted_iota(jnp.int32, (width, width), 1) // head
    return r == c


def _inproj_body(x_ref, g_ref, w_ref, cos_ref, sin_ref,
                 ua_ref, hg_ref, lru_ref, q_ref, k_ref, kb_ref, v_ref, vt_ref, uc_ref, ulo_scr, uhi_scr):
    h = _rms(x_ref[...], g_ref[...]).astype(w_ref.dtype)

    def mm(a, b):
        return _mm(h, w_ref[:, a:b])

    ua = mm(0, 256)
    ua_ref[...] = ua
    chunk = uc_ref.shape[1] // SSM_WIDTH
    half = SSM_WIDTH // 2
    ulo_scr[...] = ua[:, 0:half]
    uhi_scr[...] = ua[:, half:SSM_WIDTH]
    for t in range(chunk):
        rows = pl.ds(t, uc_ref.shape[0], stride=chunk)
        uc_ref[:, t * SSM_WIDTH:t * SSM_WIDTH + half] = ulo_scr[rows, :].astype(uc_ref.dtype)
        uc_ref[:, t * SSM_WIDTH + half:(t + 1) * SSM_WIDTH] = uhi_scr[rows, :].astype(uc_ref.dtype)
    hg_ref[...] = mm(256, 1280)
    lru_ref[...] = mm(1280, 1792)
    cos = cos_ref[...]
    sin = sin_ref[...]
    lane = lax.broadcasted_iota(jnp.int32, cos.shape, 1)
    first = (lane % DA_HEAD_DIM) < (DA_HEAD_DIM // 2)

    def rope(z):
        rot = jnp.where(first, -pltpu.roll(z, DA_QK_WIDTH - DA_HEAD_DIM // 2, 1),
                        pltpu.roll(z, DA_HEAD_DIM // 2, 1))
        return z * cos + rot * sin

    q = rope(mm(1792, 2304))
    q_ref[...] = (q * Q_SCALE).astype(q_ref.dtype)
    k = rope(mm(2304, 2816))
    kb_ref[...] = k.astype(BF16)
    v = mm(2816, 3328)
    tm = k.shape[0]
    unit = vt_ref.shape[2]
    for u in range(tm // unit):
        vt_ref[u] = v[u * unit:(u + 1) * unit, :].T.astype(BF16)
    for hd in range(DA_HEADS):
        cs = slice(hd * DA_V_DIM, (hd + 1) * DA_V_DIM)
        k_ref[pl.ds(hd, tm, stride=DA_HEADS), :] = k[:, cs]
        v_ref[pl.ds(hd, tm, stride=DA_HEADS), :] = v[:, cs]


def _inproj(x, g, w, cos, sin, tm, chunk):
    n = x.shape[0]
    ntab = cos.shape[0] // tm
    row = lambda i: (i, 0)
    fixed = lambda i: (0, 0)
    tab = lambda i: (i % ntab, 0)
    outs = ((1, 256, F32), (1, 1024, F32), (1, 512, F32), (1, 512, w.dtype),
            (DA_HEADS, DA_V_DIM, F32), (1, 512, BF16), (DA_HEADS, DA_V_DIM, F32))
    unit = min(tm, KV_UNIT)
    return pl.pallas_call(
        _inproj_body,
        grid=(n // tm,),
        in_specs=[pl.BlockSpec((tm, D_MODEL), row), pl.BlockSpec((1, D_MODEL), fixed),
                  pl.BlockSpec((D_MODEL, MIX_COLS), fixed),
                  pl.BlockSpec((tm, 512), tab), pl.BlockSpec((tm, 512), tab)],
        out_specs=[pl.BlockSpec((tm * r, wd), row) for r, wd, _ in outs]
                  + [pl.BlockSpec((tm // unit, DA_WIDTH, unit), lambda i: (i, 0, 0)),
                     pl.BlockSpec((tm // chunk, chunk * SSM_WIDTH), row)],
        out_shape=[jax.ShapeDtypeStruct((n * r, wd), dt) for r, wd, dt in outs]
                  + [jax.ShapeDtypeStruct((n // unit, DA_WIDTH, unit), BF16),
                     jax.ShapeDtypeStruct((n // chunk, chunk * SSM_WIDTH), w.dtype)],
        scratch_shapes=[pltpu.VMEM((tm, SSM_WIDTH // 2), F32), pltpu.VMEM((tm, SSM_WIDTH // 2), F32)],
        compiler_params=_cparams(("parallel",)),
        name="inproj",
    )(x, g, w, cos, sin)


S5_STATE_LANES = SSM_GROUPS * SSM_STATE
S5_HALF = S5_STATE_LANES // 2


def _s5_body(u_ref, k_ref, p_ref, q_ref, a_ref, h0_ref, y_ref, hf_ref, pu_scr, hs_scr, *, L, bt, nc):
    W = SSM_WIDTH
    HW = W // 2
    SL, SH = S5_STATE_LANES, S5_HALF
    for half in range(2):
        acc = None
        for t in range(L):
            c0 = t * W + half * HW
            part = _mm(u_ref[:, c0:c0 + HW], p_ref[t, half])
            acc = part if acc is None else acc + part
        pu_scr[:, half * SH:(half + 1) * SH] = acc[:, 0:SH]
        pu_scr[:, SL + half * SH:SL + (half + 1) * SH] = acc[:, SH:2 * SH]
    ar2 = a_ref[0:1, :]
    ai2 = a_ref[1:2, :]

    assert bt == 1 or nc == 1

    def step(c, h):
        rows = pl.ds(c * bt, bt)
        hs_scr[rows, :] = h
        return ar2 * h + ai2 * pltpu.roll(h, SL, 1) + pu_scr[rows, :]

    hf_ref[0] = lax.fori_loop(0, nc, step, h0_ref[0])
    hs = [jnp.concatenate([hs_scr[:, half * SH:(half + 1) * SH],
                           hs_scr[:, SL + half * SH:SL + (half + 1) * SH]], axis=1).astype(k_ref.dtype)
          for half in range(2)]
    for t2 in range(L):
        acc = jnp.concatenate([_mm(hs[0], q_ref[t2, 0]), _mm(hs[1], q_ref[t2, 1])], axis=1)
        for t in range(t2 + 1):
            acc = acc + _mm(u_ref[:, t * W:(t + 1) * W], k_ref[t2 - t])
        y_ref[:, t2 * W:(t2 + 1) * W] = acc


def _s5(ucat, kbd, pmat, qmat, a2, h0, bt, nc):
    rows_all, lw = ucat.shape
    L = lw // SSM_WIDTH
    rows = bt * nc
    once = lambda a: pl.BlockSpec(a.shape, lambda i: (0,) * a.ndim, pipeline_mode=pl.Buffered(1))
    return pl.pallas_call(
        functools.partial(_s5_body, L=L, bt=bt, nc=nc),
        grid=(rows_all // rows,),
        in_specs=[pl.BlockSpec((rows, lw), lambda i: (i, 0)), once(kbd), once(pmat), once(qmat), once(a2),
                  pl.BlockSpec((1, bt, 2 * S5_STATE_LANES), lambda i: (i, 0, 0))],
        out_specs=[pl.BlockSpec((rows, lw), lambda i: (i, 0)),
                   pl.BlockSpec((1, bt, 2 * S5_STATE_LANES), lambda i: (i, 0, 0))],
        out_shape=[jax.ShapeDtypeStruct((rows_all, lw), F32), jax.ShapeDtypeStruct(h0.shape, F32)],
        scratch_shapes=[pltpu.VMEM((rows, 2 * S5_STATE_LANES), F32), pltpu.VMEM((rows, 2 * S5_STATE_LANES), F32)],
        compiler_params=_cparams(("parallel",)),
        name="s5",
    )(ucat, kbd, pmat, qmat, a2, h0)


def _s5_weights(lam_re, lam_im, log_dt, b_re, b_im, c_re, c_im, L, wdtype):
    G, P, J = SSM_GROUPS, SSM_STATE, SSM_GROUP
    lr, li = lam_re.astype(F32), lam_im.astype(F32)
    dt = jnp.exp(log_dt.astype(F32))[:, None]
    mag = jnp.exp(lr * dt)
    ar = mag * jnp.cos(li * dt)
    ai = mag * jnp.sin(li * dt)
    den = lr * lr + li * li
    fr = ((ar - 1.0) * lr + ai * li) / den
    fi = (ai * lr - (ar - 1.0) * li) / den
    br, bi = b_re.astype(F32), b_im.astype(F32)
    bbr = fr[..., None] * br - fi[..., None] * bi
    bbi = fr[..., None] * bi + fi[..., None] * br
    tau = jnp.arange(L + 1, dtype=F32)[:, None, None]
    pmag = jnp.exp(lr * dt * tau)
    pr = pmag * jnp.cos(li * dt * tau)
    pi = pmag * jnp.sin(li * dt * tau)
    t1r = pr[..., None] * bbr - pi[..., None] * bbi
    t1i = pr[..., None] * bbi + pi[..., None] * bbr
    cr, ci = c_re.astype(F32), c_im.astype(F32)
    kt = (jnp.einsum('gip,tgpj->tgij', cr, t1r, precision=HI)
          - jnp.einsum('gip,tgpj->tgij', ci, t1i, precision=HI))
    GH = G // 2

    def block_diag(a, rows_per_group, cols_per_group, groups):
        r = lax.broadcasted_iota(jnp.int32, a.shape[-2:], 0) // rows_per_group
        c = lax.broadcasted_iota(jnp.int32, a.shape[-2:], 1) // cols_per_group
        return jnp.where(r % groups == c % groups, a, 0.0)

    kji = kt[:L].transpose(0, 1, 3, 2).reshape(L, G * J, 1, J)
    kbd = block_diag(jnp.broadcast_to(kji, (L, G * J, G, J)).reshape(L, G * J, G * J), J, J, G)
    rev = L - 1 - jnp.arange(L)
    ph = jnp.stack([t1r[rev], t1i[rev]], axis=1).reshape(L, 2, 2, GH, P, J)
    ph = ph.transpose(0, 2, 3, 5, 1, 4).reshape(L, 2, GH * J, 2, 1, P)
    pmat = block_diag(jnp.broadcast_to(ph, (L, 2, GH * J, 2, GH, P)).reshape(L, 2, GH * J, 2 * GH * P), J, P, GH)
    car = cr[None] * pr[1:, :, None, :] - ci[None] * pi[1:, :, None, :]
    cai = cr[None] * pi[1:, :, None, :] + ci[None] * pr[1:, :, None, :]
    qh = jnp.stack([car, -cai], axis=1).reshape(L, 2, 2, GH, J, P)
    qh = qh.transpose(0, 2, 1, 3, 5, 4).reshape(L, 2, 2 * GH * P, 1, J)
    qmat = block_diag(jnp.broadcast_to(qh, (L, 2, 2 * GH * P, GH, J)).reshape(L, 2, 2 * GH * P, GH * J), P, J, GH)
    a_l = jnp.stack([jnp.concatenate([pr[L].reshape(-1), pr[L].reshape(-1)]),
                     jnp.concatenate([-pi[L].reshape(-1), pi[L].reshape(-1)])])
    return kbd.astype(wdtype), pmat.astype(wdtype), qmat.astype(wdtype), a_l


def _s5_mixer(ucat, h0_re, h0_im, wts, bt, nc):
    B = h0_re.shape[0]
    h0 = jnp.concatenate([h0_re.reshape(B, -1), h0_im.reshape(B, -1)], axis=-1).astype(F32)
    y, hf = _s5(ucat, *wts, h0.reshape(B // bt, bt, -1), bt, nc)
    hf = hf.reshape(B, 2, SSM_GROUPS, SSM_STATE)
    return y, hf[:, 0], hf[:, 1]


def _hgrn_body(hg_ref, s0_ref, lb_ref, ng_ref, y_ref, sf_ref,
               st_scr, k_scr, b_scr, v_scr, w_scr, *, tb, c, t_valid, precise):
    j = pl.program_id(1)
    W = HG_WIDTH
    mdt = F32 if precise else BF16

    @pl.when(j == 0)
    def _():
        st_scr[...] = s0_ref[0]
        k_scr[0:c, :] = jnp.zeros((c, W), F32)
        b_scr[0:c, :] = jnp.zeros((c, W), F32)
        v_scr[0:c, :] = jnp.zeros((c, W), F32)

    q = hg_ref[0, :, 0:W]
    lb = lb_ref[...]
    fv = lb + (1.0 - lb) * jax.nn.sigmoid(hg_ref[0, :, W:2 * W])
    logf = jnp.log(fv) * math.log2(math.e)
    kk = 1.0 - fv
    v = hg_ref[0, :, 2 * W:3 * W]
    row = lax.broadcasted_iota(jnp.int32, (tb, W), 0)
    if t_valid < tb:
        valid = row < t_valid
        logf = jnp.where(valid, logf, 0.0)
        kk = jnp.where(valid, kk, 0.0)
    ri = lax.broadcasted_iota(jnp.int32, (tb, tb), 0)
    ci = lax.broadcasted_iota(jnp.int32, (tb, tb), 1)
    tril = ((ri // c == ci // c) & (ci <= ri)).astype(BF16)
    b = _split_dot_lhs_exact(tril, logf)
    k_scr[c:c + tb, :] = kk
    b_scr[c:c + tb, :] = b
    v_scr[c:c + tb, :] = v
    rin = row % c
    for d in range(c):
        ksh = k_scr[c - d:c - d + tb, :]
        bsh = b_scr[c - d:c - d + tb, :]
        w = jnp.where(rin >= d, q * ksh * jnp.exp2(b - bsh), 0.0)
        w_scr[d * tb:(d + 1) * tb, :] = w.astype(mdt)
    ones_bd = _head_ones(W, HG_HEAD_DIM).astype(BF16)
    att = _mm(w_scr[...], ones_bd.astype(mdt))
    o = att[0:tb] * v
    for d in range(1, c):
        o = o + att[d * tb:(d + 1) * tb] * v_scr[c - d:c - d + tb, :]
    bd = _head_ones(W, HG_HEAD_DIM)
    outs = []
    for ch in range(tb // c):
        sl = slice(ch * c, (ch + 1) * c)
        bc = b[sl]
        bl = bc[c - 1:c, :]
        st = st_scr[...]
        outs.append(_mm(q[sl] * jnp.exp2(bc), st.astype(mdt), NT_DIMS))
        khat = (kk[sl] * jnp.exp2(bl - bc)).astype(mdt)
        upd = _mm(v[sl], khat, TN_DIMS)
        st_scr[...] = st * jnp.exp2(bl) + jnp.where(bd, upd, 0.0)
    o = o + jnp.concatenate(outs, axis=0) if len(outs) > 1 else o + outs[0]
    ms = _split_dot(o * o, ones_bd, 3 if precise else 2) * (1.0 / HG_HEAD_DIM)
    y = o * lax.rsqrt(ms + NORM_EPS) * ng_ref[...]
    y_ref[0] = y * jax.nn.silu(hg_ref[0, :, 3 * W:4 * W])
    sf_ref[0] = st_scr[...]


def _split_dot_lhs_exact(a_bf16, b):
    out = None
    rem = b
    for _ in range(3):
        piece = rem.astype(BF16)
        part = _dot(a_bf16, piece)
        out = part if out is None else out + part
        rem = rem - piece.astype(F32)
    return out


def _hgrn(hg, s0t, lb, ng, tb, c, t_valid, precise):
    B, T, _ = hg.shape
    W = HG_WIDTH
    return pl.pallas_call(
        functools.partial(_hgrn_body, tb=tb, c=c, t_valid=t_valid, precise=precise),
        grid=(B, T // tb),
        in_specs=[pl.BlockSpec((1, tb, 4 * W), lambda b, j: (b, j, 0)),
                  pl.BlockSpec((1, W, W), lambda b, j: (b, 0, 0)),
                  pl.BlockSpec((1, W), lambda b, j: (0, 0)),
                  pl.BlockSpec((1, W), lambda b, j: (0, 0))],
        out_specs=[pl.BlockSpec((1, tb, W), lambda b, j: (b, j, 0)),
                   pl.BlockSpec((1, W, W), lambda b, j: (b, 0, 0))],
        out_shape=[jax.ShapeDtypeStruct((B, T, W), F32), jax.ShapeDtypeStruct((B, W, W), F32)],
        scratch_shapes=[pltpu.VMEM((W, W), F32), pltpu.VMEM((c + tb, W), F32),
                        pltpu.VMEM((c + tb, W), F32), pltpu.VMEM((c + tb, W), F32),
                        pltpu.VMEM((c * tb, W), F32 if precise else BF16)],
        compiler_params=_cparams(("parallel", "arbitrary")),
        name="hgrn2",
    )(hg, s0t, lb, ng)


def _hgrn_state_to_t(s):
    B = s.shape[0]
    eye = jnp.eye(HG_HEADS, dtype=s.dtype)
    return jnp.einsum('bhde,hg->bhegd', s, eye).reshape(B, HG_WIDTH, HG_WIDTH)


def _hgrn_state_from_t(st):
    B = st.shape[0]
    s5 = st.reshape(B, HG_HEADS, HG_HEAD_DIM, HG_HEADS, HG_HEAD_DIM)
    idx = jnp.arange(HG_HEADS)
    return s5[:, idx, :, idx, :].transpose(1, 0, 3, 2)


def _lru_body(x_ref, c0_ref, h0_ref, cw_ref, cb_ref, wax_ref, bax_ref, nsp_ref, y_ref, hl_ref,
              xs_scr, hc_scr, *, tb, r_last):
    j = pl.program_id(1)
    W = LRU_WIDTH

    @pl.when(j == 0)
    def _():
        xs_scr[0:8, :] = c0_ref[0]
        hc_scr[...] = h0_ref[0]

    x = x_ref[0, :, 0:W]
    xs_scr[8:8 + tb, :] = x
    xc = cb_ref[...] + x * cw_ref[3:4, :]
    for jj in range(CONV_WIDTH - 1):
        xc = xc + xs_scr[5 + jj:5 + jj + tb, :] * cw_ref[jj:jj + 1, :]
    tail = xs_scr[tb:tb + 8, :]
    xs_scr[0:8, :] = tail
    rg = _mm(xc, wax_ref[...]) + bax_ref[...]
    r = jax.nn.sigmoid(rg[:, 0:W])
    ig = jax.nn.sigmoid(rg[:, W:2 * W])
    log_a = nsp_ref[...] * r
    a = jnp.exp(log_a)
    bt = jnp.sqrt(jnp.maximum(1.0 - a * a, 0.0)) * (ig * xc)
    row = lax.broadcasted_iota(jnp.int32, (tb, W), 0) % SUBLANES
    k = 1
    while k < SUBLANES:
        keep = row >= k
        a_sh = jnp.where(keep, pltpu.roll(a, k, 0), 1.0)
        b_sh = jnp.where(keep, pltpu.roll(bt, k, 0), 0.0)
        bt = a * b_sh + bt
        a = a * a_sh
        k *= 2
    carry = hc_scr[...]
    groups = []
    for g0 in range(0, tb, SUBLANES):
        hg = a[g0:g0 + SUBLANES] * carry + bt[g0:g0 + SUBLANES]
        groups.append(hg)
        carry = hg[SUBLANES - 1:SUBLANES]
    h = jnp.concatenate(groups, axis=0)
    y_ref[0] = jax.nn.gelu(x_ref[0, :, W:2 * W]) * h
    hc = h[r_last:r_last + 1, :]
    hc_scr[...] = hc
    hl_ref[0] = hc


def _lru(xg, c0, h0, cw, cb, wax, bax, nsp, tb, t_valid):
    B, T, _ = xg.shape
    W = LRU_WIDTH
    fixed = lambda b, j: (0, 0)
    return pl.pallas_call(
        functools.partial(_lru_body, tb=tb, r_last=(t_valid - 1) % tb),
        grid=(B, T // tb),
        in_specs=[pl.BlockSpec((1, tb, 2 * W), lambda b, j: (b, j, 0)),
                  pl.BlockSpec((1, 8, W), lambda b, j: (b, 0, 0)),
                  pl.BlockSpec((1, 1, W), lambda b, j: (b, 0, 0)),
                  pl.BlockSpec((CONV_WIDTH, W), fixed), pl.BlockSpec((1, W), fixed),
                  pl.BlockSpec((W, 2 * W), fixed), pl.BlockSpec((1, 2 * W), fixed),
                  pl.BlockSpec((1, W), fixed)],
        out_specs=[pl.BlockSpec((1, tb, W), lambda b, j: (b, j, 0)),
                   pl.BlockSpec((1, 1, W), lambda b, j: (b, 0, 0))],
        out_shape=[jax.ShapeDtypeStruct((B, T, W), F32), jax.ShapeDtypeStruct((B, 1, W), F32)],
        scratch_shapes=[pltpu.VMEM((8 + tb, W), F32), pltpu.VMEM((1, W), F32)],
        compiler_params=_cparams(("parallel", "arbitrary")),
        name="rglru",
    )(xg, c0, h0, cw, cb, wax, bax, nsp)


def _attn_body(lam_ref, q_ref, k_ref, vt_ref, g_ref, o_ref, q2_scr, m_scr, l_scr, acc_scr,
               *, tq, wide_units, out_scale):
    qi = pl.program_id(1)
    hw = DA_V_DIM
    lane = lax.broadcasted_iota(jnp.int32, (tq, hw), 1)
    for h in range(DA_HEADS):
        qh = q_ref[0, :, h * hw:(h + 1) * hw]
        zero = jnp.zeros_like(qh)
        q2_scr[h, 0:tq, :] = jnp.where(lane < DA_HEAD_DIM, qh, zero)
        q2_scr[h, tq:2 * tq, :] = jnp.where(lane >= DA_HEAD_DIM, qh, zero)
    m_scr[...] = jnp.full(m_scr.shape, MASK_VALUE, F32)
    l_scr[...] = jnp.zeros(l_scr.shape, F32)
    acc_scr[...] = jnp.zeros(acc_scr.shape, F32)

    def block(u0, nu, diagonal):
        tk = nu * KV_UNIT
        r0 = pl.multiple_of(u0 * KV_UNIT, KV_UNIT)
        for h in range(DA_HEADS):
            cs = slice(h * hw, (h + 1) * hw)
            st = _dot_nt(k_ref[0, pl.ds(r0, tk), cs], q2_scr[h])
            if diagonal:
                keyi = lax.broadcasted_iota(jnp.int32, (tk, 2 * tq), 0)
                qryi = lax.broadcasted_iota(jnp.int32, (tk, 2 * tq), 1) % tq
                st = jnp.where(keyi <= qryi, st, MASK_VALUE)
            m = m_scr[h]
            m_new = jnp.maximum(m, jnp.max(st, axis=0, keepdims=True))
            alpha = jnp.exp2(m - m_new)
            p = jnp.exp2(st - m_new)
            l_scr[h] = alpha * l_scr[h] + jnp.sum(p, axis=0, keepdims=True)
            pb = p.astype(BF16)
            pv = _dot(vt_ref[u0, cs, :], pb[0:KV_UNIT])
            for u in range(1, nu):
                pv = pv + _dot(vt_ref[u0 + u, cs, :], pb[u * KV_UNIT:(u + 1) * KV_UNIT])
            acc_scr[h] = alpha * acc_scr[h] + pv
            m_scr[h] = m_new

    nq = tq // KV_UNIT
    n_before = qi * nq
    done = 0
    width = wide_units
    while width >= 1:
        n_blocks = (n_before - done) // width

        def step(jb, carry, width=width, done=done):
            block(done + jb * width, width, False)
            return carry

        lax.fori_loop(0, n_blocks, step, 0)
        done = done + n_blocks * width
        width //= 2
    block(n_before, nq, True)
    lam = lam_ref[0]
    for h in range(DA_HEADS):
        on = acc_scr[h] / l_scr[h]
        o = (on[:, 0:tq] - lam * on[:, tq:2 * tq]).T
        o_ref[0, :, h * hw:(h + 1) * hw] = _rms(o, g_ref[...]) * out_scale


def _attn(lam, q, k, vt, g, tq, wide_units, out_scale):
    B, T, Wd = q.shape
    hw = DA_V_DIM
    units = T // KV_UNIT
    return pl.pallas_call(
        functools.partial(_attn_body, tq=tq, wide_units=wide_units, out_scale=out_scale),
        grid=(B, T // tq),
        in_specs=[pl.BlockSpec(memory_space=pltpu.SMEM),
                  pl.BlockSpec((1, tq, Wd), lambda b, i: (b, i, 0)),
                  pl.BlockSpec((1, T, Wd), lambda b, i: (b, 0, 0)),
                  pl.BlockSpec((units, Wd, KV_UNIT), lambda b, i: (b, 0, 0)),
                  pl.BlockSpec((1, hw), lambda b, i: (0, 0))],
        out_specs=pl.BlockSpec((1, tq, Wd), lambda b, i: (b, i, 0)),
        out_shape=jax.ShapeDtypeStruct((B, T, Wd), F32),
        scratch_shapes=[pltpu.VMEM((DA_HEADS, 2 * tq, hw), BF16), pltpu.VMEM((DA_HEADS, 1, 2 * tq), F32),
                        pltpu.VMEM((DA_HEADS, 1, 2 * tq), F32), pltpu.VMEM((DA_HEADS, hw, 2 * tq), F32)],
        compiler_params=_cparams(("parallel", "arbitrary")),
        name="diff_attn",
    )(lam, q, k, vt, g)


def _dec_body(pt_ref, lam_ref, q_ref, kn_ref, vn_ref, g_ref, *rest, pp, t_new, out_scale):
    k_refs = rest[0:pp]
    v_refs = rest[pp:2 * pp]
    o_ref = rest[2 * pp]
    m_scr, l_scr, acc_scr, bias_scr = rest[2 * pp + 1:]
    j = pl.program_id(1)
    nrow = 2 * DA_HEADS * t_new

    @pl.when(j == 0)
    def _():
        m_scr[...] = jnp.full(m_scr.shape, MASK_VALUE, F32)
        l_scr[...] = jnp.zeros(l_scr.shape, F32)
        acc_scr[...] = jnp.zeros(acc_scr.shape, F32)
        ncol = PAGE_SIZE * DA_HEADS
        rh = (lax.broadcasted_iota(jnp.int32, (nrow, ncol), 0) // t_new) % DA_HEADS
        chd = lax.broadcasted_iota(jnp.int32, (nrow, ncol), 1) % DA_HEADS
        bias_scr[...] = jnp.where(rh == chd, 0.0, MASK_VALUE)

    def hi_lo(a):
        hi = a.astype(BF16)
        return jnp.concatenate([hi, (a - hi.astype(F32)).astype(BF16)], axis=0)

    def fold(a):
        return a[0:nrow] + a[nrow:2 * nrow]

    q = hi_lo(q_ref[0])

    def scores(keys):
        return fold(_dot_nt(q, keys.astype(BF16)))

    def update(ss, vals):
        m = m_scr[...]
        smax = ss[0]
        for s in ss[1:]:
            smax = jnp.maximum(smax, s)
        m_new = jnp.maximum(m, jnp.max(smax, axis=1, keepdims=True))
        alpha = jnp.exp2(m - m_new)
        ps = [jnp.exp2(s - m_new) for s in ss]
        psum = ps[0]
        for p in ps[1:]:
            psum = psum + p
        pv = _dot(hi_lo(ps[0]), vals[0])
        for p, vv in zip(ps[1:], vals[1:]):
            pv = pv + _dot(hi_lo(p), vv)
        l_scr[...] = alpha * l_scr[...] + jnp.sum(psum, axis=1, keepdims=True)
        acc_scr[...] = alpha * acc_scr[...] + fold(pv)
        m_scr[...] = m_new

    bias = bias_scr[...]
    update([scores(k_refs[i][...]) + bias for i in range(pp)],
           [v_refs[i][...].astype(BF16) for i in range(pp)])

    @pl.when(j == pl.num_programs(1) - 1)
    def _():
        nn = kn_ref.shape[1]
        r = lax.broadcasted_iota(jnp.int32, (nrow, nn), 0)
        cidx = lax.broadcasted_iota(jnp.int32, (nrow, nn), 1)
        ok = ((r // t_new) % DA_HEADS == cidx % DA_HEADS) & (cidx // DA_HEADS <= r % t_new)
        update([jnp.where(ok, scores(kn_ref[0]), MASK_VALUE)], [vn_ref[0].astype(BF16)])
        on = acc_scr[...] / l_scr[...]
        half = nrow // 2
        o = on[0:half] - lam_ref[0] * on[half:nrow]
        o_ref[0] = _rms(o, g_ref[...]) * out_scale


def _dec_attn(pt, lam, q2, kn, vn, g, ck, cv, layer, pp, t_new, out_scale):
    B, nrow, hw = q2.shape
    n_pages = pt.shape[0] // B
    nn = kn.shape[1]
    rows = PAGE_SIZE * DA_HEADS

    def page_spec(i):
        return pl.BlockSpec((None, None, rows, hw),
                            lambda b, j, pt_ref: (layer, pt_ref[b * n_pages + j * pp + i], 0, 0))

    grid_spec = pltpu.PrefetchScalarGridSpec(
        num_scalar_prefetch=1,
        grid=(B, n_pages // pp),
        in_specs=[pl.BlockSpec(memory_space=pltpu.SMEM),
                  pl.BlockSpec((1, nrow, hw), lambda b, j, pt_ref: (b, 0, 0)),
                  pl.BlockSpec((1, nn, hw), lambda b, j, pt_ref: (b, 0, 0)),
                  pl.BlockSpec((1, nn, hw), lambda b, j, pt_ref: (b, 0, 0)),
                  pl.BlockSpec((1, hw), lambda b, j, pt_ref: (0, 0))]
                 + [page_spec(i) for i in range(pp)] + [page_spec(i) for i in range(pp)],
        out_specs=pl.BlockSpec((1, nrow // 2, hw), lambda b, j, pt_ref: (b, 0, 0)),
        scratch_shapes=[pltpu.VMEM((nrow, 1), F32), pltpu.VMEM((nrow, 1), F32),
                        pltpu.VMEM((nrow, hw), F32), pltpu.VMEM((nrow, rows), F32)],
    )
    return pl.pallas_call(
        functools.partial(_dec_body, pp=pp, t_new=t_new, out_scale=out_scale),
        grid_spec=grid_spec,
        out_shape=jax.ShapeDtypeStruct((B, nrow // 2, hw), F32),
        compiler_params=_cparams(("parallel", "arbitrary")),
        name="paged_diff_attn",
    )(pt, lam, q2, kn, vn, g, *([ck] * pp), *([cv] * pp))


def _merge_body(x_ref, g_ref, ya_ref, ua_ref, yb_ref, yc_ref, yd_ref, d_ref, wglu_ref,
                wgt_ref, wa_ref, wb_ref, wc_ref, wd_ref, wo_ref, o_ref, ylo_scr, yhi_scr):
    x = x_ref[...]
    h = _rms(x, g_ref[...]).astype(wgt_ref.dtype)
    chunk = ya_ref.shape[1] // SSM_WIDTH
    half = SSM_WIDTH // 2
    for t in range(chunk):
        rows = pl.ds(t, ya_ref.shape[0], stride=chunk)
        ylo_scr[rows, :] = ya_ref[:, t * SSM_WIDTH:t * SSM_WIDTH + half]
        yhi_scr[rows, :] = ya_ref[:, t * SSM_WIDTH + half:(t + 1) * SSM_WIDTH]
    ya_tok = jnp.concatenate([ylo_scr[...], yhi_scr[...]], axis=1)
    z = jax.nn.gelu(ya_tok + d_ref[...] * ua_ref[...])
    ya = z * jax.nn.sigmoid(_mm(z, wglu_ref[...]))
    merged = None
    branches = ((ya, wa_ref), (yb_ref[...], wb_ref), (yc_ref[...], wc_ref), (yd_ref[...], wd_ref))
    for i, (yv, w_ref) in enumerate(branches):
        gate = jax.nn.sigmoid(_mm(h, wgt_ref[:, i * D_MODEL:(i + 1) * D_MODEL]))
        term = gate * _mm(yv, w_ref[...])
        merged = term if merged is None else merged + term
    o_ref[...] = x + _mm(merged, wo_ref[...])


def _merge(x, g, ya, ua, yb, yc, yd, d, wglu, wgt, wa, wb, wc, wd, wo, tm):
    n = x.shape[0]
    row = lambda i: (i, 0)
    fixed = lambda i: (0, 0)
    full = lambda a: pl.BlockSpec(a.shape, fixed, pipeline_mode=pl.Buffered(1))
    return pl.pallas_call(
        _merge_body,
        grid=(n // tm,),
        in_specs=[pl.BlockSpec((tm, D_MODEL), row), full(g),
                  pl.BlockSpec((tm * ya.shape[0] // n, ya.shape[1]), row), pl.BlockSpec((tm, 256), row),
                  pl.BlockSpec((tm, 256), row), pl.BlockSpec((tm, 256), row),
                  pl.BlockSpec((tm, 512), row),
                  full(d), full(wglu), full(wgt), full(wa), full(wb), full(wc), full(wd), full(wo)],
        out_specs=pl.BlockSpec((tm, D_MODEL), row),
        out_shape=jax.ShapeDtypeStruct((n, D_MODEL), F32),
        scratch_shapes=[pltpu.VMEM((tm, SSM_WIDTH // 2), F32), pltpu.VMEM((tm, SSM_WIDTH // 2), F32)],
        compiler_params=_cparams(("parallel",)),
        name="merge",
    )(x, g, ya, ua, yb, yc, yd, d, wglu, wgt, wa, wb, wc, wd, wo)


def _moe_body(x_ref, g_ref, wr_ref, br_ref, wg_ref, wu_ref, wd_ref, gf_ref, o_ref,
              h_scr, gate_scr, acc_scr, *, final_norm):
    gi = pl.program_id(1)
    tm = x_ref.shape[0]
    R = ROUTER_LANES

    @pl.when(gi == 0)
    def _():
        h = _rms(x_ref[...], g_ref[...]).astype(h_scr.dtype)
        h_scr[...] = h
        logits = _mm(h, wr_ref[...]) + br_ref[...]
        lane = lax.broadcasted_iota(jnp.int32, (tm, R), 1)
        lanef = lane.astype(F32)
        neg = -jnp.inf
        is_g = lane < MOE_GROUPS
        glm = jnp.where(is_g, logits, neg)
        gmax = jnp.max(glm, axis=1, keepdims=True)
        gsum = jnp.sum(jnp.where(is_g, jnp.exp(glm - gmax), 0.0), axis=1, keepdims=True)
        g_w = 1.0 / gsum
        g_i = jnp.min(jnp.where(glm == gmax, lanef, float(R)), axis=1, keepdims=True)
        e_grp = ((lane - MOE_GROUPS) // MOE_PER_GROUP).astype(F32)
        sel = (lane >= MOE_GROUPS) & (lane < MOE_GROUPS + MOE_EXPERTS) & (e_grp == g_i)
        elm = jnp.where(sel, logits, neg)
        e1 = jnp.max(elm, axis=1, keepdims=True)
        i1 = jnp.min(jnp.where(elm == e1, lanef, float(R)), axis=1, keepdims=True)
        elm2 = jnp.where(lanef == i1, neg, elm)
        e2 = jnp.max(elm2, axis=1, keepdims=True)
        i2 = jnp.min(jnp.where(elm2 == e2, lanef, float(R)), axis=1, keepdims=True)
        t = jnp.exp(e2 - e1)
        w1 = g_w / (1.0 + t)
        w2 = g_w * t / (1.0 + t)
        gates = jnp.where(lanef == i1, w1, 0.0) + jnp.where(lanef == i2, w2, 0.0)
        for grp in range(MOE_GROUPS):
            gate_scr[grp] = pltpu.roll(gates, R - MOE_GROUPS - MOE_PER_GROUP * grp, 1)
        acc_scr[...] = jnp.zeros(acc_scr.shape, F32)

    h = h_scr[...]
    hid = jax.nn.silu(_mm(h, wg_ref[...])) * _mm(h, wu_ref[...])
    gates = gate_scr[gi]
    hid = jnp.concatenate([hid[:, e * MOE_HIDDEN:(e + 1) * MOE_HIDDEN] * gates[:, e:e + 1]
                           for e in range(MOE_PER_GROUP)], axis=1)
    acc_scr[...] += _mm(hid, wd_ref[...])

    @pl.when(gi == MOE_GROUPS - 1)
    def _():
        o = x_ref[...] + acc_scr[...]
        if final_norm:
            o = _rms(o, gf_ref[...])
        o_ref[...] = o


def _moe(x, g, wr, br, wg, wu, wd, gf, tm, final_norm):
    n = x.shape[0]
    gw = MOE_PER_GROUP * MOE_HIDDEN
    row = lambda i, e: (i, 0)
    fixed = lambda i, e: (0, 0)
    return pl.pallas_call(
        functools.partial(_moe_body, final_norm=final_norm),
        grid=(n // tm, MOE_GROUPS),
        in_specs=[pl.BlockSpec((tm, D_MODEL), row), pl.BlockSpec((1, D_MODEL), fixed),
                  pl.BlockSpec((D_MODEL, ROUTER_LANES), fixed), pl.BlockSpec((1, ROUTER_LANES), fixed),
                  pl.BlockSpec((D_MODEL, gw), lambda i, e: (0, e)),
                  pl.BlockSpec((D_MODEL, gw), lambda i, e: (0, e)),
                  pl.BlockSpec((gw, D_MODEL), lambda i, e: (e, 0)),
                  pl.BlockSpec((1, D_MODEL), fixed)],
        out_specs=pl.BlockSpec((tm, D_MODEL), row),
        out_shape=jax.ShapeDtypeStruct((n, D_MODEL), F32),
        scratch_shapes=[pltpu.VMEM((tm, D_MODEL), wg.dtype), pltpu.VMEM((MOE_GROUPS, tm, ROUTER_LANES), F32),
                        pltpu.VMEM((tm, D_MODEL), F32)],
        compiler_params=_cparams(("parallel", "arbitrary")),
        name="moe",
    )(x, g, wr, br, wg, wu, wd, gf)


def _rope_tables(pos):
    half = DA_HEAD_DIM // 2
    inv = 1.0 / (ROPE_THETA ** (jnp.arange(half, dtype=F32) * 2.0 / DA_HEAD_DIM))
    ang = pos.astype(F32)[:, None] * inv[None, :]
    reps = DA_QK_WIDTH // half
    return jnp.tile(jnp.cos(ang), (1, reps)), jnp.tile(jnp.sin(ang), (1, reps))


def _block_diag(w):
    n, a, b = w.shape
    eye = jnp.eye(n, dtype=w.dtype)
    return jnp.einsum('nij,nm->nimj', w, eye).reshape(n * a, n * b)


def _layer_params(l, p, wdt):
    row = lambda a: a.astype(F32).reshape(1, -1)
    w_in = p['w_in'][l]
    p_lb = jax.nn.softmax(p['hg_lb_logits'].astype(F32), axis=0)
    lb = jnp.cumsum(p_lb, axis=0)[l] - p_lb[0]
    lam_init = 0.8 - 0.6 * math.exp(-0.3 * l)
    lam = (jnp.exp(jnp.sum(p['diff_lq1'][l].astype(F32) * p['diff_lk1'][l].astype(F32)))
           - jnp.exp(jnp.sum(p['diff_lq2'][l].astype(F32) * p['diff_lk2'][l].astype(F32))) + lam_init)
    w_router = jnp.concatenate(
        [p['moe_w_grp'][l], p['moe_w_exp'][l],
         jnp.zeros((D_MODEL, ROUTER_LANES - MOE_GROUPS - MOE_EXPERTS), F32)], axis=1)
    b_router = jnp.concatenate(
        [p['moe_b_grp'][l].astype(F32), p['moe_b_exp'][l].astype(F32),
         jnp.zeros((ROUTER_LANES - MOE_GROUPS - MOE_EXPERTS,), F32)]).reshape(1, -1)
    eh = MOE_EXPERTS * MOE_HIDDEN
    return dict(
        norm_mix=row(p['norm_mix'][l]),
        wdt=wdt,
        w_mix=w_in[:, :MIX_COLS].astype(wdt),
        w_gates=w_in[:, MIX_COLS:].astype(wdt),
        s5=(p['ssm_lambda_re'][l], p['ssm_lambda_im'][l], p['ssm_log_dt'][l], p['ssm_b_re'][l],
            p['ssm_b_im'][l], p['ssm_c_re'][l], p['ssm_c_im'][l]),
        ssm_d=row(p['ssm_d'][l]),
        w_glu=p['ssm_w_glu'][l].astype(wdt),
        hg_lb=lb.reshape(1, -1),
        hg_norm=jnp.tile(p['hg_norm'][l].astype(F32), HG_HEADS).reshape(1, -1),
        conv_w=p['lru_conv_w'][l].astype(F32),
        conv_b=row(p['lru_conv_b'][l]),
        w_ax=jnp.concatenate([_block_diag(p['lru_wa'][l]), _block_diag(p['lru_wx'][l])], axis=1).astype(wdt),
        b_ax=jnp.concatenate([p['lru_ba'][l], p['lru_bx'][l]]).astype(F32).reshape(1, -1),
        neg_c_softplus=(-LRU_C * jax.nn.softplus(-p['lru_lambda'][l].astype(F32))).reshape(1, -1),
        lam=lam.reshape(1).astype(F32),
        out_scale=1.0 - lam_init,
        diff_norm=row(p['diff_norm'][l]),
        w_br_a=p['w_br_a'][l].astype(wdt), w_br_b=p['w_br_b'][l].astype(wdt),
        w_br_c=p['w_br_c'][l].astype(wdt), w_br_d=p['w_br_d'][l].astype(wdt),
        w_out=p['w_out'][l].astype(wdt),
        norm_ffn=row(p['norm_ffn'][l]),
        w_router=w_router.astype(wdt), b_router=b_router,
        moe_gate=p['moe_w_gate'][l].astype(wdt).transpose(1, 0, 2).reshape(D_MODEL, eh),
        moe_up=p['moe_w_up'][l].astype(wdt).transpose(1, 0, 2).reshape(D_MODEL, eh),
        moe_down=p['moe_w_down'][l].astype(wdt).reshape(eh, D_MODEL),
    )


def _pad_rows(a, rows):
    return jnp.pad(a, ((0, 0), (0, rows - a.shape[1])) + ((0, 0),) * (a.ndim - 2))


def _trunk(x, pos0, states, cache, page_table, layers, norm_final, cfg):
    B, T, _ = x.shape
    n = B * T
    tm, s5_chunk, tpad, hg_tb, hg_c, lru_tb, tq = (cfg[k] for k in
                                                   ('tm', 's5_chunk', 'tpad', 'hg_tb', 'hg_c', 'lru_tb', 'tq'))
    cos, sin = _rope_tables(pos0 + jnp.arange(T, dtype=jnp.int32))
    if T % tm:
        cos, sin = jnp.tile(cos, (tm // T, 1)), jnp.tile(sin, (tm // T, 1))
    xf = x.reshape(n, D_MODEL)
    ks, vs, sts = [], [], []
    for l, lp in enumerate(layers):
        st = states[l]
        ua, hg, xg, qb, k, kb, v, vt, ucat = _inproj(xf, lp['norm_mix'], lp['w_mix'], cos, sin, tm, s5_chunk)
        ya, ssm_re, ssm_im = _s5_mixer(ucat, st[0], st[1], _s5_weights(*lp['s5'], s5_chunk, lp['wdt']),
                                       cfg['s5_bt'], T // s5_chunk)
        hg3 = _pad_rows(hg.reshape(B, T, 4 * HG_WIDTH), tpad)
        yb, s_t = _hgrn(hg3, _hgrn_state_to_t(st[2].astype(F32)), lp['hg_lb'], lp['hg_norm'],
                        hg_tb, hg_c, T, lp['wdt'] == F32)
        hg_state = _hgrn_state_from_t(s_t)
        xg3 = xg.reshape(B, T, 2 * LRU_WIDTH)
        conv0 = jnp.pad(st[4].astype(F32), ((0, 0), (8 - (CONV_WIDTH - 1), 0), (0, 0)))
        yc, lru_h = _lru(_pad_rows(xg3, tpad), conv0, st[3].astype(F32).reshape(B, 1, LRU_WIDTH),
                         lp['conv_w'], lp['conv_b'], lp['w_ax'], lp['b_ax'], lp['neg_c_softplus'],
                         lru_tb, T)
        xp = jnp.concatenate([st[4].astype(F32), xg3[:, :, :LRU_WIDTH]], axis=1)
        conv_buf = xp[:, T:]
        if cache is None:
            yd = _attn(lp['lam'], qb.reshape(B, T, -1), kb.reshape(B, T, -1), vt,
                       lp['diff_norm'], tq, cfg['wide_units'], lp['out_scale'])
        else:
            hw = DA_V_DIM
            q4 = qb.reshape(B, T, DA_HEADS, 2, DA_HEAD_DIM)
            zero = jnp.zeros_like(q4[:, :, :, 0])
            q2 = jnp.stack([jnp.concatenate([q4[:, :, :, 0], zero], -1),
                            jnp.concatenate([zero, q4[:, :, :, 1]], -1)], axis=1)
            q2 = q2.transpose(0, 1, 3, 2, 4).reshape(B, 2 * DA_HEADS * T, hw)
            nn = 128
            kn = _pad_rows(k.reshape(B, T * DA_HEADS, hw), nn)
            vn = _pad_rows(v.reshape(B, T * DA_HEADS, hw), nn)
            o = _dec_attn(page_table.reshape(-1), lp['lam'], q2, kn, vn, lp['diff_norm'],
                          cache[0], cache[1], l, cfg['pp'], T, lp['out_scale'])
            yd = o.reshape(B, DA_HEADS, T, hw).transpose(0, 2, 1, 3).reshape(B, T, DA_WIDTH)
        x1 = _merge(xf, lp['norm_mix'], ya, ua, yb[:, :T].reshape(n, -1),
                    yc[:, :T].reshape(n, -1), yd.reshape(n, -1), lp['ssm_d'], lp['w_glu'], lp['w_gates'],
                    lp['w_br_a'], lp['w_br_b'], lp['w_br_c'], lp['w_br_d'], lp['w_out'], tm)
        xf = _moe(x1, lp['norm_ffn'], lp['w_router'], lp['b_router'], lp['moe_gate'], lp['moe_up'],
                  lp['moe_down'], norm_final, cfg['tm_moe'], l == len(layers) - 1)
        ks.append(k.reshape(B, T, DA_HEADS, 2 * DA_HEAD_DIM))
        vs.append(v.reshape(B, T, DA_HEADS, DA_V_DIM))
        sts.append((ssm_re, ssm_im, hg_state, lru_h.reshape(B, LRU_WIDTH), conv_buf))
    stacked = [jnp.stack([s[j] for s in sts]) for j in range(5)]
    return xf.reshape(B, T, D_MODEL), jnp.stack(ks), jnp.stack(vs), stacked


PROMPT_CFG = dict(tm=512, tm_moe=1024, s5_chunk=16, s5_bt=1, tpad=2048, hg_tb=256, hg_c=16, lru_tb=512, tq=256,
                  wide_units=4)
SAMPLE_CFG = dict(tm=128, tm_moe=128, s5_chunk=4, s5_bt=32, tpad=16, hg_tb=16, hg_c=16, lru_tb=16, tq=0, pp=16)


def kernel(x_prompt, x_sample, cache_k, cache_v, page_table, state_ssm_re, state_ssm_im, state_hgrn,
           state_lru, state_conv, norm_mix, w_in, ssm_lambda_re, ssm_lambda_im, ssm_log_dt, ssm_b_re,
           ssm_b_im, ssm_c_re, ssm_c_im, ssm_d, ssm_w_glu, hg_lb_logits, hg_norm, lru_conv_w, lru_conv_b,
           lru_wa, lru_ba, lru_wx, lru_bx, lru_lambda, diff_lq1, diff_lk1, diff_lq2, diff_lk2, diff_norm,
           w_br_a, w_br_b, w_br_c, w_br_d, w_out, norm_ffn, moe_w_grp, moe_b_grp, moe_w_exp, moe_b_exp,
           moe_w_gate, moe_w_up, moe_w_down, norm_final):
    p = dict(norm_mix=norm_mix, w_in=w_in, ssm_lambda_re=ssm_lambda_re, ssm_lambda_im=ssm_lambda_im,
             ssm_log_dt=ssm_log_dt, ssm_b_re=ssm_b_re, ssm_b_im=ssm_b_im, ssm_c_re=ssm_c_re,
             ssm_c_im=ssm_c_im, ssm_d=ssm_d, ssm_w_glu=ssm_w_glu, hg_lb_logits=hg_lb_logits,
             hg_norm=hg_norm, lru_conv_w=lru_conv_w, lru_conv_b=lru_conv_b, lru_wa=lru_wa, lru_ba=lru_ba,
             lru_wx=lru_wx, lru_bx=lru_bx, lru_lambda=lru_lambda, diff_lq1=diff_lq1, diff_lk1=diff_lk1,
             diff_lq2=diff_lq2, diff_lk2=diff_lk2, diff_norm=diff_norm, w_br_a=w_br_a, w_br_b=w_br_b,
             w_br_c=w_br_c, w_br_d=w_br_d, w_out=w_out, norm_ffn=norm_ffn, moe_w_grp=moe_w_grp,
             moe_b_grp=moe_b_grp, moe_w_exp=moe_w_exp, moe_b_exp=moe_b_exp, moe_w_gate=moe_w_gate,
             moe_w_up=moe_w_up, moe_w_down=moe_w_down)
    layers = [_layer_params(l, p, BF16) for l in range(DEPTH)]
    layers_f32 = [_layer_params(l, p, F32) for l in range(DEPTH)]
    gf = norm_final.astype(F32).reshape(1, -1)
    Bp = x_prompt.shape[0]
    Bs = x_sample.shape[0]
    zero_states = [(jnp.zeros((Bp, SSM_GROUPS, SSM_STATE), F32), jnp.zeros((Bp, SSM_GROUPS, SSM_STATE), F32),
                    jnp.zeros((Bp, HG_HEADS, HG_HEAD_DIM, HG_HEAD_DIM), F32), jnp.zeros((Bp, LRU_WIDTH), F32),
                    jnp.zeros((Bp, CONV_WIDTH - 1, LRU_WIDTH), F32)) for _ in range(DEPTH)]
    y_p, k_p, v_p, st_p = _trunk(x_prompt, 0, zero_states, None, None, layers, gf, PROMPT_CFG)
    past_len = page_table.shape[1] * PAGE_SIZE
    sample_states = [(state_ssm_re[l], state_ssm_im[l], state_hgrn[l], state_lru[l], state_conv[l])
                     for l in range(DEPTH)]
    n_pool = cache_k.shape[1]
    rows = PAGE_SIZE * DA_HEADS
    cache = (cache_k.reshape(DEPTH, n_pool, rows, 2 * DA_HEAD_DIM), cache_v.reshape(DEPTH, n_pool, rows, DA_V_DIM))
    y_s, k_s, v_s, st_s = _trunk(x_sample, past_len, sample_states, cache, page_table, layers_f32, gf,
                                 SAMPLE_CFG)
    return (y_p, y_s, k_p, v_p, k_s, v_s,
            st_p[0], st_p[1], st_s[0], st_s[1], st_p[2], st_s[2], st_p[3], st_s[3], st_p[4], st_s[4])
```

```python
import functools
import math

import jax
import jax.numpy as jnp
from jax import lax
from jax.experimental import pallas as pl
from jax.experimental.pallas import tpu as pltpu

F32 = jnp.float32
BF16 = jnp.bfloat16

D_MODEL = 1024
DEPTH = 2
PAGE_SIZE = 128
SSM_WIDTH = 256
SSM_GROUP = 16
SSM_GROUPS = 16
SSM_STATE = 64
HG_WIDTH = 256
HG_HEAD_DIM = 64
HG_HEADS = 4
LRU_WIDTH = 256
LRU_BLOCKS = 4
LRU_BLOCK = 64
CONV_WIDTH = 4
LRU_C = 8.0
DA_HEADS = 4
DA_HEAD_DIM = 64
DA_V_DIM = 128
DA_QK_WIDTH = 512
DA_WIDTH = 512
ROPE_THETA = 10000.0
MASK_VALUE = -1e30
N_BRANCH = 4
MOE_GROUPS = 4
MOE_PER_GROUP = 8
MOE_EXPERTS = 32
MOE_HIDDEN = 128
NORM_EPS = 1e-6
MIX_COLS = 3328
ROUTER_LANES = 128
VMEM_LIMIT = 56 * 1024 * 1024
HI = lax.Precision.HIGHEST
Q_SCALE = DA_HEAD_DIM ** -0.5 * math.log2(math.e)
KV_UNIT = 256
SUBLANES = 8


def _cparams(sem):
    return pltpu.CompilerParams(dimension_semantics=sem, vmem_limit_bytes=VMEM_LIMIT)


def _rms(x, g):
    return x * lax.rsqrt(jnp.mean(x * x, axis=-1, keepdims=True) + NORM_EPS) * g


def _dot(a, b):
    return jnp.dot(a, b, preferred_element_type=F32)


def _mm(a, w, dims=(((1,), (0,)), ((), ()))):
    if w.dtype == BF16:
        return lax.dot_general(a.astype(BF16), w, dims, preferred_element_type=F32)
    return lax.dot_general(a.astype(F32), w, dims, preferred_element_type=F32, precision=HI)


NT_DIMS = (((1,), (1,)), ((), ()))
TN_DIMS = (((0,), (0,)), ((), ()))


def _dot_nt(a, b):
    return lax.dot_general(a, b, (((1,), (1,)), ((), ())), preferred_element_type=F32)


def _dot_tn(a, b):
    return lax.dot_general(a, b, (((0,), (0,)), ((), ())), preferred_element_type=F32)


def _split_dot(a, b_bf16, terms):
    out = None
    rem = a
    for _ in range(terms):
        piece = rem.astype(BF16)
        part = _dot(piece, b_bf16)
        out = part if out is None else out + part
        rem = rem - piece.astype(F32)
    return out


def _head_ones(width, head):
    r = lax.broadcasted_iota(jnp.int32, (width, width), 0) // head
    c = lax.broadcasted_iota(jnp.int32, (width, width), 1) // head
    return r == c


def _inproj_body(x_ref, g_ref, w_ref, cos_ref, sin_ref,
                 ua_ref, hg_ref, lru_ref, q_ref, k_ref, kb_ref, v_ref, vt_ref, uc_ref, ulo_scr, uhi_scr):
    h = _rms(x_ref[...], g_ref[...]).astype(w_ref.dtype)

    def mm(a, b):
        return _mm(h, w_ref[:, a:b])

    ua = mm(0, 256)
    ua_ref[...] = ua
    chunk = uc_ref.shape[1] // SSM_WIDTH
    half = SSM_WIDTH // 2
    ulo_scr[...] = ua[:, 0:half]
    uhi_scr[...] = ua[:, half:SSM_WIDTH]
    for t in range(chunk):
        rows = pl.ds(t, uc_ref.shape[0], stride=chunk)
        uc_ref[:, t * SSM_WIDTH:t * SSM_WIDTH + half] = ulo_scr[rows, :].astype(uc_ref.dtype)
        uc_ref[:, t * SSM_WIDTH + half:(t + 1) * SSM_WIDTH] = uhi_scr[rows, :].astype(uc_ref.dtype)
    hg_ref[...] = mm(256, 1280)
    lru_ref[...] = mm(1280, 1792)
    cos = cos_ref[...]
    sin = sin_ref[...]
    lane = lax.broadcasted_iota(jnp.int32, cos.shape, 1)
    first = (lane % DA_HEAD_DIM) < (DA_HEAD_DIM // 2)

    def rope(z):
        rot = jnp.where(first, -pltpu.roll(z, DA_QK_WIDTH - DA_HEAD_DIM // 2, 1),
                        pltpu.roll(z, DA_HEAD_DIM // 2, 1))
        return z * cos + rot * sin

    q = rope(mm(1792, 2304))
    q_ref[...] = (q * Q_SCALE).astype(q_ref.dtype)
    k = rope(mm(2304, 2816))
    kb_ref[...] = k.astype(BF16)
    v = mm(2816, 3328)
    tm = k.shape[0]
    unit = vt_ref.shape[2]
    for u in range(tm // unit):
        vt_ref[u] = v[u * unit:(u + 1) * unit, :].T.astype(BF16)
    for hd in range(DA_HEADS):
        cs = slice(hd * DA_V_DIM, (hd + 1) * DA_V_DIM)
        k_ref[pl.ds(hd, tm, stride=DA_HEADS), :] = k[:, cs]
        v_ref[pl.ds(hd, tm, stride=DA_HEADS), :] = v[:, cs]


def _inproj(x, g, w, cos, sin, tm, chunk):
    n = x.shape[0]
    ntab = cos.shape[0] // tm
    row = lambda i: (i, 0)
    fixed = lambda i: (0, 0)
    tab = lambda i: (i % ntab, 0)
    outs = ((1, 256, F32), (1, 1024, F32), (1, 512, F32), (1, 512, w.dtype),
            (DA_HEADS, DA_V_DIM, F32), (1, 512, BF16), (DA_HEADS, DA_V_DIM, F32))
    unit = min(tm, KV_UNIT)
    return pl.pallas_call(
        _inproj_body,
        grid=(n // tm,),
        in_specs=[pl.BlockSpec((tm, D_MODEL), row), pl.BlockSpec((1, D_MODEL), fixed),
                  pl.BlockSpec((D_MODEL, MIX_COLS), fixed),
                  pl.BlockSpec((tm, 512), tab), pl.BlockSpec((tm, 512), tab)],
        out_specs=[pl.BlockSpec((tm * r, wd), row) for r, wd, _ in outs]
                  + [pl.BlockSpec((tm // unit, DA_WIDTH, unit), lambda i: (i, 0, 0)),
                     pl.BlockSpec((tm // chunk, chunk * SSM_WIDTH), row)],
        out_shape=[jax.ShapeDtypeStruct((n * r, wd), dt) for r, wd, dt in outs]
                  + [jax.ShapeDtypeStruct((n // unit, DA_WIDTH, unit), BF16),
                     jax.ShapeDtypeStruct((n // chunk, chunk * SSM_WIDTH), w.dtype)],
        scratch_shapes=[pltpu.VMEM((tm, SSM_WIDTH // 2), F32), pltpu.VMEM((tm, SSM_WIDTH // 2), F32)],
        compiler_params=_cparams(("parallel",)),
        name="inproj",
    )(x, g, w, cos, sin)


S5_STATE_LANES = SSM_GROUPS * SSM_STATE
S5_HALF = S5_STATE_LANES // 2


def _s5_body(u_ref, k_ref, p_ref, q_ref, a_ref, h0_ref, y_ref, hf_ref, pu_scr, hs_scr, *, L, bt, nc):
    W = SSM_WIDTH
    HW = W // 2
    SL, SH = S5_STATE_LANES, S5_HALF
    for half in range(2):
        acc = None
        for t in range(L):
            c0 = t * W + half * HW
            part = _mm(u_ref[:, c0:c0 + HW], p_ref[t, half])
            acc = part if acc is None else acc + part
        pu_scr[:, half * SH:(half + 1) * SH] = acc[:, 0:SH]
        pu_scr[:, SL + half * SH:SL + (half + 1) * SH] = acc[:, SH:2 * SH]
    ar2 = a_ref[0:1, :]
    ai2 = a_ref[1:2, :]

    assert bt == 1 or nc == 1

    def step(c, h):
        rows = pl.ds(c * bt, bt)
        hs_scr[rows, :] = h
        return ar2 * h + ai2 * pltpu.roll(h, SL, 1) + pu_scr[rows, :]

    hf_ref[0] = lax.fori_loop(0, nc, step, h0_ref[0])
    hs = [jnp.concatenate([hs_scr[:, half * SH:(half + 1) * SH],
                           hs_scr[:, SL + half * SH:SL + (half + 1) * SH]], axis=1).astype(k_ref.dtype)
          for half in range(2)]
    for t2 in range(L):
        acc = jnp.concatenate([_mm(hs[0], q_ref[t2, 0]), _mm(hs[1], q_ref[t2, 1])], axis=1)
        for t in range(t2 + 1):
            acc = acc + _mm(u_ref[:, t * W:(t + 1) * W], k_ref[t2 - t])
        y_ref[:, t2 * W:(t2 + 1) * W] = acc


def _s5(ucat, kbd, pmat, qmat, a2, h0, bt, nc):
    rows_all, lw = ucat.shape
    L = lw // SSM_WIDTH
    rows = bt * nc
    once = lambda a: pl.BlockSpec(a.shape, lambda i: (0,) * a.ndim, pipeline_mode=pl.Buffered(1))
    return pl.pallas_call(
        functools.partial(_s5_body, L=L, bt=bt, nc=nc),
        grid=(rows_all // rows,),
        in_specs=[pl.BlockSpec((rows, lw), lambda i: (i, 0)), once(kbd), once(pmat), once(qmat), once(a2),
                  pl.BlockSpec((1, bt, 2 * S5_STATE_LANES), lambda i: (i, 0, 0))],
        out_specs=[pl.BlockSpec((rows, lw), lambda i: (i, 0)),
                   pl.BlockSpec((1, bt, 2 * S5_STATE_LANES), lambda i: (i, 0, 0))],
        out_shape=[jax.ShapeDtypeStruct((rows_all, lw), F32), jax.ShapeDtypeStruct(h0.shape, F32)],
        scratch_shapes=[pltpu.VMEM((rows, 2 * S5_STATE_LANES), F32), pltpu.VMEM((rows, 2 * S5_STATE_LANES), F32)],
        compiler_params=_cparams(("parallel",)),
        name="s5",
    )(ucat, kbd, pmat, qmat, a2, h0)


def _s5_weights(lam_re, lam_im, log_dt, b_re, b_im, c_re, c_im, L, wdtype):
    G, P, J = SSM_GROUPS, SSM_STATE, SSM_GROUP
    lr, li = lam_re.astype(F32), lam_im.astype(F32)
    dt = jnp.exp(log_dt.astype(F32))[:, None]
    mag = jnp.exp(lr * dt)
    ar = mag * jnp.cos(li * dt)
    ai = mag * jnp.sin(li * dt)
    den = lr * lr + li * li
    fr = ((ar - 1.0) * lr + ai * li) / den
    fi = (ai * lr - (ar - 1.0) * li) / den
    br, bi = b_re.astype(F32), b_im.astype(F32)
    bbr = fr[..., None] * br - fi[..., None] * bi
    bbi = fr[..., None] * bi + fi[..., None] * br
    tau = jnp.arange(L + 1, dtype=F32)[:, None, None]
    pmag = jnp.exp(lr * dt * tau)
    pr = pmag * jnp.cos(li * dt * tau)
    pi = pmag * jnp.sin(li * dt * tau)
    t1r = pr[..., None] * bbr - pi[..., None] * bbi
    t1i = pr[..., None] * bbi + pi[..., None] * bbr
    cr, ci = c_re.astype(F32), c_im.astype(F32)
    kt = (jnp.einsum('gip,tgpj->tgij', cr, t1r, precision=HI)
          - jnp.einsum('gip,tgpj->tgij', ci, t1i, precision=HI))
    GH = G // 2

    def block_diag(a, rows_per_group, cols_per_group, groups):
        r = lax.broadcasted_iota(jnp.int32, a.shape[-2:], 0) // rows_per_group
        c = lax.broadcasted_iota(jnp.int32, a.shape[-2:], 1) // cols_per_group
        return jnp.where(r % groups == c % groups, a, 0.0)

    kji = kt[:L].transpose(0, 1, 3, 2).reshape(L, G * J, 1, J)
    kbd = block_diag(jnp.broadcast_to(kji, (L, G * J, G, J)).reshape(L, G * J, G * J), J, J, G)
    rev = L - 1 - jnp.arange(L)
    ph = jnp.stack([t1r[rev], t1i[rev]], axis=1).reshape(L, 2, 2, GH, P, J)
    ph = ph.transpose(0, 2, 3, 5, 1, 4).reshape(L, 2, GH * J, 2, 1, P)
    pmat = block_diag(jnp.broadcast_to(ph, (L, 2, GH * J, 2, GH, P)).reshape(L, 2, GH * J, 2 * GH * P), J, P, GH)
    car = cr[None] * pr[1:, :, None, :] - ci[None] * pi[1:, :, None, :]
    cai = cr[None] * pi[1:, :, None, :] + ci[None] * pr[1:, :, None, :]
    qh = jnp.stack([car, -cai], axis=1).reshape(L, 2, 2, GH, J, P)
    qh = qh.transpose(0, 2, 1, 3, 5, 4).reshape(L, 2, 2 * GH * P, 1, J)
    qmat = block_diag(jnp.broadcast_to(qh, (L, 2, 2 * GH * P, GH, J)).reshape(L, 2, 2 * GH * P, GH * J), P, J, GH)
    a_l = jnp.stack([jnp.concatenate([pr[L].reshape(-1), pr[L].reshape(-1)]),
                     jnp.concatenate([-pi[L].reshape(-1), pi[L].reshape(-1)])])
    return kbd.astype(wdtype), pmat.astype(wdtype), qmat.astype(wdtype), a_l


def _s5_mixer(ucat, h0_re, h0_im, wts, bt, nc):
    B = h0_re.shape[0]
    h0 = jnp.concatenate([h0_re.reshape(B, -1), h0_im.reshape(B, -1)], axis=-1).astype(F32)
    y, hf = _s5(ucat, *wts, h0.reshape(B // bt, bt, -1), bt, nc)
    hf = hf.reshape(B, 2, SSM_GROUPS, SSM_STATE)
    return y, hf[:, 0], hf[:, 1]


def _hgrn_body(hg_ref, s0_ref, lb_ref, ng_ref, y_ref, sf_ref,
               st_scr, k_scr, b_scr, v_scr, w_scr, *, tb, c, t_valid, precise):
    j = pl.program_id(1)
    W = HG_WIDTH
    mdt = F32 if precise else BF16

    hd = HG_HEAD_DIM

    @pl.when(j == 0)
    def _():
        st_scr[...] = jnp.zeros((W, W), F32)
        for hh in range(HG_HEADS):
            st_scr[hh * hd:(hh + 1) * hd, hh * hd:(hh + 1) * hd] = s0_ref[0, hh]
        k_scr[0:c, :] = jnp.zeros((c, W), F32)
        b_scr[0:c, :] = jnp.zeros((c, W), F32)
        v_scr[0:c, :] = jnp.zeros((c, W), F32)

    q = hg_ref[0, :, 0:W]
    lb = lb_ref[...]
    fv = lb + (1.0 - lb) * jax.nn.sigmoid(hg_ref[0, :, W:2 * W])
    logf = jnp.log(fv) * math.log2(math.e)
    kk = 1.0 - fv
    v = hg_ref[0, :, 2 * W:3 * W]
    row = lax.broadcasted_iota(jnp.int32, (tb, W), 0)
    if t_valid < tb:
        valid = row < t_valid
        logf = jnp.where(valid, logf, 0.0)
        kk = jnp.where(valid, kk, 0.0)
    ri = lax.broadcasted_iota(jnp.int32, (tb, tb), 0)
    ci = lax.broadcasted_iota(jnp.int32, (tb, tb), 1)
    tril = ((ri // c == ci // c) & (ci <= ri)).astype(BF16)
    b = _split_dot_lhs_exact(tril, logf)
    k_scr[c:c + tb, :] = kk
    b_scr[c:c + tb, :] = b
    v_scr[c:c + tb, :] = v
    rin = row % c
    for d in range(c):
        ksh = k_scr[c - d:c - d + tb, :]
        bsh = b_scr[c - d:c - d + tb, :]
        w = jnp.where(rin >= d, q * ksh * jnp.exp2(b - bsh), 0.0)
        w_scr[d * tb:(d + 1) * tb, :] = w.astype(mdt)
    ones_bd = _head_ones(W, HG_HEAD_DIM).astype(BF16)
    att = _mm(w_scr[...], ones_bd.astype(mdt))
    o = att[0:tb] * v
    for d in range(1, c):
        o = o + att[d * tb:(d + 1) * tb] * v_scr[c - d:c - d + tb, :]
    bd = _head_ones(W, HG_HEAD_DIM)
    outs = []
    for ch in range(tb // c):
        sl = slice(ch * c, (ch + 1) * c)
        bc = b[sl]
        bl = bc[c - 1:c, :]
        st = st_scr[...]
        outs.append(_mm(q[sl] * jnp.exp2(bc), st.astype(mdt), NT_DIMS))
        khat = (kk[sl] * jnp.exp2(bl - bc)).astype(mdt)
        upd = _mm(v[sl], khat, TN_DIMS)
        st_scr[...] = st * jnp.exp2(bl) + jnp.where(bd, upd, 0.0)
    o = o + jnp.concatenate(outs, axis=0) if len(outs) > 1 else o + outs[0]
    ms = _split_dot(o * o, ones_bd, 3 if precise else 2) * (1.0 / HG_HEAD_DIM)
    y = o * lax.rsqrt(ms + NORM_EPS) * ng_ref[...]
    y_ref[0] = y * jax.nn.silu(hg_ref[0, :, 3 * W:4 * W])
    @pl.when(j == pl.num_programs(1) - 1)
    def _():
        for hh in range(HG_HEADS):
            sf_ref[0, hh] = st_scr[hh * hd:(hh + 1) * hd, hh * hd:(hh + 1) * hd]


def _split_dot_lhs_exact(a_bf16, b):
    out = None
    rem = b
    for _ in range(3):
        piece = rem.astype(BF16)
        part = _dot(a_bf16, piece)
        out = part if out is None else out + part
        rem = rem - piece.astype(F32)
    return out


def _hgrn(hg, s0t, lb, ng, tb, c, t_valid, precise):
    B, T, _ = hg.shape
    W = HG_WIDTH
    state = pl.BlockSpec((1, HG_HEADS, HG_HEAD_DIM, HG_HEAD_DIM), lambda b, j: (b, 0, 0, 0))
    return pl.pallas_call(
        functools.partial(_hgrn_body, tb=tb, c=c, t_valid=t_valid, precise=precise),
        grid=(B, T // tb),
        in_specs=[pl.BlockSpec((1, tb, 4 * W), lambda b, j: (b, j, 0)), state,
                  pl.BlockSpec((1, W), lambda b, j: (0, 0)),
                  pl.BlockSpec((1, W), lambda b, j: (0, 0))],
        out_specs=[pl.BlockSpec((1, tb, W), lambda b, j: (b, j, 0)), state],
        out_shape=[jax.ShapeDtypeStruct((B, T, W), F32), jax.ShapeDtypeStruct(s0t.shape, F32)],
        scratch_shapes=[pltpu.VMEM((W, W), F32), pltpu.VMEM((c + tb, W), F32),
                        pltpu.VMEM((c + tb, W), F32), pltpu.VMEM((c + tb, W), F32),
                        pltpu.VMEM((c * tb, W), F32 if precise else BF16)],
        compiler_params=_cparams(("parallel", "arbitrary")),
        name="hgrn2",
    )(hg, s0t, lb, ng)


def _hgrn_state_to_t(s):
    return jnp.swapaxes(s, -1, -2)


_hgrn_state_from_t = _hgrn_state_to_t


def _lru_body(x_ref, c0_ref, h0_ref, cw_ref, cb_ref, wax_ref, bax_ref, nsp_ref, y_ref, hl_ref,
              xs_scr, hc_scr, *, tb, r_last):
    j = pl.program_id(1)
    W = LRU_WIDTH

    @pl.when(j == 0)
    def _():
        xs_scr[0:8, :] = c0_ref[0]
        hc_scr[...] = h0_ref[0]

    x = x_ref[0, :, 0:W]
    xs_scr[8:8 + tb, :] = x
    xc = cb_ref[...] + x * cw_ref[3:4, :]
    for jj in range(CONV_WIDTH - 1):
        xc = xc + xs_scr[5 + jj:5 + jj + tb, :] * cw_ref[jj:jj + 1, :]
    tail = xs_scr[tb:tb + 8, :]
    xs_scr[0:8, :] = tail
    rg = _mm(xc, wax_ref[...]) + bax_ref[...]
    r = jax.nn.sigmoid(rg[:, 0:W])
    ig = jax.nn.sigmoid(rg[:, W:2 * W])
    log_a = nsp_ref[...] * r
    a = jnp.exp(log_a)
    bt = jnp.sqrt(jnp.maximum(1.0 - a * a, 0.0)) * (ig * xc)
    row = lax.broadcasted_iota(jnp.int32, (tb, W), 0) % SUBLANES
    k = 1
    while k < SUBLANES:
        keep = row >= k
        a_sh = jnp.where(keep, pltpu.roll(a, k, 0), 1.0)
        b_sh = jnp.where(keep, pltpu.roll(bt, k, 0), 0.0)
        bt = a * b_sh + bt
        a = a * a_sh
        k *= 2
    carry = hc_scr[...]
    groups = []
    for g0 in range(0, tb, SUBLANES):
        hg = a[g0:g0 + SUBLANES] * carry + bt[g0:g0 + SUBLANES]
        groups.append(hg)
        carry = hg[SUBLANES - 1:SUBLANES]
    h = jnp.concatenate(groups, axis=0)
    y_ref[0] = jax.nn.gelu(x_ref[0, :, W:2 * W]) * h
    hc = h[r_last:r_last + 1, :]
    hc_scr[...] = hc
    hl_ref[0] = hc


def _lru(xg, c0, h0, cw, cb, wax, bax, nsp, tb, t_valid):
    B, T, _ = xg.shape
    W = LRU_WIDTH
    fixed = lambda b, j: (0, 0)
    return pl.pallas_call(
        functools.partial(_lru_body, tb=tb, r_last=(t_valid - 1) % tb),
        grid=(B, T // tb),
        in_specs=[pl.BlockSpec((1, tb, 2 * W), lambda b, j: (b, j, 0)),
                  pl.BlockSpec((1, 8, W), lambda b, j: (b, 0, 0)),
                  pl.BlockSpec((1, 1, W), lambda b, j: (b, 0, 0)),
                  pl.BlockSpec((CONV_WIDTH, W), fixed), pl.BlockSpec((1, W), fixed),
                  pl.BlockSpec((W, 2 * W), fixed), pl.BlockSpec((1, 2 * W), fixed),
                  pl.BlockSpec((1, W), fixed)],
        out_specs=[pl.BlockSpec((1, tb, W), lambda b, j: (b, j, 0)),
                   pl.BlockSpec((1, 1, W), lambda b, j: (b, 0, 0))],
        out_shape=[jax.ShapeDtypeStruct((B, T, W), F32), jax.ShapeDtypeStruct((B, 1, W), F32)],
        scratch_shapes=[pltpu.VMEM((8 + tb, W), F32), pltpu.VMEM((1, W), F32)],
        compiler_params=_cparams(("parallel", "arbitrary")),
        name="rglru",
    )(xg, c0, h0, cw, cb, wax, bax, nsp)


def _attn_body(lam_ref, q_ref, k_ref, vt_ref, g_ref, o_ref, q2_scr, m_scr, l_scr, acc_scr,
               *, tq, wide_units, out_scale):
    qi = pl.program_id(1)
    hw = DA_V_DIM
    lane = lax.broadcasted_iota(jnp.int32, (tq, hw), 1)
    for h in range(DA_HEADS):
        qh = q_ref[0, :, h * hw:(h + 1) * hw]
        zero = jnp.zeros_like(qh)
        q2_scr[h, 0:tq, :] = jnp.where(lane < DA_HEAD_DIM, qh, zero)
        q2_scr[h, tq:2 * tq, :] = jnp.where(lane >= DA_HEAD_DIM, qh, zero)
    m_scr[...] = jnp.full(m_scr.shape, MASK_VALUE, F32)
    l_scr[...] = jnp.zeros(l_scr.shape, F32)
    acc_scr[...] = jnp.zeros(acc_scr.shape, F32)

    def block(u0, nu, diagonal):
        tk = nu * KV_UNIT
        r0 = pl.multiple_of(u0 * KV_UNIT, KV_UNIT)
        for h in range(DA_HEADS):
            cs = slice(h * hw, (h + 1) * hw)
            st = _dot_nt(k_ref[0, pl.ds(r0, tk), cs], q2_scr[h])
            if diagonal:
                keyi = lax.broadcasted_iota(jnp.int32, (tk, 2 * tq), 0)
                qryi = lax.broadcasted_iota(jnp.int32, (tk, 2 * tq), 1) % tq
                st = jnp.where(keyi <= qryi, st, MASK_VALUE)
            m = m_scr[h]
            m_new = jnp.maximum(m, jnp.max(st, axis=0, keepdims=True))
            alpha = jnp.exp2(m - m_new)
            p = jnp.exp2(st - m_new)
            l_scr[h] = alpha * l_scr[h] + jnp.sum(p, axis=0, keepdims=True)
            pb = p.astype(BF16)
            pv = _dot(vt_ref[u0, cs, :], pb[0:KV_UNIT])
            for u in range(1, nu):
                pv = pv + _dot(vt_ref[u0 + u, cs, :], pb[u * KV_UNIT:(u + 1) * KV_UNIT])
            acc_scr[h] = alpha * acc_scr[h] + pv
            m_scr[h] = m_new

    nq = tq // KV_UNIT
    n_before = qi * nq
    done = 0
    width = wide_units
    while width >= 1:
        n_blocks = (n_before - done) // width

        def step(jb, carry, width=width, done=done):
            block(done + jb * width, width, False)
            return carry

        lax.fori_loop(0, n_blocks, step, 0)
        done = done + n_blocks * width
        width //= 2
    block(n_before, nq, True)
    lam = lam_ref[0]
    for h in range(DA_HEADS):
        on = acc_scr[h] / l_scr[h]
        o = (on[:, 0:tq] - lam * on[:, tq:2 * tq]).T
        o_ref[0, :, h * hw:(h + 1) * hw] = _rms(o, g_ref[...]) * out_scale


def _attn(lam, q, k, vt, g, tq, wide_units, out_scale):
    B, T, Wd = q.shape
    hw = DA_V_DIM
    units = T // KV_UNIT
    return pl.pallas_call(
        functools.partial(_attn_body, tq=tq, wide_units=wide_units, out_scale=out_scale),
        grid=(B, T // tq),
        in_specs=[pl.BlockSpec(memory_space=pltpu.SMEM),
                  pl.BlockSpec((1, tq, Wd), lambda b, i: (b, i, 0)),
                  pl.BlockSpec((1, T, Wd), lambda b, i: (b, 0, 0)),
                  pl.BlockSpec((units, Wd, KV_UNIT), lambda b, i: (b, 0, 0)),
                  pl.BlockSpec((1, hw), lambda b, i: (0, 0))],
        out_specs=pl.BlockSpec((1, tq, Wd), lambda b, i: (b, i, 0)),
        out_shape=jax.ShapeDtypeStruct((B, T, Wd), F32),
        scratch_shapes=[pltpu.VMEM((DA_HEADS, 2 * tq, hw), BF16), pltpu.VMEM((DA_HEADS, 1, 2 * tq), F32),
                        pltpu.VMEM((DA_HEADS, 1, 2 * tq), F32), pltpu.VMEM((DA_HEADS, hw, 2 * tq), F32)],
        compiler_params=_cparams(("parallel", "arbitrary")),
        name="diff_attn",
    )(lam, q, k, vt, g)


def _dec_body(pt_ref, lam_ref, q_ref, kn_ref, vn_ref, g_ref, *rest, pp, t_new, out_scale):
    k_refs = rest[0:pp]
    v_refs = rest[pp:2 * pp]
    o_ref = rest[2 * pp]
    m_scr, l_scr, acc_scr, bias_scr = rest[2 * pp + 1:]
    j = pl.program_id(1)
    nrow = 2 * DA_HEADS * t_new

    @pl.when(j == 0)
    def _():
        m_scr[...] = jnp.full(m_scr.shape, MASK_VALUE, F32)
        l_scr[...] = jnp.zeros(l_scr.shape, F32)
        acc_scr[...] = jnp.zeros(acc_scr.shape, F32)
        ncol = PAGE_SIZE * DA_HEADS
        rh = (lax.broadcasted_iota(jnp.int32, (nrow, ncol), 0) // t_new) % DA_HEADS
        chd = lax.broadcasted_iota(jnp.int32, (nrow, ncol), 1) % DA_HEADS
        bias_scr[...] = jnp.where(rh == chd, 0.0, MASK_VALUE)

    def hi_lo(a):
        hi = a.astype(BF16)
        return jnp.concatenate([hi, (a - hi.astype(F32)).astype(BF16)], axis=0)

    def fold(a):
        return a[0:nrow] + a[nrow:2 * nrow]

    q = hi_lo(q_ref[0])

    def scores(keys):
        return fold(_dot_nt(q, keys.astype(BF16)))

    def update(ss, vals):
        m = m_scr[...]
        smax = ss[0]
        for s in ss[1:]:
            smax = jnp.maximum(smax, s)
        m_new = jnp.maximum(m, jnp.max(smax, axis=1, keepdims=True))
        alpha = jnp.exp2(m - m_new)
        ps = [jnp.exp2(s - m_new) for s in ss]
        psum = ps[0]
        for p in ps[1:]:
            psum = psum + p
        pv = _dot(hi_lo(ps[0]), vals[0])
        for p, vv in zip(ps[1:], vals[1:]):
            pv = pv + _dot(hi_lo(p), vv)
        l_scr[...] = alpha * l_scr[...] + jnp.sum(psum, axis=1, keepdims=True)
        acc_scr[...] = alpha * acc_scr[...] + fold(pv)
        m_scr[...] = m_new

    bias = bias_scr[...]
    update([scores(k_refs[i][...]) + bias for i in range(pp)],
           [v_refs[i][...].astype(BF16) for i in range(pp)])

    @pl.when(j == pl.num_programs(1) - 1)
    def _():
        nn = kn_ref.shape[1]
        r = lax.broadcasted_iota(jnp.int32, (nrow, nn), 0)
        cidx = lax.broadcasted_iota(jnp.int32, (nrow, nn), 1)
        ok = ((r // t_new) % DA_HEADS == cidx % DA_HEADS) & (cidx // DA_HEADS <= r % t_new)
        update([jnp.where(ok, scores(kn_ref[0]), MASK_VALUE)], [vn_ref[0].astype(BF16)])
        on = acc_scr[...] / l_scr[...]
        half = nrow // 2
        o = on[0:half] - lam_ref[0] * on[half:nrow]
        o_ref[0] = _rms(o, g_ref[...]) * out_scale


def _dec_attn(pt, lam, q2, kn, vn, g, ck, cv, layer, pp, t_new, out_scale):
    B, nrow, hw = q2.shape
    n_pages = pt.shape[0] // B
    nn = kn.shape[1]
    rows = PAGE_SIZE * DA_HEADS

    def page_spec(i):
        return pl.BlockSpec((None, None, rows, hw),
                            lambda b, j, pt_ref: (layer, pt_ref[b * n_pages + j * pp + i], 0, 0))

    grid_spec = pltpu.PrefetchScalarGridSpec(
        num_scalar_prefetch=1,
        grid=(B, n_pages // pp),
        in_specs=[pl.BlockSpec(memory_space=pltpu.SMEM),
                  pl.BlockSpec((1, nrow, hw), lambda b, j, pt_ref: (b, 0, 0)),
                  pl.BlockSpec((1, nn, hw), lambda b, j, pt_ref: (b, 0, 0)),
                  pl.BlockSpec((1, nn, hw), lambda b, j, pt_ref: (b, 0, 0)),
                  pl.BlockSpec((1, hw), lambda b, j, pt_ref: (0, 0))]
                 + [page_spec(i) for i in range(pp)] + [page_spec(i) for i in range(pp)],
        out_specs=pl.BlockSpec((1, nrow // 2, hw), lambda b, j, pt_ref: (b, 0, 0)),
        scratch_shapes=[pltpu.VMEM((nrow, 1), F32), pltpu.VMEM((nrow, 1), F32),
                        pltpu.VMEM((nrow, hw), F32), pltpu.VMEM((nrow, rows), F32)],
    )
    return pl.pallas_call(
        functools.partial(_dec_body, pp=pp, t_new=t_new, out_scale=out_scale),
        grid_spec=grid_spec,
        out_shape=jax.ShapeDtypeStruct((B, nrow // 2, hw), F32),
        compiler_params=_cparams(("parallel", "arbitrary")),
        name="paged_diff_attn",
    )(pt, lam, q2, kn, vn, g, *([ck] * pp), *([cv] * pp))


def _merge_body(x_ref, g_ref, ya_ref, ua_ref, yb_ref, yc_ref, yd_ref, d_ref, wglu_ref,
                wgt_ref, wa_ref, wb_ref, wc_ref, wd_ref, wo_ref, o_ref, ylo_scr, yhi_scr):
    x = x_ref[...]
    h = _rms(x, g_ref[...]).astype(wgt_ref.dtype)
    chunk = ya_ref.shape[1] // SSM_WIDTH
    half = SSM_WIDTH // 2
    for t in range(chunk):
        rows = pl.ds(t, ya_ref.shape[0], stride=chunk)
        ylo_scr[rows, :] = ya_ref[:, t * SSM_WIDTH:t * SSM_WIDTH + half]
        yhi_scr[rows, :] = ya_ref[:, t * SSM_WIDTH + half:(t + 1) * SSM_WIDTH]
    ya_tok = jnp.concatenate([ylo_scr[...], yhi_scr[...]], axis=1)
    z = jax.nn.gelu(ya_tok + d_ref[...] * ua_ref[...])
    ya = z * jax.nn.sigmoid(_mm(z, wglu_ref[...]))
    merged = None
    branches = ((ya, wa_ref), (yb_ref[...], wb_ref), (yc_ref[...], wc_ref), (yd_ref[...], wd_ref))
    for i, (yv, w_ref) in enumerate(branches):
        gate = jax.nn.sigmoid(_mm(h, wgt_ref[:, i * D_MODEL:(i + 1) * D_MODEL]))
        term = gate * _mm(yv, w_ref[...])
        merged = term if merged is None else merged + term
    o_ref[...] = x + _mm(merged, wo_ref[...])


def _merge(x, g, ya, ua, yb, yc, yd, d, wglu, wgt, wa, wb, wc, wd, wo, tm):
    n = x.shape[0]
    row = lambda i: (i, 0)
    fixed = lambda i: (0, 0)
    full = lambda a: pl.BlockSpec(a.shape, fixed, pipeline_mode=pl.Buffered(1))
    return pl.pallas_call(
        _merge_body,
        grid=(n // tm,),
        in_specs=[pl.BlockSpec((tm, D_MODEL), row), full(g),
                  pl.BlockSpec((tm * ya.shape[0] // n, ya.shape[1]), row), pl.BlockSpec((tm, 256), row),
                  pl.BlockSpec((tm, 256), row), pl.BlockSpec((tm, 256), row),
                  pl.BlockSpec((tm, 512), row),
                  full(d), full(wglu), full(wgt), full(wa), full(wb), full(wc), full(wd), full(wo)],
        out_specs=pl.BlockSpec((tm, D_MODEL), row),
        out_shape=jax.ShapeDtypeStruct((n, D_MODEL), F32),
        scratch_shapes=[pltpu.VMEM((tm, SSM_WIDTH // 2), F32), pltpu.VMEM((tm, SSM_WIDTH // 2), F32)],
        compiler_params=_cparams(("parallel",)),
        name="merge",
    )(x, g, ya, ua, yb, yc, yd, d, wglu, wgt, wa, wb, wc, wd, wo)


def _moe_body(x_ref, g_ref, wr_ref, br_ref, wg_ref, wu_ref, wd_ref, gf_ref, o_ref,
              h_scr, gate_scr, acc_scr, *, final_norm):
    gi = pl.program_id(1)
    tm = x_ref.shape[0]
    R = ROUTER_LANES

    @pl.when(gi == 0)
    def _():
        h = _rms(x_ref[...], g_ref[...]).astype(h_scr.dtype)
        h_scr[...] = h
        logits = _mm(h, wr_ref[...]) + br_ref[...]
        lane = lax.broadcasted_iota(jnp.int32, (tm, R), 1)
        lanef = lane.astype(F32)
        neg = -jnp.inf
        is_g = lane < MOE_GROUPS
        glm = jnp.where(is_g, logits, neg)
        gmax = jnp.max(glm, axis=1, keepdims=True)
        gsum = jnp.sum(jnp.where(is_g, jnp.exp(glm - gmax), 0.0), axis=1, keepdims=True)
        g_w = 1.0 / gsum
        g_i = jnp.min(jnp.where(glm == gmax, lanef, float(R)), axis=1, keepdims=True)
        e_grp = ((lane - MOE_GROUPS) // MOE_PER_GROUP).astype(F32)
        sel = (lane >= MOE_GROUPS) & (lane < MOE_GROUPS + MOE_EXPERTS) & (e_grp == g_i)
        elm = jnp.where(sel, logits, neg)
        e1 = jnp.max(elm, axis=1, keepdims=True)
        i1 = jnp.min(jnp.where(elm == e1, lanef, float(R)), axis=1, keepdims=True)
        elm2 = jnp.where(lanef == i1, neg, elm)
        e2 = jnp.max(elm2, axis=1, keepdims=True)
        i2 = jnp.min(jnp.where(elm2 == e2, lanef, float(R)), axis=1, keepdims=True)
        t = jnp.exp(e2 - e1)
        w1 = g_w / (1.0 + t)
        w2 = g_w * t / (1.0 + t)
        gates = jnp.where(lanef == i1, w1, 0.0) + jnp.where(lanef == i2, w2, 0.0)
        for grp in range(MOE_GROUPS):
            gate_scr[grp] = pltpu.roll(gates, R - MOE_GROUPS - MOE_PER_GROUP * grp, 1)
        acc_scr[...] = jnp.zeros(acc_scr.shape, F32)

    h = h_scr[...]
    gates = gate_scr[gi]
    if wg_ref.ndim == 3:
        out = None
        for e in range(MOE_PER_GROUP):
            hid = jax.nn.silu(_mm(h, wg_ref[e])) * _mm(h, wu_ref[e]) * gates[:, e:e + 1]
            part = _mm(hid, wd_ref[e])
            out = part if out is None else out + part
        acc_scr[...] += out
    else:
        hid = jax.nn.silu(_mm(h, wg_ref[...])) * _mm(h, wu_ref[...])
        hid = jnp.concatenate([hid[:, e * MOE_HIDDEN:(e + 1) * MOE_HIDDEN] * gates[:, e:e + 1]
                               for e in range(MOE_PER_GROUP)], axis=1)
        acc_scr[...] += _mm(hid, wd_ref[...])

    @pl.when(gi == MOE_GROUPS - 1)
    def _():
        o = x_ref[...] + acc_scr[...]
        if final_norm:
            o = _rms(o, gf_ref[...])
        o_ref[...] = o


def _moe(x, g, wr, br, wg, wu, wd, gf, tm, final_norm, layer):
    n = x.shape[0]
    gw = MOE_PER_GROUP * MOE_HIDDEN
    row = lambda i, e: (i, 0)
    fixed = lambda i, e: (0, 0)
    if wg.ndim == 3:
        per_expert = lambda i, e: (layer * MOE_GROUPS + e, 0, 0)
        w_specs = [pl.BlockSpec((MOE_PER_GROUP, D_MODEL, MOE_HIDDEN), per_expert),
                   pl.BlockSpec((MOE_PER_GROUP, D_MODEL, MOE_HIDDEN), per_expert),
                   pl.BlockSpec((MOE_PER_GROUP, MOE_HIDDEN, D_MODEL), per_expert)]
    else:
        w_specs = [pl.BlockSpec((D_MODEL, gw), lambda i, e: (0, e)), pl.BlockSpec((D_MODEL, gw), lambda i, e: (0, e)),
                   pl.BlockSpec((gw, D_MODEL), lambda i, e: (e, 0))]
    return pl.pallas_call(
        functools.partial(_moe_body, final_norm=final_norm),
        grid=(n // tm, MOE_GROUPS),
        in_specs=[pl.BlockSpec((tm, D_MODEL), row), pl.BlockSpec((1, D_MODEL), fixed),
                  pl.BlockSpec((D_MODEL, ROUTER_LANES), fixed), pl.BlockSpec((1, ROUTER_LANES), fixed),
                  *w_specs,
                  pl.BlockSpec((1, D_MODEL), fixed)],
        out_specs=pl.BlockSpec((tm, D_MODEL), row),
        out_shape=jax.ShapeDtypeStruct((n, D_MODEL), F32),
        scratch_shapes=[pltpu.VMEM((tm, D_MODEL), wg.dtype), pltpu.VMEM((MOE_GROUPS, tm, ROUTER_LANES), F32),
                        pltpu.VMEM((tm, D_MODEL), F32)],
        compiler_params=_cparams(("parallel", "arbitrary")),
        name="moe",
    )(x, g, wr, br, wg, wu, wd, gf)


def _rope_tables(pos):
    half = DA_HEAD_DIM // 2
    inv = 1.0 / (ROPE_THETA ** (jnp.arange(half, dtype=F32) * 2.0 / DA_HEAD_DIM))
    ang = pos.astype(F32)[:, None] * inv[None, :]
    reps = DA_QK_WIDTH // half
    return jnp.tile(jnp.cos(ang), (1, reps)), jnp.tile(jnp.sin(ang), (1, reps))


def _block_diag(w):
    n, a, b = w.shape
    eye = jnp.eye(n, dtype=w.dtype)
    return jnp.einsum('nij,nm->nimj', w, eye).reshape(n * a, n * b)


def _layer_params(l, p, wdt):
    row = lambda a: a.astype(F32).reshape(1, -1)
    w_in = p['w_in'][l]
    p_lb = jax.nn.softmax(p['hg_lb_logits'].astype(F32), axis=0)
    lb = jnp.cumsum(p_lb, axis=0)[l] - p_lb[0]
    lam_init = 0.8 - 0.6 * math.exp(-0.3 * l)
    lam = (jnp.exp(jnp.sum(p['diff_lq1'][l].astype(F32) * p['diff_lk1'][l].astype(F32)))
           - jnp.exp(jnp.sum(p['diff_lq2'][l].astype(F32) * p['diff_lk2'][l].astype(F32))) + lam_init)
    w_router = jnp.concatenate(
        [p['moe_w_grp'][l], p['moe_w_exp'][l],
         jnp.zeros((D_MODEL, ROUTER_LANES - MOE_GROUPS - MOE_EXPERTS), F32)], axis=1)
    b_router = jnp.concatenate(
        [p['moe_b_grp'][l].astype(F32), p['moe_b_exp'][l].astype(F32),
         jnp.zeros((ROUTER_LANES - MOE_GROUPS - MOE_EXPERTS,), F32)]).reshape(1, -1)
    eh = MOE_EXPERTS * MOE_HIDDEN
    return dict(
        norm_mix=row(p['norm_mix'][l]),
        wdt=wdt,
        w_mix=w_in[:, :MIX_COLS].astype(wdt),
        w_gates=w_in[:, MIX_COLS:].astype(wdt),
        s5=(p['ssm_lambda_re'][l], p['ssm_lambda_im'][l], p['ssm_log_dt'][l], p['ssm_b_re'][l],
            p['ssm_b_im'][l], p['ssm_c_re'][l], p['ssm_c_im'][l]),
        ssm_d=row(p['ssm_d'][l]),
        w_glu=p['ssm_w_glu'][l].astype(wdt),
        hg_lb=lb.reshape(1, -1),
        hg_norm=jnp.tile(p['hg_norm'][l].astype(F32), HG_HEADS).reshape(1, -1),
        conv_w=p['lru_conv_w'][l].astype(F32),
        conv_b=row(p['lru_conv_b'][l]),
        w_ax=jnp.concatenate([_block_diag(p['lru_wa'][l]), _block_diag(p['lru_wx'][l])], axis=1).astype(wdt),
        b_ax=jnp.concatenate([p['lru_ba'][l], p['lru_bx'][l]]).astype(F32).reshape(1, -1),
        neg_c_softplus=(-LRU_C * jax.nn.softplus(-p['lru_lambda'][l].astype(F32))).reshape(1, -1),
        lam=lam.reshape(1).astype(F32),
        out_scale=1.0 - lam_init,
        diff_norm=row(p['diff_norm'][l]),
        w_br_a=p['w_br_a'][l].astype(wdt), w_br_b=p['w_br_b'][l].astype(wdt),
        w_br_c=p['w_br_c'][l].astype(wdt), w_br_d=p['w_br_d'][l].astype(wdt),
        w_out=p['w_out'][l].astype(wdt),
        norm_ffn=row(p['norm_ffn'][l]),
        w_router=w_router.astype(wdt), b_router=b_router,
        **(dict(moe_gate=p['moe_w_gate'].reshape(-1, D_MODEL, MOE_HIDDEN),
                moe_up=p['moe_w_up'].reshape(-1, D_MODEL, MOE_HIDDEN),
                moe_down=p['moe_w_down'].reshape(-1, MOE_HIDDEN, D_MODEL)) if wdt == F32 else
           dict(moe_gate=p['moe_w_gate'][l].astype(wdt).transpose(1, 0, 2).reshape(D_MODEL, eh),
                moe_up=p['moe_w_up'][l].astype(wdt).transpose(1, 0, 2).reshape(D_MODEL, eh),
                moe_down=p['moe_w_down'][l].astype(wdt).reshape(eh, D_MODEL))),
    )


def _pad_rows(a, rows):
    return jnp.pad(a, ((0, 0), (0, rows - a.shape[1])) + ((0, 0),) * (a.ndim - 2))


def _trunk(x, pos0, states, cache, page_table, layers, norm_final, cfg):
    B, T, _ = x.shape
    n = B * T
    tm, s5_chunk, tpad, hg_tb, hg_c, lru_tb, tq = (cfg[k] for k in
                                                   ('tm', 's5_chunk', 'tpad', 'hg_tb', 'hg_c', 'lru_tb', 'tq'))
    cos, sin = _rope_tables(pos0 + jnp.arange(T, dtype=jnp.int32))
    if T % tm:
        cos, sin = jnp.tile(cos, (tm // T, 1)), jnp.tile(sin, (tm // T, 1))
    xf = x.reshape(n, D_MODEL)
    ks, vs, sts = [], [], []
    for l, lp in enumerate(layers):
        st = states[l]
        ua, hg, xg, qb, k, kb, v, vt, ucat = _inproj(xf, lp['norm_mix'], lp['w_mix'], cos, sin, tm, s5_chunk)
        ya, ssm_re, ssm_im = _s5_mixer(ucat, st[0], st[1], _s5_weights(*lp['s5'], s5_chunk, lp['wdt']),
                                       cfg['s5_bt'], T // s5_chunk)
        hg3 = _pad_rows(hg.reshape(B, T, 4 * HG_WIDTH), tpad)
        yb, s_t = _hgrn(hg3, _hgrn_state_to_t(st[2].astype(F32)), lp['hg_lb'], lp['hg_norm'],
                        hg_tb, hg_c, T, lp['wdt'] == F32)
        hg_state = _hgrn_state_from_t(s_t)
        xg3 = xg.reshape(B, T, 2 * LRU_WIDTH)
        conv0 = jnp.pad(st[4].astype(F32), ((0, 0), (8 - (CONV_WIDTH - 1), 0), (0, 0)))
        yc, lru_h = _lru(_pad_rows(xg3, tpad), conv0, st[3].astype(F32).reshape(B, 1, LRU_WIDTH),
                         lp['conv_w'], lp['conv_b'], lp['w_ax'], lp['b_ax'], lp['neg_c_softplus'],
                         lru_tb, T)
        xp = jnp.concatenate([st[4].astype(F32), xg3[:, :, :LRU_WIDTH]], axis=1)
        conv_buf = xp[:, T:]
        if cache is None:
            yd = _attn(lp['lam'], qb.reshape(B, T, -1), kb.reshape(B, T, -1), vt,
                       lp['diff_norm'], tq, cfg['wide_units'], lp['out_scale'])
        else:
            hw = DA_V_DIM
            q4 = qb.reshape(B, T, DA_HEADS, 2, DA_HEAD_DIM)
            zero = jnp.zeros_like(q4[:, :, :, 0])
            q2 = jnp.stack([jnp.concatenate([q4[:, :, :, 0], zero], -1),
                            jnp.concatenate([zero, q4[:, :, :, 1]], -1)], axis=1)
            q2 = q2.transpose(0, 1, 3, 2, 4).reshape(B, 2 * DA_HEADS * T, hw)
            nn = 128
            kn = _pad_rows(k.reshape(B, T * DA_HEADS, hw), nn)
            vn = _pad_rows(v.reshape(B, T * DA_HEADS, hw), nn)
            o = _dec_attn(page_table.reshape(-1), lp['lam'], q2, kn, vn, lp['diff_norm'],
                          cache[0], cache[1], l, cfg['pp'], T, lp['out_scale'])
            yd = o.reshape(B, DA_HEADS, T, hw).transpose(0, 2, 1, 3).reshape(B, T, DA_WIDTH)
        x1 = _merge(xf, lp['norm_mix'], ya, ua, yb[:, :T].reshape(n, -1),
                    yc[:, :T].reshape(n, -1), yd.reshape(n, -1), lp['ssm_d'], lp['w_glu'], lp['w_gates'],
                    lp['w_br_a'], lp['w_br_b'], lp['w_br_c'], lp['w_br_d'], lp['w_out'], tm)
        xf = _moe(x1, lp['norm_ffn'], lp['w_router'], lp['b_router'], lp['moe_gate'], lp['moe_up'],
                  lp['moe_down'], norm_final, cfg['tm_moe'], l == len(layers) - 1, l)
        ks.append(k.reshape(B, T, DA_HEADS, 2 * DA_HEAD_DIM))
        vs.append(v.reshape(B, T, DA_HEADS, DA_V_DIM))
        sts.append((ssm_re, ssm_im, hg_state, lru_h.reshape(B, LRU_WIDTH), conv_buf))
    stacked = [jnp.stack([s[j] for s in sts]) for j in range(5)]
    return xf.reshape(B, T, D_MODEL), jnp.stack(ks), jnp.stack(vs), stacked


PROMPT_CFG = dict(tm=512, tm_moe=1024, s5_chunk=16, s5_bt=1, tpad=2048, hg_tb=256, hg_c=16, lru_tb=512, tq=256,
                  wide_units=4)
SAMPLE_CFG = dict(tm=128, tm_moe=128, s5_chunk=4, s5_bt=32, tpad=16, hg_tb=16, hg_c=16, lru_tb=16, tq=0, pp=16)


def kernel(x_prompt, x_sample, cache_k, cache_v, page_table, state_ssm_re, state_ssm_im, state_hgrn,
           state_lru, state_conv, norm_mix, w_in, ssm_lambda_re, ssm_lambda_im, ssm_log_dt, ssm_b_re,
           ssm_b_im, ssm_c_re, ssm_c_im, ssm_d, ssm_w_glu, hg_lb_logits, hg_norm, lru_conv_w, lru_conv_b,
           lru_wa, lru_ba, lru_wx, lru_bx, lru_lambda, diff_lq1, diff_lk1, diff_lq2, diff_lk2, diff_norm,
           w_br_a, w_br_b, w_br_c, w_br_d, w_out, norm_ffn, moe_w_grp, moe_b_grp, moe_w_exp, moe_b_exp,
           moe_w_gate, moe_w_up, moe_w_down, norm_final):
    p = dict(norm_mix=norm_mix, w_in=w_in, ssm_lambda_re=ssm_lambda_re, ssm_lambda_im=ssm_lambda_im,
             ssm_log_dt=ssm_log_dt, ssm_b_re=ssm_b_re, ssm_b_im=ssm_b_im, ssm_c_re=ssm_c_re,
             ssm_c_im=ssm_c_im, ssm_d=ssm_d, ssm_w_glu=ssm_w_glu, hg_lb_logits=hg_lb_logits,
             hg_norm=hg_norm, lru_conv_w=lru_conv_w, lru_conv_b=lru_conv_b, lru_wa=lru_wa, lru_ba=lru_ba,
             lru_wx=lru_wx, lru_bx=lru_bx, lru_lambda=lru_lambda, diff_lq1=diff_lq1, diff_lk1=diff_lk1,
             diff_lq2=diff_lq2, diff_lk2=diff_lk2, diff_norm=diff_norm, w_br_a=w_br_a, w_br_b=w_br_b,
             w_br_c=w_br_c, w_br_d=w_br_d, w_out=w_out, norm_ffn=norm_ffn, moe_w_grp=moe_w_grp,
             moe_b_grp=moe_b_grp, moe_w_exp=moe_w_exp, moe_b_exp=moe_b_exp, moe_w_gate=moe_w_gate,
             moe_w_up=moe_w_up, moe_w_down=moe_w_down)
    layers = [_layer_params(l, p, BF16) for l in range(DEPTH)]
    layers_f32 = [_layer_params(l, p, F32) for l in range(DEPTH)]
    gf = norm_final.astype(F32).reshape(1, -1)
    Bp = x_prompt.shape[0]
    Bs = x_sample.shape[0]
    zero_states = [(jnp.zeros((Bp, SSM_GROUPS, SSM_STATE), F32), jnp.zeros((Bp, SSM_GROUPS, SSM_STATE), F32),
                    jnp.zeros((Bp, HG_HEADS, HG_HEAD_DIM, HG_HEAD_DIM), F32), jnp.zeros((Bp, LRU_WIDTH), F32),
                    jnp.zeros((Bp, CONV_WIDTH - 1, LRU_WIDTH), F32)) for _ in range(DEPTH)]
    y_p, k_p, v_p, st_p = _trunk(x_prompt, 0, zero_states, None, None, layers, gf, PROMPT_CFG)
    past_len = page_table.shape[1] * PAGE_SIZE
    sample_states = [(state_ssm_re[l], state_ssm_im[l], state_hgrn[l], state_lru[l], state_conv[l])
                     for l in range(DEPTH)]
    n_pool = cache_k.shape[1]
    rows = PAGE_SIZE * DA_HEADS
    cache = (cache_k.reshape(DEPTH, n_pool, rows, 2 * DA_HEAD_DIM), cache_v.reshape(DEPTH, n_pool, rows, DA_V_DIM))
    y_s, k_s, v_s, st_s = _trunk(x_sample, past_len, sample_states, cache, page_table, layers_f32, gf,
                                 SAMPLE_CFG)
    return (y_p, y_s, k_p, v_p, k_s, v_s,
            st_p[0], st_p[1], st_s[0], st_s[1], st_p[2], st_s[2], st_p[3], st_s[3], st_p[4], st_s[4])
```

```python
import functools
import math

import jax
import jax.numpy as jnp
from jax import lax
from jax.experimental import pallas as pl
from jax.experimental.pallas import tpu as pltpu

F32 = jnp.float32
BF16 = jnp.bfloat16

D_MODEL = 1024
DEPTH = 2
PAGE_SIZE = 128
SSM_WIDTH = 256
SSM_GROUP = 16
SSM_GROUPS = 16
SSM_STATE = 64
HG_WIDTH = 256
HG_HEAD_DIM = 64
HG_HEADS = 4
LRU_WIDTH = 256
LRU_BLOCKS = 4
LRU_BLOCK = 64
CONV_WIDTH = 4
LRU_C = 8.0
DA_HEADS = 4
DA_HEAD_DIM = 64
DA_V_DIM = 128
DA_QK_WIDTH = 512
DA_WIDTH = 512
ROPE_THETA = 10000.0
MASK_VALUE = -1e30
N_BRANCH = 4
MOE_GROUPS = 4
MOE_PER_GROUP = 8
MOE_EXPERTS = 32
MOE_HIDDEN = 128
NORM_EPS = 1e-6
MIX_COLS = 3328
ROUTER_LANES = 128
VMEM_LIMIT = 56 * 1024 * 1024
HI = lax.Precision.HIGHEST
Q_SCALE = DA_HEAD_DIM ** -0.5 * math.log2(math.e)
KV_UNIT = 256
SUBLANES = 8


def _cparams(sem):
    return pltpu.CompilerParams(dimension_semantics=sem, vmem_limit_bytes=VMEM_LIMIT)


def _rms(x, g):
    return x * lax.rsqrt(jnp.mean(x * x, axis=-1, keepdims=True) + NORM_EPS) * g


def _dot(a, b):
    return jnp.dot(a, b, preferred_element_type=F32)


def _mm(a, w, dims=(((1,), (0,)), ((), ()))):
    if w.dtype == BF16:
        return lax.dot_general(a.astype(BF16), w, dims, preferred_element_type=F32)
    return lax.dot_general(a.astype(F32), w, dims, preferred_element_type=F32, precision=HI)


NT_DIMS = (((1,), (1,)), ((), ()))
TN_DIMS = (((0,), (0,)), ((), ()))


def _dot_nt(a, b):
    return lax.dot_general(a, b, (((1,), (1,)), ((), ())), preferred_element_type=F32)


def _dot_tn(a, b):
    return lax.dot_general(a, b, (((0,), (0,)), ((), ())), preferred_element_type=F32)


def _split_dot(a, b_bf16, terms):
    out = None
    rem = a
    for _ in range(terms):
        piece = rem.astype(BF16)
        part = _dot(piece, b_bf16)
        out = part if out is None else out + part
        rem = rem - piece.astype(F32)
    return out


def _head_ones(width, head):
    r = lax.broadcasted_iota(jnp.int32, (width, width), 0) // head
    c = lax.broadcasted_iota(jnp.int32, (width, width), 1) // head
    return r == c


def _inproj_body(x_ref, g_ref, w_ref, cos_ref, sin_ref,
                 ua_ref, hg_ref, lru_ref, q_ref, k_ref, kb_ref, v_ref, vt_ref, uc_ref, ulo_scr, uhi_scr):
    h = _rms(x_ref[...], g_ref[...]).astype(w_ref.dtype)

    def mm(a, b):
        return _mm(h, w_ref[:, a:b])

    ua = mm(0, 256)
    ua_ref[...] = ua
    chunk = uc_ref.shape[1] // SSM_WIDTH
    half = SSM_WIDTH // 2
    ulo_scr[...] = ua[:, 0:half]
    uhi_scr[...] = ua[:, half:SSM_WIDTH]
    for t in range(chunk):
        rows = pl.ds(t, uc_ref.shape[0], stride=chunk)
        uc_ref[:, t * SSM_WIDTH:t * SSM_WIDTH + half] = ulo_scr[rows, :].astype(uc_ref.dtype)
        uc_ref[:, t * SSM_WIDTH + half:(t + 1) * SSM_WIDTH] = uhi_scr[rows, :].astype(uc_ref.dtype)
    hg_ref[...] = mm(256, 1280)
    lru_ref[...] = mm(1280, 1792)
    cos = cos_ref[...]
    sin = sin_ref[...]
    lane = lax.broadcasted_iota(jnp.int32, cos.shape, 1)
    first = (lane % DA_HEAD_DIM) < (DA_HEAD_DIM // 2)

    def rope(z):
        rot = jnp.where(first, -pltpu.roll(z, DA_QK_WIDTH - DA_HEAD_DIM // 2, 1),
                        pltpu.roll(z, DA_HEAD_DIM // 2, 1))
        return z * cos + rot * sin

    q = rope(mm(1792, 2304))
    q_ref[...] = (q * Q_SCALE).astype(q_ref.dtype)
    k = rope(mm(2304, 2816))
    kb_ref[...] = k.astype(BF16)
    v = mm(2816, 3328)
    tm = k.shape[0]
    unit = vt_ref.shape[2]
    for u in range(tm // unit):
        vt_ref[u] = v[u * unit:(u + 1) * unit, :].T.astype(BF16)
    for hd in range(DA_HEADS):
        cs = slice(hd * DA_V_DIM, (hd + 1) * DA_V_DIM)
        k_ref[pl.ds(hd, tm, stride=DA_HEADS), :] = k[:, cs]
        v_ref[pl.ds(hd, tm, stride=DA_HEADS), :] = v[:, cs]


def _inproj(x, g, w, cos, sin, tm, chunk):
    n = x.shape[0]
    ntab = cos.shape[0] // tm
    row = lambda i: (i, 0)
    fixed = lambda i: (0, 0)
    tab = lambda i: (i % ntab, 0)
    outs = ((1, 256, F32), (1, 1024, F32), (1, 512, F32), (1, 512, w.dtype),
            (DA_HEADS, DA_V_DIM, F32), (1, 512, BF16), (DA_HEADS, DA_V_DIM, F32))
    unit = min(tm, KV_UNIT)
    return pl.pallas_call(
        _inproj_body,
        grid=(n // tm,),
        in_specs=[pl.BlockSpec((tm, D_MODEL), row), pl.BlockSpec((1, D_MODEL), fixed),
                  pl.BlockSpec((D_MODEL, MIX_COLS), fixed),
                  pl.BlockSpec((tm, 512), tab), pl.BlockSpec((tm, 512), tab)],
        out_specs=[pl.BlockSpec((tm * r, wd), row) for r, wd, _ in outs]
                  + [pl.BlockSpec((tm // unit, DA_WIDTH, unit), lambda i: (i, 0, 0)),
                     pl.BlockSpec((tm // chunk, chunk * SSM_WIDTH), row)],
        out_shape=[jax.ShapeDtypeStruct((n * r, wd), dt) for r, wd, dt in outs]
                  + [jax.ShapeDtypeStruct((n // unit, DA_WIDTH, unit), BF16),
                     jax.ShapeDtypeStruct((n // chunk, chunk * SSM_WIDTH), w.dtype)],
        scratch_shapes=[pltpu.VMEM((tm, SSM_WIDTH // 2), F32), pltpu.VMEM((tm, SSM_WIDTH // 2), F32)],
        compiler_params=_cparams(("parallel",)),
        name="inproj",
    )(x, g, w, cos, sin)


S5_STATE_LANES = SSM_GROUPS * SSM_STATE
S5_HALF = S5_STATE_LANES // 2


def _s5_body(u_ref, k_ref, p_ref, q_ref, a_ref, h0_ref, y_ref, hf_ref, pu_scr, hs_scr, *, L, bt, nc):
    W = SSM_WIDTH
    HW = W // 2
    SL, SH = S5_STATE_LANES, S5_HALF
    for half in range(2):
        acc = None
        for t in range(L):
            c0 = t * W + half * HW
            part = _mm(u_ref[:, c0:c0 + HW], p_ref[t, half])
            acc = part if acc is None else acc + part
        pu_scr[:, half * SH:(half + 1) * SH] = acc[:, 0:SH]
        pu_scr[:, SL + half * SH:SL + (half + 1) * SH] = acc[:, SH:2 * SH]
    ar2 = a_ref[0:1, :]
    ai2 = a_ref[1:2, :]

    assert bt == 1 or nc == 1

    def step(c, h):
        rows = pl.ds(c * bt, bt)
        hs_scr[rows, :] = h
        return ar2 * h + ai2 * pltpu.roll(h, SL, 1) + pu_scr[rows, :]

    hf_ref[0] = lax.fori_loop(0, nc, step, h0_ref[0])
    hs = [jnp.concatenate([hs_scr[:, half * SH:(half + 1) * SH],
                           hs_scr[:, SL + half * SH:SL + (half + 1) * SH]], axis=1).astype(k_ref.dtype)
          for half in range(2)]
    for t2 in range(L):
        acc = jnp.concatenate([_mm(hs[0], q_ref[t2, 0]), _mm(hs[1], q_ref[t2, 1])], axis=1)
        for t in range(t2 + 1):
            acc = acc + _mm(u_ref[:, t * W:(t + 1) * W], k_ref[t2 - t])
        y_ref[:, t2 * W:(t2 + 1) * W] = acc


def _s5(ucat, kbd, pmat, qmat, a2, h0, bt, nc):
    rows_all, lw = ucat.shape
    L = lw // SSM_WIDTH
    rows = bt * nc
    once = lambda a: pl.BlockSpec(a.shape, lambda i: (0,) * a.ndim, pipeline_mode=pl.Buffered(1))
    return pl.pallas_call(
        functools.partial(_s5_body, L=L, bt=bt, nc=nc),
        grid=(rows_all // rows,),
        in_specs=[pl.BlockSpec((rows, lw), lambda i: (i, 0)), once(kbd), once(pmat), once(qmat), once(a2),
                  pl.BlockSpec((1, bt, 2 * S5_STATE_LANES), lambda i: (i, 0, 0))],
        out_specs=[pl.BlockSpec((rows, lw), lambda i: (i, 0)),
                   pl.BlockSpec((1, bt, 2 * S5_STATE_LANES), lambda i: (i, 0, 0))],
        out_shape=[jax.ShapeDtypeStruct((rows_all, lw), F32), jax.ShapeDtypeStruct(h0.shape, F32)],
        scratch_shapes=[pltpu.VMEM((rows, 2 * S5_STATE_LANES), F32), pltpu.VMEM((rows, 2 * S5_STATE_LANES), F32)],
        compiler_params=_cparams(("parallel",)),
        name="s5",
    )(ucat, kbd, pmat, qmat, a2, h0)


def _s5_weights(lam_re, lam_im, log_dt, b_re, b_im, c_re, c_im, L, wdtype):
    G, P, J = SSM_GROUPS, SSM_STATE, SSM_GROUP
    lr, li = lam_re.astype(F32), lam_im.astype(F32)
    dt = jnp.exp(log_dt.astype(F32))[:, None]
    mag = jnp.exp(lr * dt)
    ar = mag * jnp.cos(li * dt)
    ai = mag * jnp.sin(li * dt)
    den = lr * lr + li * li
    fr = ((ar - 1.0) * lr + ai * li) / den
    fi = (ai * lr - (ar - 1.0) * li) / den
    br, bi = b_re.astype(F32), b_im.astype(F32)
    bbr = fr[..., None] * br - fi[..., None] * bi
    bbi = fr[..., None] * bi + fi[..., None] * br
    tau = jnp.arange(L + 1, dtype=F32)[:, None, None]
    pmag = jnp.exp(lr * dt * tau)
    pr = pmag * jnp.cos(li * dt * tau)
    pi = pmag * jnp.sin(li * dt * tau)
    t1r = pr[..., None] * bbr - pi[..., None] * bbi
    t1i = pr[..., None] * bbi + pi[..., None] * bbr
    cr, ci = c_re.astype(F32), c_im.astype(F32)
    kt = (jnp.einsum('gip,tgpj->tgij', cr, t1r, precision=HI)
          - jnp.einsum('gip,tgpj->tgij', ci, t1i, precision=HI))
    GH = G // 2

    def block_diag(a, rows_per_group, cols_per_group, groups):
        r = lax.broadcasted_iota(jnp.int32, a.shape[-2:], 0) // rows_per_group
        c = lax.broadcasted_iota(jnp.int32, a.shape[-2:], 1) // cols_per_group
        return jnp.where(r % groups == c % groups, a, 0.0)

    kji = kt[:L].transpose(0, 1, 3, 2).reshape(L, G * J, 1, J)
    kbd = block_diag(jnp.broadcast_to(kji, (L, G * J, G, J)).reshape(L, G * J, G * J), J, J, G)
    rev = L - 1 - jnp.arange(L)
    ph = jnp.stack([t1r[rev], t1i[rev]], axis=1).reshape(L, 2, 2, GH, P, J)
    ph = ph.transpose(0, 2, 3, 5, 1, 4).reshape(L, 2, GH * J, 2, 1, P)
    pmat = block_diag(jnp.broadcast_to(ph, (L, 2, GH * J, 2, GH, P)).reshape(L, 2, GH * J, 2 * GH * P), J, P, GH)
    car = cr[None] * pr[1:, :, None, :] - ci[None] * pi[1:, :, None, :]
    cai = cr[None] * pi[1:, :, None, :] + ci[None] * pr[1:, :, None, :]
    qh = jnp.stack([car, -cai], axis=1).reshape(L, 2, 2, GH, J, P)
    qh = qh.transpose(0, 2, 1, 3, 5, 4).reshape(L, 2, 2 * GH * P, 1, J)
    qmat = block_diag(jnp.broadcast_to(qh, (L, 2, 2 * GH * P, GH, J)).reshape(L, 2, 2 * GH * P, GH * J), P, J, GH)
    a_l = jnp.stack([jnp.concatenate([pr[L].reshape(-1), pr[L].reshape(-1)]),
                     jnp.concatenate([-pi[L].reshape(-1), pi[L].reshape(-1)])])
    return kbd.astype(wdtype), pmat.astype(wdtype), qmat.astype(wdtype), a_l


def _s5_mixer(ucat, h0_re, h0_im, wts, bt, nc):
    B = h0_re.shape[0]
    h0 = jnp.concatenate([h0_re.reshape(B, -1), h0_im.reshape(B, -1)], axis=-1).astype(F32)
    y, hf = _s5(ucat, *wts, h0.reshape(B // bt, bt, -1), bt, nc)
    hf = hf.reshape(B, 2, SSM_GROUPS, SSM_STATE)
    return y, hf[:, 0], hf[:, 1]


def _hgrn_body(hg_ref, s0_ref, lb_ref, ng_ref, y_ref, sf_ref,
               st_scr, k_scr, b_scr, v_scr, w_scr, *, tb, c, t_valid, precise):
    j = pl.program_id(1)
    W = HG_WIDTH
    mdt = F32 if precise else BF16

    hd = HG_HEAD_DIM

    @pl.when(j == 0)
    def _():
        st_scr[...] = jnp.zeros((W, W), F32)
        for hh in range(HG_HEADS):
            st_scr[hh * hd:(hh + 1) * hd, hh * hd:(hh + 1) * hd] = s0_ref[0, hh]
        k_scr[0:c, :] = jnp.zeros((c, W), F32)
        b_scr[0:c, :] = jnp.zeros((c, W), F32)
        v_scr[0:c, :] = jnp.zeros((c, W), F32)

    q = hg_ref[0, :, 0:W]
    lb = lb_ref[...]
    fv = lb + (1.0 - lb) * jax.nn.sigmoid(hg_ref[0, :, W:2 * W])
    logf = jnp.log(fv) * math.log2(math.e)
    kk = 1.0 - fv
    v = hg_ref[0, :, 2 * W:3 * W]
    row = lax.broadcasted_iota(jnp.int32, (tb, W), 0)
    if t_valid < tb:
        valid = row < t_valid
        logf = jnp.where(valid, logf, 0.0)
        kk = jnp.where(valid, kk, 0.0)
    ri = lax.broadcasted_iota(jnp.int32, (tb, tb), 0)
    ci = lax.broadcasted_iota(jnp.int32, (tb, tb), 1)
    tril = ((ri // c == ci // c) & (ci <= ri)).astype(BF16)
    b = _split_dot_lhs_exact(tril, logf)
    k_scr[c:c + tb, :] = kk
    b_scr[c:c + tb, :] = b
    v_scr[c:c + tb, :] = v
    rin = row % c
    for d in range(c):
        ksh = k_scr[c - d:c - d + tb, :]
        bsh = b_scr[c - d:c - d + tb, :]
        w = jnp.where(rin >= d, q * ksh * jnp.exp2(b - bsh), 0.0)
        w_scr[d * tb:(d + 1) * tb, :] = w.astype(mdt)
    ones_bd = _head_ones(W, HG_HEAD_DIM).astype(BF16)
    att = _mm(w_scr[...], ones_bd.astype(mdt))
    o = att[0:tb] * v
    for d in range(1, c):
        o = o + att[d * tb:(d + 1) * tb] * v_scr[c - d:c - d + tb, :]
    bd = _head_ones(W, HG_HEAD_DIM)
    outs = []
    for ch in range(tb // c):
        sl = slice(ch * c, (ch + 1) * c)
        bc = b[sl]
        bl = bc[c - 1:c, :]
        st = st_scr[...]
        outs.append(_mm(q[sl] * jnp.exp2(bc), st.astype(mdt), NT_DIMS))
        khat = (kk[sl] * jnp.exp2(bl - bc)).astype(mdt)
        upd = _mm(v[sl], khat, TN_DIMS)
        st_scr[...] = st * jnp.exp2(bl) + jnp.where(bd, upd, 0.0)
    o = o + jnp.concatenate(outs, axis=0) if len(outs) > 1 else o + outs[0]
    ms = _split_dot(o * o, ones_bd, 3 if precise else 2) * (1.0 / HG_HEAD_DIM)
    y = o * lax.rsqrt(ms + NORM_EPS) * ng_ref[...]
    y_ref[0] = y * jax.nn.silu(hg_ref[0, :, 3 * W:4 * W])
    @pl.when(j == pl.num_programs(1) - 1)
    def _():
        for hh in range(HG_HEADS):
            sf_ref[0, hh] = st_scr[hh * hd:(hh + 1) * hd, hh * hd:(hh + 1) * hd]


def _split_dot_lhs_exact(a_bf16, b):
    out = None
    rem = b
    for _ in range(3):
        piece = rem.astype(BF16)
        part = _dot(a_bf16, piece)
        out = part if out is None else out + part
        rem = rem - piece.astype(F32)
    return out


def _hgrn(hg, s0t, lb, ng, tb, c, t_valid, precise):
    B, T, _ = hg.shape
    W = HG_WIDTH
    state = pl.BlockSpec((1, HG_HEADS, HG_HEAD_DIM, HG_HEAD_DIM), lambda b, j: (b, 0, 0, 0))
    return pl.pallas_call(
        functools.partial(_hgrn_body, tb=tb, c=c, t_valid=t_valid, precise=precise),
        grid=(B, T // tb),
        in_specs=[pl.BlockSpec((1, tb, 4 * W), lambda b, j: (b, j, 0)), state,
                  pl.BlockSpec((1, W), lambda b, j: (0, 0)),
                  pl.BlockSpec((1, W), lambda b, j: (0, 0))],
        out_specs=[pl.BlockSpec((1, tb, W), lambda b, j: (b, j, 0)), state],
        out_shape=[jax.ShapeDtypeStruct((B, T, W), F32), jax.ShapeDtypeStruct(s0t.shape, F32)],
        scratch_shapes=[pltpu.VMEM((W, W), F32), pltpu.VMEM((c + tb, W), F32),
                        pltpu.VMEM((c + tb, W), F32), pltpu.VMEM((c + tb, W), F32),
                        pltpu.VMEM((c * tb, W), F32 if precise else BF16)],
        compiler_params=_cparams(("parallel", "arbitrary")),
        name="hgrn2",
    )(hg, s0t, lb, ng)


def _hgrn_state_to_t(s):
    return jnp.swapaxes(s, -1, -2)


_hgrn_state_from_t = _hgrn_state_to_t


def _lru_body(x_ref, c0_ref, h0_ref, cw_ref, cb_ref, wax_ref, bax_ref, nsp_ref, y_ref, hl_ref,
              xs_scr, hc_scr, *, tb, r_last):
    j = pl.program_id(1)
    W = LRU_WIDTH

    @pl.when(j == 0)
    def _():
        xs_scr[0:8, :] = c0_ref[0]
        hc_scr[...] = h0_ref[0]

    x = x_ref[0, :, 0:W]
    xs_scr[8:8 + tb, :] = x
    xc = cb_ref[...] + x * cw_ref[3:4, :]
    for jj in range(CONV_WIDTH - 1):
        xc = xc + xs_scr[5 + jj:5 + jj + tb, :] * cw_ref[jj:jj + 1, :]
    tail = xs_scr[tb:tb + 8, :]
    xs_scr[0:8, :] = tail
    rg = _mm(xc, wax_ref[...]) + bax_ref[...]
    r = jax.nn.sigmoid(rg[:, 0:W])
    ig = jax.nn.sigmoid(rg[:, W:2 * W])
    log_a = nsp_ref[...] * r
    a = jnp.exp(log_a)
    bt = jnp.sqrt(jnp.maximum(1.0 - a * a, 0.0)) * (ig * xc)
    row = lax.broadcasted_iota(jnp.int32, (tb, W), 0) % SUBLANES
    k = 1
    while k < SUBLANES:
        keep = row >= k
        a_sh = jnp.where(keep, pltpu.roll(a, k, 0), 1.0)
        b_sh = jnp.where(keep, pltpu.roll(bt, k, 0), 0.0)
        bt = a * b_sh + bt
        a = a * a_sh
        k *= 2
    carry = hc_scr[...]
    groups = []
    for g0 in range(0, tb, SUBLANES):
        hg = a[g0:g0 + SUBLANES] * carry + bt[g0:g0 + SUBLANES]
        groups.append(hg)
        carry = hg[SUBLANES - 1:SUBLANES]
    h = jnp.concatenate(groups, axis=0)
    y_ref[0] = jax.nn.gelu(x_ref[0, :, W:2 * W]) * h
    hc = h[r_last:r_last + 1, :]
    hc_scr[...] = hc
    hl_ref[0] = hc


def _lru(xg, c0, h0, cw, cb, wax, bax, nsp, tb, t_valid):
    B, T, _ = xg.shape
    W = LRU_WIDTH
    fixed = lambda b, j: (0, 0)
    return pl.pallas_call(
        functools.partial(_lru_body, tb=tb, r_last=(t_valid - 1) % tb),
        grid=(B, T // tb),
        in_specs=[pl.BlockSpec((1, tb, 2 * W), lambda b, j: (b, j, 0)),
                  pl.BlockSpec((1, 8, W), lambda b, j: (b, 0, 0)),
                  pl.BlockSpec((1, 1, W), lambda b, j: (b, 0, 0)),
                  pl.BlockSpec((CONV_WIDTH, W), fixed), pl.BlockSpec((1, W), fixed),
                  pl.BlockSpec((W, 2 * W), fixed), pl.BlockSpec((1, 2 * W), fixed),
                  pl.BlockSpec((1, W), fixed)],
        out_specs=[pl.BlockSpec((1, tb, W), lambda b, j: (b, j, 0)),
                   pl.BlockSpec((1, 1, W), lambda b, j: (b, 0, 0))],
        out_shape=[jax.ShapeDtypeStruct((B, T, W), F32), jax.ShapeDtypeStruct((B, 1, W), F32)],
        scratch_shapes=[pltpu.VMEM((8 + tb, W), F32), pltpu.VMEM((1, W), F32)],
        compiler_params=_cparams(("parallel", "arbitrary")),
        name="rglru",
    )(xg, c0, h0, cw, cb, wax, bax, nsp)


def _attn_body(lam_ref, q_ref, k_ref, vt_ref, g_ref, o_ref, q2_scr, m_scr, l_scr, acc_scr,
               *, tq, wide_units, out_scale):
    qi = pl.program_id(1)
    hw = DA_V_DIM
    lane = lax.broadcasted_iota(jnp.int32, (tq, hw), 1)
    for h in range(DA_HEADS):
        qh = q_ref[0, :, h * hw:(h + 1) * hw]
        zero = jnp.zeros_like(qh)
        q2_scr[h, 0:tq, :] = jnp.where(lane < DA_HEAD_DIM, qh, zero)
        q2_scr[h, tq:2 * tq, :] = jnp.where(lane >= DA_HEAD_DIM, qh, zero)
    m_scr[...] = jnp.full(m_scr.shape, MASK_VALUE, F32)
    l_scr[...] = jnp.zeros(l_scr.shape, F32)
    acc_scr[...] = jnp.zeros(acc_scr.shape, F32)

    def block(u0, nu, diagonal):
        tk = nu * KV_UNIT
        r0 = pl.multiple_of(u0 * KV_UNIT, KV_UNIT)
        for h in range(DA_HEADS):
            cs = slice(h * hw, (h + 1) * hw)
            st = _dot_nt(k_ref[0, pl.ds(r0, tk), cs], q2_scr[h])
            if diagonal:
                keyi = lax.broadcasted_iota(jnp.int32, (tk, 2 * tq), 0)
                qryi = lax.broadcasted_iota(jnp.int32, (tk, 2 * tq), 1) % tq
                st = jnp.where(keyi <= qryi, st, MASK_VALUE)
            m = m_scr[h]
            m_new = jnp.maximum(m, jnp.max(st, axis=0, keepdims=True))
            alpha = jnp.exp2(m - m_new)
            p = jnp.exp2(st - m_new)
            l_scr[h] = alpha * l_scr[h] + jnp.sum(p, axis=0, keepdims=True)
            pb = p.astype(BF16)
            pv = _dot(vt_ref[u0, cs, :], pb[0:KV_UNIT])
            for u in range(1, nu):
                pv = pv + _dot(vt_ref[u0 + u, cs, :], pb[u * KV_UNIT:(u + 1) * KV_UNIT])
            acc_scr[h] = alpha * acc_scr[h] + pv
            m_scr[h] = m_new

    nq = tq // KV_UNIT
    n_before = qi * nq
    done = 0
    width = wide_units
    while width >= 1:
        n_blocks = (n_before - done) // width

        def step(jb, carry, width=width, done=done):
            block(done + jb * width, width, False)
            return carry

        lax.fori_loop(0, n_blocks, step, 0)
        done = done + n_blocks * width
        width //= 2
    block(n_before, nq, True)
    lam = lam_ref[0]
    for h in range(DA_HEADS):
        on = acc_scr[h] / l_scr[h]
        o = (on[:, 0:tq] - lam * on[:, tq:2 * tq]).T
        o_ref[0, :, h * hw:(h + 1) * hw] = _rms(o, g_ref[...]) * out_scale


def _attn(lam, q, k, vt, g, tq, wide_units, out_scale):
    B, T, Wd = q.shape
    hw = DA_V_DIM
    units = T // KV_UNIT
    return pl.pallas_call(
        functools.partial(_attn_body, tq=tq, wide_units=wide_units, out_scale=out_scale),
        grid=(B, T // tq),
        in_specs=[pl.BlockSpec(memory_space=pltpu.SMEM),
                  pl.BlockSpec((1, tq, Wd), lambda b, i: (b, i, 0)),
                  pl.BlockSpec((1, T, Wd), lambda b, i: (b, 0, 0)),
                  pl.BlockSpec((units, Wd, KV_UNIT), lambda b, i: (b, 0, 0)),
                  pl.BlockSpec((1, hw), lambda b, i: (0, 0))],
        out_specs=pl.BlockSpec((1, tq, Wd), lambda b, i: (b, i, 0)),
        out_shape=jax.ShapeDtypeStruct((B, T, Wd), F32),
        scratch_shapes=[pltpu.VMEM((DA_HEADS, 2 * tq, hw), BF16), pltpu.VMEM((DA_HEADS, 1, 2 * tq), F32),
                        pltpu.VMEM((DA_HEADS, 1, 2 * tq), F32), pltpu.VMEM((DA_HEADS, hw, 2 * tq), F32)],
        compiler_params=_cparams(("parallel", "arbitrary")),
        name="diff_attn",
    )(lam, q, k, vt, g)


def _dec_body(pt_ref, lam_ref, q_ref, kn_ref, vn_ref, g_ref, *rest, pp, t_new, out_scale):
    k_refs = rest[0:pp]
    v_refs = rest[pp:2 * pp]
    o_ref = rest[2 * pp]
    m_scr, l_scr, acc_scr, bias_scr = rest[2 * pp + 1:]
    j = pl.program_id(1)
    nrow = 2 * DA_HEADS * t_new

    @pl.when(j == 0)
    def _():
        m_scr[...] = jnp.full(m_scr.shape, MASK_VALUE, F32)
        l_scr[...] = jnp.zeros(l_scr.shape, F32)
        acc_scr[...] = jnp.zeros(acc_scr.shape, F32)
        ncol = PAGE_SIZE * DA_HEADS
        rh = (lax.broadcasted_iota(jnp.int32, (nrow, ncol), 0) // t_new) % DA_HEADS
        chd = lax.broadcasted_iota(jnp.int32, (nrow, ncol), 1) % DA_HEADS
        bias_scr[...] = jnp.where(rh == chd, 0.0, MASK_VALUE)

    def hi_lo(a):
        hi = a.astype(BF16)
        return jnp.concatenate([hi, (a - hi.astype(F32)).astype(BF16)], axis=0)

    def fold(a):
        return a[0:nrow] + a[nrow:2 * nrow]

    q = hi_lo(q_ref[0])

    def split(a):
        hi = a.astype(BF16)
        return hi, (a - hi.astype(F32)).astype(BF16)

    def scores(keys):
        k_hi, k_lo = split(keys)
        return fold(_dot_nt(q, k_hi) + _dot_nt(q, k_lo))

    def update(ss, vals):
        m = m_scr[...]
        smax = ss[0]
        for s in ss[1:]:
            smax = jnp.maximum(smax, s)
        m_new = jnp.maximum(m, jnp.max(smax, axis=1, keepdims=True))
        alpha = jnp.exp2(m - m_new)
        ps = [jnp.exp2(s - m_new) for s in ss]
        psum = ps[0]
        for p in ps[1:]:
            psum = psum + p
        pv = None
        for p, vv in zip(ps, vals):
            p2 = hi_lo(p)
            part = _dot(p2, vv[0]) + _dot(p2, vv[1])
            pv = part if pv is None else pv + part
        l_scr[...] = alpha * l_scr[...] + jnp.sum(psum, axis=1, keepdims=True)
        acc_scr[...] = alpha * acc_scr[...] + fold(pv)
        m_scr[...] = m_new

    bias = bias_scr[...]
    update([scores(k_refs[i][...]) + bias for i in range(pp)],
           [split(v_refs[i][...]) for i in range(pp)])

    @pl.when(j == pl.num_programs(1) - 1)
    def _():
        nn = kn_ref.shape[1]
        r = lax.broadcasted_iota(jnp.int32, (nrow, nn), 0)
        cidx = lax.broadcasted_iota(jnp.int32, (nrow, nn), 1)
        ok = ((r // t_new) % DA_HEADS == cidx % DA_HEADS) & (cidx // DA_HEADS <= r % t_new)
        update([jnp.where(ok, scores(kn_ref[0]), MASK_VALUE)], [split(vn_ref[0])])
        on = acc_scr[...] / l_scr[...]
        half = nrow // 2
        o = on[0:half] - lam_ref[0] * on[half:nrow]
        o_ref[0] = _rms(o, g_ref[...]) * out_scale


def _dec_attn(pt, lam, q2, kn, vn, g, ck, cv, layer, pp, t_new, out_scale):
    B, nrow, hw = q2.shape
    n_pages = pt.shape[0] // B
    nn = kn.shape[1]
    rows = PAGE_SIZE * DA_HEADS

    def page_spec(i):
        return pl.BlockSpec((None, None, rows, hw),
                            lambda b, j, pt_ref: (layer, pt_ref[b * n_pages + j * pp + i], 0, 0))

    grid_spec = pltpu.PrefetchScalarGridSpec(
        num_scalar_prefetch=1,
        grid=(B, n_pages // pp),
        in_specs=[pl.BlockSpec(memory_space=pltpu.SMEM),
                  pl.BlockSpec((1, nrow, hw), lambda b, j, pt_ref: (b, 0, 0)),
                  pl.BlockSpec((1, nn, hw), lambda b, j, pt_ref: (b, 0, 0)),
                  pl.BlockSpec((1, nn, hw), lambda b, j, pt_ref: (b, 0, 0)),
                  pl.BlockSpec((1, hw), lambda b, j, pt_ref: (0, 0))]
                 + [page_spec(i) for i in range(pp)] + [page_spec(i) for i in range(pp)],
        out_specs=pl.BlockSpec((1, nrow // 2, hw), lambda b, j, pt_ref: (b, 0, 0)),
        scratch_shapes=[pltpu.VMEM((nrow, 1), F32), pltpu.VMEM((nrow, 1), F32),
                        pltpu.VMEM((nrow, hw), F32), pltpu.VMEM((nrow, rows), F32)],
    )
    return pl.pallas_call(
        functools.partial(_dec_body, pp=pp, t_new=t_new, out_scale=out_scale),
        grid_spec=grid_spec,
        out_shape=jax.ShapeDtypeStruct((B, nrow // 2, hw), F32),
        compiler_params=_cparams(("parallel", "arbitrary")),
        name="paged_diff_attn",
    )(pt, lam, q2, kn, vn, g, *([ck] * pp), *([cv] * pp))


def _merge_body(x_ref, g_ref, ya_ref, ua_ref, yb_ref, yc_ref, yd_ref, d_ref, wglu_ref,
                wgt_ref, wa_ref, wb_ref, wc_ref, wd_ref, wo_ref, o_ref, ylo_scr, yhi_scr):
    x = x_ref[...]
    h = _rms(x, g_ref[...]).astype(wgt_ref.dtype)
    chunk = ya_ref.shape[1] // SSM_WIDTH
    half = SSM_WIDTH // 2
    for t in range(chunk):
        rows = pl.ds(t, ya_ref.shape[0], stride=chunk)
        ylo_scr[rows, :] = ya_ref[:, t * SSM_WIDTH:t * SSM_WIDTH + half]
        yhi_scr[rows, :] = ya_ref[:, t * SSM_WIDTH + half:(t + 1) * SSM_WIDTH]
    ya_tok = jnp.concatenate([ylo_scr[...], yhi_scr[...]], axis=1)
    z = jax.nn.gelu(ya_tok + d_ref[...] * ua_ref[...])
    ya = z * jax.nn.sigmoid(_mm(z, wglu_ref[...]))
    merged = None
    branches = ((ya, wa_ref), (yb_ref[...], wb_ref), (yc_ref[...], wc_ref), (yd_ref[...], wd_ref))
    for i, (yv, w_ref) in enumerate(branches):
        gate = jax.nn.sigmoid(_mm(h, wgt_ref[:, i * D_MODEL:(i + 1) * D_MODEL]))
        term = gate * _mm(yv, w_ref[...])
        merged = term if merged is None else merged + term
    o_ref[...] = x + _mm(merged, wo_ref[...])


def _merge(x, g, ya, ua, yb, yc, yd, d, wglu, wgt, wa, wb, wc, wd, wo, tm):
    n = x.shape[0]
    row = lambda i: (i, 0)
    fixed = lambda i: (0, 0)
    full = lambda a: pl.BlockSpec(a.shape, fixed, pipeline_mode=pl.Buffered(1))
    return pl.pallas_call(
        _merge_body,
        grid=(n // tm,),
        in_specs=[pl.BlockSpec((tm, D_MODEL), row), full(g),
                  pl.BlockSpec((tm * ya.shape[0] // n, ya.shape[1]), row), pl.BlockSpec((tm, 256), row),
                  pl.BlockSpec((tm, 256), row), pl.BlockSpec((tm, 256), row),
                  pl.BlockSpec((tm, 512), row),
                  full(d), full(wglu), full(wgt), full(wa), full(wb), full(wc), full(wd), full(wo)],
        out_specs=pl.BlockSpec((tm, D_MODEL), row),
        out_shape=jax.ShapeDtypeStruct((n, D_MODEL), F32),
        scratch_shapes=[pltpu.VMEM((tm, SSM_WIDTH // 2), F32), pltpu.VMEM((tm, SSM_WIDTH // 2), F32)],
        compiler_params=_cparams(("parallel",)),
        name="merge",
    )(x, g, ya, ua, yb, yc, yd, d, wglu, wgt, wa, wb, wc, wd, wo)


def _moe_body(x_ref, g_ref, wr_ref, br_ref, wg_ref, wu_ref, wd_ref, gf_ref, o_ref,
              h_scr, gate_scr, acc_scr, *, final_norm):
    gi = pl.program_id(1)
    tm = x_ref.shape[0]
    R = ROUTER_LANES

    @pl.when(gi == 0)
    def _():
        h = _rms(x_ref[...], g_ref[...]).astype(h_scr.dtype)
        h_scr[...] = h
        logits = _mm(h, wr_ref[...]) + br_ref[...]
        lane = lax.broadcasted_iota(jnp.int32, (tm, R), 1)
        lanef = lane.astype(F32)
        neg = -jnp.inf
        is_g = lane < MOE_GROUPS
        glm = jnp.where(is_g, logits, neg)
        gmax = jnp.max(glm, axis=1, keepdims=True)
        gsum = jnp.sum(jnp.where(is_g, jnp.exp(glm - gmax), 0.0), axis=1, keepdims=True)
        g_w = 1.0 / gsum
        g_i = jnp.min(jnp.where(glm == gmax, lanef, float(R)), axis=1, keepdims=True)
        e_grp = ((lane - MOE_GROUPS) // MOE_PER_GROUP).astype(F32)
        sel = (lane >= MOE_GROUPS) & (lane < MOE_GROUPS + MOE_EXPERTS) & (e_grp == g_i)
        elm = jnp.where(sel, logits, neg)
        e1 = jnp.max(elm, axis=1, keepdims=True)
        i1 = jnp.min(jnp.where(elm == e1, lanef, float(R)), axis=1, keepdims=True)
        elm2 = jnp.where(lanef == i1, neg, elm)
        e2 = jnp.max(elm2, axis=1, keepdims=True)
        i2 = jnp.min(jnp.where(elm2 == e2, lanef, float(R)), axis=1, keepdims=True)
        t = jnp.exp(e2 - e1)
        w1 = g_w / (1.0 + t)
        w2 = g_w * t / (1.0 + t)
        gates = jnp.where(lanef == i1, w1, 0.0) + jnp.where(lanef == i2, w2, 0.0)
        for grp in range(MOE_GROUPS):
            gate_scr[grp] = pltpu.roll(gates, R - MOE_GROUPS - MOE_PER_GROUP * grp, 1)
        acc_scr[...] = jnp.zeros(acc_scr.shape, F32)

    h = h_scr[...]
    gates = gate_scr[gi]
    if wg_ref.ndim == 3:
        out = None
        for e in range(MOE_PER_GROUP):
            hid = jax.nn.silu(_mm(h, wg_ref[e])) * _mm(h, wu_ref[e]) * gates[:, e:e + 1]
            part = _mm(hid, wd_ref[e])
            out = part if out is None else out + part
        acc_scr[...] += out
    else:
        hid = jax.nn.silu(_mm(h, wg_ref[...])) * _mm(h, wu_ref[...])
        hid = jnp.concatenate([hid[:, e * MOE_HIDDEN:(e + 1) * MOE_HIDDEN] * gates[:, e:e + 1]
                               for e in range(MOE_PER_GROUP)], axis=1)
        acc_scr[...] += _mm(hid, wd_ref[...])

    @pl.when(gi == MOE_GROUPS - 1)
    def _():
        o = x_ref[...] + acc_scr[...]
        if final_norm:
            o = _rms(o, gf_ref[...])
        o_ref[...] = o


def _moe(x, g, wr, br, wg, wu, wd, gf, tm, final_norm, layer):
    n = x.shape[0]
    gw = MOE_PER_GROUP * MOE_HIDDEN
    row = lambda i, e: (i, 0)
    fixed = lambda i, e: (0, 0)
    if wg.ndim == 3:
        per_expert = lambda i, e: (layer * MOE_GROUPS + e, 0, 0)
        w_specs = [pl.BlockSpec((MOE_PER_GROUP, D_MODEL, MOE_HIDDEN), per_expert),
                   pl.BlockSpec((MOE_PER_GROUP, D_MODEL, MOE_HIDDEN), per_expert),
                   pl.BlockSpec((MOE_PER_GROUP, MOE_HIDDEN, D_MODEL), per_expert)]
    else:
        w_specs = [pl.BlockSpec((D_MODEL, gw), lambda i, e: (0, e)), pl.BlockSpec((D_MODEL, gw), lambda i, e: (0, e)),
                   pl.BlockSpec((gw, D_MODEL), lambda i, e: (e, 0))]
    return pl.pallas_call(
        functools.partial(_moe_body, final_norm=final_norm),
        grid=(n // tm, MOE_GROUPS),
        in_specs=[pl.BlockSpec((tm, D_MODEL), row), pl.BlockSpec((1, D_MODEL), fixed),
                  pl.BlockSpec((D_MODEL, ROUTER_LANES), fixed), pl.BlockSpec((1, ROUTER_LANES), fixed),
                  *w_specs,
                  pl.BlockSpec((1, D_MODEL), fixed)],
        out_specs=pl.BlockSpec((tm, D_MODEL), row),
        out_shape=jax.ShapeDtypeStruct((n, D_MODEL), F32),
        scratch_shapes=[pltpu.VMEM((tm, D_MODEL), wg.dtype), pltpu.VMEM((MOE_GROUPS, tm, ROUTER_LANES), F32),
                        pltpu.VMEM((tm, D_MODEL), F32)],
        compiler_params=_cparams(("parallel", "arbitrary")),
        name="moe",
    )(x, g, wr, br, wg, wu, wd, gf)


def _rope_tables(pos):
    half = DA_HEAD_DIM // 2
    inv = 1.0 / (ROPE_THETA ** (jnp.arange(half, dtype=F32) * 2.0 / DA_HEAD_DIM))
    ang = pos.astype(F32)[:, None] * inv[None, :]
    reps = DA_QK_WIDTH // half
    return jnp.tile(jnp.cos(ang), (1, reps)), jnp.tile(jnp.sin(ang), (1, reps))


def _block_diag(w):
    n, a, b = w.shape
    eye = jnp.eye(n, dtype=w.dtype)
    return jnp.einsum('nij,nm->nimj', w, eye).reshape(n * a, n * b)


def _layer_params(l, p, wdt):
    row = lambda a: a.astype(F32).reshape(1, -1)
    w_in = p['w_in'][l]
    p_lb = jax.nn.softmax(p['hg_lb_logits'].astype(F32), axis=0)
    lb = jnp.cumsum(p_lb, axis=0)[l] - p_lb[0]
    lam_init = 0.8 - 0.6 * math.exp(-0.3 * l)
    lam = (jnp.exp(jnp.sum(p['diff_lq1'][l].astype(F32) * p['diff_lk1'][l].astype(F32)))
           - jnp.exp(jnp.sum(p['diff_lq2'][l].astype(F32) * p['diff_lk2'][l].astype(F32))) + lam_init)
    w_router = jnp.concatenate(
        [p['moe_w_grp'][l], p['moe_w_exp'][l],
         jnp.zeros((D_MODEL, ROUTER_LANES - MOE_GROUPS - MOE_EXPERTS), F32)], axis=1)
    b_router = jnp.concatenate(
        [p['moe_b_grp'][l].astype(F32), p['moe_b_exp'][l].astype(F32),
         jnp.zeros((ROUTER_LANES - MOE_GROUPS - MOE_EXPERTS,), F32)]).reshape(1, -1)
    eh = MOE_EXPERTS * MOE_HIDDEN
    return dict(
        norm_mix=row(p['norm_mix'][l]),
        wdt=wdt,
        w_mix=w_in[:, :MIX_COLS].astype(wdt),
        w_gates=w_in[:, MIX_COLS:].astype(wdt),
        s5=(p['ssm_lambda_re'][l], p['ssm_lambda_im'][l], p['ssm_log_dt'][l], p['ssm_b_re'][l],
            p['ssm_b_im'][l], p['ssm_c_re'][l], p['ssm_c_im'][l]),
        ssm_d=row(p['ssm_d'][l]),
        w_glu=p['ssm_w_glu'][l].astype(wdt),
        hg_lb=lb.reshape(1, -1),
        hg_norm=jnp.tile(p['hg_norm'][l].astype(F32), HG_HEADS).reshape(1, -1),
        conv_w=p['lru_conv_w'][l].astype(F32),
        conv_b=row(p['lru_conv_b'][l]),
        w_ax=jnp.concatenate([_block_diag(p['lru_wa'][l]), _block_diag(p['lru_wx'][l])], axis=1).astype(wdt),
        b_ax=jnp.concatenate([p['lru_ba'][l], p['lru_bx'][l]]).astype(F32).reshape(1, -1),
        neg_c_softplus=(-LRU_C * jax.nn.softplus(-p['lru_lambda'][l].astype(F32))).reshape(1, -1),
        lam=lam.reshape(1).astype(F32),
        out_scale=1.0 - lam_init,
        diff_norm=row(p['diff_norm'][l]),
        w_br_a=p['w_br_a'][l].astype(wdt), w_br_b=p['w_br_b'][l].astype(wdt),
        w_br_c=p['w_br_c'][l].astype(wdt), w_br_d=p['w_br_d'][l].astype(wdt),
        w_out=p['w_out'][l].astype(wdt),
        norm_ffn=row(p['norm_ffn'][l]),
        w_router=w_router.astype(wdt), b_router=b_router,
        **(dict(moe_gate=p['moe_w_gate'].reshape(-1, D_MODEL, MOE_HIDDEN),
                moe_up=p['moe_w_up'].reshape(-1, D_MODEL, MOE_HIDDEN),
                moe_down=p['moe_w_down'].reshape(-1, MOE_HIDDEN, D_MODEL)) if wdt == F32 else
           dict(moe_gate=p['moe_w_gate'][l].astype(wdt).transpose(1, 0, 2).reshape(D_MODEL, eh),
                moe_up=p['moe_w_up'][l].astype(wdt).transpose(1, 0, 2).reshape(D_MODEL, eh),
                moe_down=p['moe_w_down'][l].astype(wdt).reshape(eh, D_MODEL))),
    )


def _pad_rows(a, rows):
    return jnp.pad(a, ((0, 0), (0, rows - a.shape[1])) + ((0, 0),) * (a.ndim - 2))


def _trunk(x, pos0, states, cache, page_table, layers, norm_final, cfg):
    B, T, _ = x.shape
    n = B * T
    tm, s5_chunk, tpad, hg_tb, hg_c, lru_tb, tq = (cfg[k] for k in
                                                   ('tm', 's5_chunk', 'tpad', 'hg_tb', 'hg_c', 'lru_tb', 'tq'))
    cos, sin = _rope_tables(pos0 + jnp.arange(T, dtype=jnp.int32))
    if T % tm:
        cos, sin = jnp.tile(cos, (tm // T, 1)), jnp.tile(sin, (tm // T, 1))
    xf = x.reshape(n, D_MODEL)
    ks, vs, sts = [], [], []
    for l, lp in enumerate(layers):
        st = states[l]
        ua, hg, xg, qb, k, kb, v, vt, ucat = _inproj(xf, lp['norm_mix'], lp['w_mix'], cos, sin, tm, s5_chunk)
        ya, ssm_re, ssm_im = _s5_mixer(ucat, st[0], st[1], _s5_weights(*lp['s5'], s5_chunk, lp['wdt']),
                                       cfg['s5_bt'], T // s5_chunk)
        hg3 = _pad_rows(hg.reshape(B, T, 4 * HG_WIDTH), tpad)
        yb, s_t = _hgrn(hg3, _hgrn_state_to_t(st[2].astype(F32)), lp['hg_lb'], lp['hg_norm'],
                        hg_tb, hg_c, T, lp['wdt'] == F32)
        hg_state = _hgrn_state_from_t(s_t)
        xg3 = xg.reshape(B, T, 2 * LRU_WIDTH)
        conv0 = jnp.pad(st[4].astype(F32), ((0, 0), (8 - (CONV_WIDTH - 1), 0), (0, 0)))
        yc, lru_h = _lru(_pad_rows(xg3, tpad), conv0, st[3].astype(F32).reshape(B, 1, LRU_WIDTH),
                         lp['conv_w'], lp['conv_b'], lp['w_ax'], lp['b_ax'], lp['neg_c_softplus'],
                         lru_tb, T)
        xp = jnp.concatenate([st[4].astype(F32), xg3[:, :, :LRU_WIDTH]], axis=1)
        conv_buf = xp[:, T:]
        if cache is None:
            yd = _attn(lp['lam'], qb.reshape(B, T, -1), kb.reshape(B, T, -1), vt,
                       lp['diff_norm'], tq, cfg['wide_units'], lp['out_scale'])
        else:
            hw = DA_V_DIM
            q4 = qb.reshape(B, T, DA_HEADS, 2, DA_HEAD_DIM)
            zero = jnp.zeros_like(q4[:, :, :, 0])
            q2 = jnp.stack([jnp.concatenate([q4[:, :, :, 0], zero], -1),
                            jnp.concatenate([zero, q4[:, :, :, 1]], -1)], axis=1)
            q2 = q2.transpose(0, 1, 3, 2, 4).reshape(B, 2 * DA_HEADS * T, hw)
            nn = 128
            kn = _pad_rows(k.reshape(B, T * DA_HEADS, hw), nn)
            vn = _pad_rows(v.reshape(B, T * DA_HEADS, hw), nn)
            o = _dec_attn(page_table.reshape(-1), lp['lam'], q2, kn, vn, lp['diff_norm'],
                          cache[0], cache[1], l, cfg['pp'], T, lp['out_scale'])
            yd = o.reshape(B, DA_HEADS, T, hw).transpose(0, 2, 1, 3).reshape(B, T, DA_WIDTH)
        x1 = _merge(xf, lp['norm_mix'], ya, ua, yb[:, :T].reshape(n, -1),
                    yc[:, :T].reshape(n, -1), yd.reshape(n, -1), lp['ssm_d'], lp['w_glu'], lp['w_gates'],
                    lp['w_br_a'], lp['w_br_b'], lp['w_br_c'], lp['w_br_d'], lp['w_out'], tm)
        xf = _moe(x1, lp['norm_ffn'], lp['w_router'], lp['b_router'], lp['moe_gate'], lp['moe_up'],
                  lp['moe_down'], norm_final, cfg['tm_moe'], l == len(layers) - 1, l)
        ks.append(k.reshape(B, T, DA_HEADS, 2 * DA_HEAD_DIM))
        vs.append(v.reshape(B, T, DA_HEADS, DA_V_DIM))
        sts.append((ssm_re, ssm_im, hg_state, lru_h.reshape(B, LRU_WIDTH), conv_buf))
    stacked = [jnp.stack([s[j] for s in sts]) for j in range(5)]
    return xf.reshape(B, T, D_MODEL), jnp.stack(ks), jnp.stack(vs), stacked


PROMPT_CFG = dict(tm=512, tm_moe=1024, s5_chunk=16, s5_bt=1, tpad=2048, hg_tb=256, hg_c=16, lru_tb=512, tq=256,
                  wide_units=4)
SAMPLE_CFG = dict(tm=128, tm_moe=128, s5_chunk=4, s5_bt=32, tpad=16, hg_tb=16, hg_c=16, lru_tb=16, tq=0, pp=16)


def kernel(x_prompt, x_sample, cache_k, cache_v, page_table, state_ssm_re, state_ssm_im, state_hgrn,
           state_lru, state_conv, norm_mix, w_in, ssm_lambda_re, ssm_lambda_im, ssm_log_dt, ssm_b_re,
           ssm_b_im, ssm_c_re, ssm_c_im, ssm_d, ssm_w_glu, hg_lb_logits, hg_norm, lru_conv_w, lru_conv_b,
           lru_wa, lru_ba, lru_wx, lru_bx, lru_lambda, diff_lq1, diff_lk1, diff_lq2, diff_lk2, diff_norm,
           w_br_a, w_br_b, w_br_c, w_br_d, w_out, norm_ffn, moe_w_grp, moe_b_grp, moe_w_exp, moe_b_exp,
           moe_w_gate, moe_w_up, moe_w_down, norm_final):
    p = dict(norm_mix=norm_mix, w_in=w_in, ssm_lambda_re=ssm_lambda_re, ssm_lambda_im=ssm_lambda_im,
             ssm_log_dt=ssm_log_dt, ssm_b_re=ssm_b_re, ssm_b_im=ssm_b_im, ssm_c_re=ssm_c_re,
             ssm_c_im=ssm_c_im, ssm_d=ssm_d, ssm_w_glu=ssm_w_glu, hg_lb_logits=hg_lb_logits,
             hg_norm=hg_norm, lru_conv_w=lru_conv_w, lru_conv_b=lru_conv_b, lru_wa=lru_wa, lru_ba=lru_ba,
             lru_wx=lru_wx, lru_bx=lru_bx, lru_lambda=lru_lambda, diff_lq1=diff_lq1, diff_lk1=diff_lk1,
             diff_lq2=diff_lq2, diff_lk2=diff_lk2, diff_norm=diff_norm, w_br_a=w_br_a, w_br_b=w_br_b,
             w_br_c=w_br_c, w_br_d=w_br_d, w_out=w_out, norm_ffn=norm_ffn, moe_w_grp=moe_w_grp,
             moe_b_grp=moe_b_grp, moe_w_exp=moe_w_exp, moe_b_exp=moe_b_exp, moe_w_gate=moe_w_gate,
             moe_w_up=moe_w_up, moe_w_down=moe_w_down)
    layers = [_layer_params(l, p, BF16) for l in range(DEPTH)]
    layers_f32 = [_layer_params(l, p, F32) for l in range(DEPTH)]
    gf = norm_final.astype(F32).reshape(1, -1)
    Bp = x_prompt.shape[0]
    Bs = x_sample.shape[0]
    zero_states = [(jnp.zeros((Bp, SSM_GROUPS, SSM_STATE), F32), jnp.zeros((Bp, SSM_GROUPS, SSM_STATE), F32),
                    jnp.zeros((Bp, HG_HEADS, HG_HEAD_DIM, HG_HEAD_DIM), F32), jnp.zeros((Bp, LRU_WIDTH), F32),
                    jnp.zeros((Bp, CONV_WIDTH - 1, LRU_WIDTH), F32)) for _ in range(DEPTH)]
    y_p, k_p, v_p, st_p = _trunk(x_prompt, 0, zero_states, None, None, layers, gf, PROMPT_CFG)
    past_len = page_table.shape[1] * PAGE_SIZE
    sample_states = [(state_ssm_re[l], state_ssm_im[l], state_hgrn[l], state_lru[l], state_conv[l])
                     for l in range(DEPTH)]
    n_pool = cache_k.shape[1]
    rows = PAGE_SIZE * DA_HEADS
    cache = (cache_k.reshape(DEPTH, n_pool, rows, 2 * DA_HEAD_DIM), cache_v.reshape(DEPTH, n_pool, rows, DA_V_DIM))
    y_s, k_s, v_s, st_s = _trunk(x_sample, past_len, sample_states, cache, page_table, layers_f32, gf,
                                 SAMPLE_CFG)
    return (y_p, y_s, k_p, v_p, k_s, v_s,
            st_p[0], st_p[1], st_s[0], st_s[1], st_p[2], st_s[2], st_p[3], st_s[3], st_p[4], st_s[4])
```
